```python
import math
import jax, jax.numpy as jnp
from jax import lax
import numpy as np

D_MODEL = 2048
BATCH = 4
SEQ = 2048
DEPTH = 4
DEC_BATCH = 8
DEC_SEQ = 4
PAST_LEN = 16384
PAGE_SIZE = 128

N_MIXERS = 2
N_DIFF_LAYERS = (DEPTH + 1) // 2
N_GLA_LAYERS = DEPTH // 2

DIFF_HEADS = 8
DIFF_DH = D_MODEL // (2 * DIFF_HEADS)
DIFF_QK_W = DIFF_HEADS * 2 * DIFF_DH
DIFF_V_W = DIFF_HEADS * 2 * DIFF_DH
DIFF_IN_W = 2 * DIFF_QK_W + DIFF_V_W + D_MODEL

GLA_HEADS = 4
GLA_DK_W = D_MODEL // 2
GLA_DV_W = D_MODEL
GLA_DK = GLA_DK_W // GLA_HEADS
GLA_DV = GLA_DV_W // GLA_HEADS
GLA_GATE_RANK = 16
GLA_GATE_NORMALIZER = 16.0
GLA_CHUNK = 64
GLA_IN_W = 2 * GLA_DK_W + GLA_DV_W + D_MODEL + GLA_GATE_RANK

REL_BUCKETS = 32
REL_MAX_DIST = 128
Q_BLOCK = 128

NORM_EPS = 1e-5
DEEPNORM_ALPHA = (2 * DEPTH) ** 0.25
DEEPNORM_BETA = (8 * DEPTH) ** -0.25

kernel_name = 'diffattn_gla_deepnorm_decoder_step'


def _layernorm(x, g, b):
    xf = x.astype(jnp.float32)
    mu = jnp.mean(xf, axis=-1, keepdims=True)
    var = jnp.mean(jnp.square(xf - mu), axis=-1, keepdims=True)
    return ((xf - mu) * lax.rsqrt(var + NORM_EPS) * g.astype(jnp.float32) + b.astype(jnp.float32)).astype(x.dtype)


def _rmsnorm(x, g):
    xf = x.astype(jnp.float32)
    return xf * lax.rsqrt(jnp.mean(jnp.square(xf), axis=-1, keepdims=True) + NORM_EPS) * g.astype(jnp.float32)


def _t5_bucket(rel):
    n = jnp.maximum(-rel, 0)
    max_exact = REL_BUCKETS // 2
    nf = jnp.maximum(n, 1).astype(jnp.float32)
    large = max_exact + (jnp.log(nf / max_exact) / math.log(REL_MAX_DIST / max_exact)
                         * (REL_BUCKETS - max_exact)).astype(jnp.int32)
    large = jnp.minimum(large, REL_BUCKETS - 1)
    return jnp.where(n < max_exact, n, large)


def _gather_pages(pool, page_table):
    g = pool[page_table]
    return g.reshape(g.shape[0], g.shape[1] * g.shape[2], g.shape[3], g.shape[4])


def _diff_attention(q, qpos, k, v, kpos, rel_bias, lam):
    B, T = q.shape[0], q.shape[1]
    qb = min(Q_BLOCK, T)
    pad = (-T) % qb
    nb = (T + pad) // qb
    qs = jnp.pad(q, ((0, 0), (0, pad), (0, 0), (0, 0), (0, 0)))
    qs = qs.reshape(B, nb, qb, DIFF_HEADS, 2, DIFF_DH).swapaxes(0, 1)
    ps = jnp.pad(qpos, (0, pad), mode='edge').reshape(nb, qb)
    kf = k.astype(jnp.float32)
    vf = v.astype(jnp.float32)
    table = rel_bias.astype(jnp.float32)

    def block(args):
        qblk, pblk = args
        logits = jnp.einsum('bqhmd,bkhmd->bhmqk', qblk.astype(jnp.float32), kf)
        rel = kpos[None, :] - pblk[:, None]
        bias = jnp.moveaxis(table[_t5_bucket(rel)], -1, 0)
        logits = jnp.where((rel <= 0)[None, None, None], logits + bias[None, :, None], -jnp.inf)
        p = jax.nn.softmax(logits, axis=-1)
        pd = p[:, :, 0] - lam * p[:, :, 1]
        return jnp.einsum('bhqk,bkhe->bqhe', pd, vf)

    o = lax.map(block, (qs, ps))
    return o.swapaxes(0, 1).reshape(B, nb * qb, DIFF_HEADS, 2 * DIFF_DH)[:, :T]


def _gla_chunked(q, k, v, g, s0):
    B, T, H = q.shape[0], q.shape[1], q.shape[2]
    C = min(GLA_CHUNK, T)
    pad = (-T) % C
    N = (T + pad) // C

    def chunks(a):
        a = jnp.pad(a.astype(jnp.float32), ((0, 0), (0, pad), (0, 0), (0, 0)))
        return a.reshape(B, N, C, H, a.shape[-1])

    q, k, v, g = chunks(q), chunks(k), chunks(v), chunks(g)
    b = jnp.cumsum(g, axis=2)
    qe = q * jnp.exp(b)
    ke = k * jnp.exp(-b)
    causal = jnp.tril(jnp.ones((C, C), dtype=bool))
    A = jnp.where(causal, jnp.einsum('bnihk,bnjhk->bnhij', qe, ke), 0.0)
    o_intra = jnp.einsum('bnhij,bnjhv->bnihv', A, v)
    b_last = b[:, :, -1]
    kd = k * jnp.exp(b_last[:, :, None] - b)

    def step(S, xs):
        qe_n, kd_n, v_n, bl_n = xs
        o_n = jnp.einsum('bchk,bhkv->bchv', qe_n, S)
        S = jnp.exp(bl_n)[..., None] * S + jnp.einsum('bchk,bchv->bhkv', kd_n, v_n)
        return S, o_n

    xs = (jnp.moveaxis(qe, 1, 0), jnp.moveaxis(kd, 1, 0), jnp.moveaxis(v, 1, 0), jnp.moveaxis(b_last, 1, 0))
    S, o_inter = lax.scan(step, s0.astype(jnp.float32), xs)
    o = o_intra + jnp.moveaxis(o_inter, 0, 1)
    return o.reshape(B, N * C, H, v.shape[-1])[:, :T], S


def _diff_layer(x, past, rel_bias, w_in, lam_p, norm_g, w_out, lam_init):
    B, T, _ = x.shape
    h = x @ w_in
    q, k, v, z = jnp.split(h, [DIFF_QK_W, 2 * DIFF_QK_W, 2 * DIFF_QK_W + DIFF_V_W], axis=-1)
    q = q.reshape(B, T, DIFF_HEADS, 2, DIFF_DH) * (DIFF_DH ** -0.5)
    k = k.reshape(B, T, DIFF_HEADS, 2 * DIFF_DH)
    v = v.reshape(B, T, DIFF_HEADS, 2 * DIFF_DH)
    if past is None:
        k_all, v_all = k, v
    else:
        k_all = jnp.concatenate([past[0].astype(k.dtype), k], axis=1)
        v_all = jnp.concatenate([past[1].astype(v.dtype), v], axis=1)
    S = k_all.shape[1]
    kpos = jnp.arange(S, dtype=jnp.int32)
    qpos = (S - T) + jnp.arange(T, dtype=jnp.int32)
    lp = lam_p.astype(jnp.float32)
    lam = jnp.exp(jnp.sum(lp[0] * lp[1])) - jnp.exp(jnp.sum(lp[2] * lp[3])) + lam_init
    o = _diff_attention(q, qpos, k_all.reshape(B, S, DIFF_HEADS, 2, DIFF_DH), v_all, kpos, rel_bias, lam)
    o = _rmsnorm(o, norm_g) * (1.0 - lam_init)
    o = o.reshape(B, T, D_MODEL).astype(x.dtype) * jax.nn.silu(z)
    return o @ w_out, k, v


def _gla_layer(x, s0, w_in, w_g2, b_g, norm_g, w_out):
    B, T, _ = x.shape
    h = x @ w_in
    q, k, v, z, gl = jnp.split(h, [GLA_DK_W, 2 * GLA_DK_W, 2 * GLA_DK_W + GLA_DV_W,
                                   2 * GLA_DK_W + GLA_DV_W + D_MODEL], axis=-1)
    g = jax.nn.log_sigmoid((gl @ w_g2 + b_g).astype(jnp.float32)) / GLA_GATE_NORMALIZER
    q = q.reshape(B, T, GLA_HEADS, GLA_DK) * (GLA_DK ** -0.5)
    k = k.reshape(B, T, GLA_HEADS, GLA_DK)
    v = v.reshape(B, T, GLA_HEADS, GLA_DV)
    g = g.reshape(B, T, GLA_HEADS, GLA_DK)
    o, s = _gla_chunked(q, k, v, g, s0)
    o = _rmsnorm(o, norm_g).reshape(B, T, D_MODEL).astype(x.dtype) * jax.nn.silu(z)
    return o @ w_out, s.astype(x.dtype)


def _trunk(x, cache_k, cache_v, state_gla, page_table, rel_bias, diff_w_in, diff_lambda, diff_norm_g,
           diff_w_out, gla_w_in, gla_w_g2, gla_b_g, gla_norm_g, gla_w_out, ln_g, ln_b):
    B = x.shape[0]
    k_rows, v_rows, states = [], [], []
    for i in range(DEPTH):
        j = i // N_MIXERS
        if i % N_MIXERS == 0:
            if cache_k is None:
                past = None
            else:
                past = (_gather_pages(cache_k[j], page_table), _gather_pages(cache_v[j], page_table))
            lam_init = 0.8 - 0.6 * math.exp(-0.3 * i)
            y, k_new, v_new = _diff_layer(x, past, rel_bias, diff_w_in[j], diff_lambda[j], diff_norm_g[j],
                                          diff_w_out[j], lam_init)
            k_rows.append(k_new)
            v_rows.append(v_new)
        else:
            if state_gla is None:
                s0 = jnp.zeros((B, GLA_HEADS, GLA_DK, GLA_DV), jnp.float32)
            else:
                s0 = state_gla[j]
            y, s = _gla_layer(x, s0, gla_w_in[j], gla_w_g2[j], gla_b_g[j], gla_norm_g[j], gla_w_out[j])
            states.append(s)
        x = _layernorm(DEEPNORM_ALPHA * x + y, ln_g[i], ln_b[i])
    return x, jnp.stack(k_rows), jnp.stack(v_rows), jnp.stack(states)


def setup_inputs(seed: int = 0) -> dict:
    key = jax.random.key(seed)
    ks = jax.random.split(key, 20)
    f32 = jnp.float32
    n_pages = PAST_LEN // PAGE_SIZE
    n_used = DEC_BATCH * n_pages
    n_pool = n_used + n_used // 4
    x_prompt = jax.random.normal(ks[0], (BATCH, SEQ, D_MODEL), f32)
    x_sample = jax.random.normal(ks[1], (DEC_BATCH, DEC_SEQ, D_MODEL), f32)
    cache_k = jax.random.normal(ks[2], (N_DIFF_LAYERS, n_pool, PAGE_SIZE, DIFF_HEADS, 2 * DIFF_DH), f32)
    cache_v = jax.random.normal(ks[3], (N_DIFF_LAYERS, n_pool, PAGE_SIZE, DIFF_HEADS, 2 * DIFF_DH), f32)
    state_gla = jax.random.normal(ks[4], (N_GLA_LAYERS, DEC_BATCH, GLA_HEADS, GLA_DK, GLA_DV), f32)
    page_table = jax.random.permutation(ks[5], n_pool)[:n_used].reshape(DEC_BATCH, n_pages).astype(jnp.int32)
    rel_bias = 0.5 * jax.random.normal(ks[6], (REL_BUCKETS, DIFF_HEADS), f32)
    diff_w_in = jax.random.normal(ks[7], (N_DIFF_LAYERS, D_MODEL, DIFF_IN_W), f32) * D_MODEL ** -0.5
    diff_lambda = 0.1 * jax.random.normal(ks[8], (N_DIFF_LAYERS, 4, DIFF_DH), f32)
    diff_norm_g = 1.0 + 0.05 * jax.random.normal(ks[9], (N_DIFF_LAYERS, 2 * DIFF_DH), f32)
    diff_w_out = jax.random.normal(ks[10], (N_DIFF_LAYERS, D_MODEL, D_MODEL), f32) * (D_MODEL ** -0.5 * DEEPNORM_BETA)
    gla_w_in = jax.random.normal(ks[11], (N_GLA_LAYERS, D_MODEL, GLA_IN_W), f32) * D_MODEL ** -0.5
    gla_w_g2 = jax.random.normal(ks[12], (N_GLA_LAYERS, GLA_GATE_RANK, GLA_DK_W), f32) * GLA_GATE_RANK ** -0.5
    gla_b_g = 0.1 * jax.random.normal(ks[13], (N_GLA_LAYERS, GLA_DK_W), f32)
    gla_norm_g = 1.0 + 0.05 * jax.random.normal(ks[14], (N_GLA_LAYERS, GLA_DV), f32)
    gla_w_out = jax.random.normal(ks[15], (N_GLA_LAYERS, D_MODEL, D_MODEL), f32) * (D_MODEL ** -0.5 * DEEPNORM_BETA)
    ln_g = 1.0 + 0.05 * jax.random.normal(ks[16], (DEPTH, D_MODEL), f32)
    ln_b = 0.02 * jax.random.normal(ks[17], (DEPTH, D_MODEL), f32)
    return {'x_prompt': x_prompt, 'x_sample': x_sample, 'cache_k': cache_k, 'cache_v': cache_v,
            'state_gla': state_gla, 'page_table': page_table, 'rel_bias': rel_bias,
            'diff_w_in': diff_w_in, 'diff_lambda': diff_lambda, 'diff_norm_g': diff_norm_g,
            'diff_w_out': diff_w_out, 'gla_w_in': gla_w_in, 'gla_w_g2': gla_w_g2, 'gla_b_g': gla_b_g,
            'gla_norm_g': gla_norm_g, 'gla_w_out': gla_w_out, 'ln_g': ln_g, 'ln_b': ln_b}


def reference(x_prompt, x_sample, cache_k, cache_v, state_gla, page_table, rel_bias, diff_w_in, diff_lambda,
              diff_norm_g, diff_w_out, gla_w_in, gla_w_g2, gla_b_g, gla_norm_g, gla_w_out, ln_g, ln_b):
    y_prompt, k_prompt, v_prompt, gla_prompt = _trunk(
        x_prompt, None, None, None, None, rel_bias, diff_w_in, diff_lambda, diff_norm_g, diff_w_out,
        gla_w_in, gla_w_g2, gla_b_g, gla_norm_g, gla_w_out, ln_g, ln_b)
    y_sample, k_sample, v_sample, gla_sample = _trunk(
        x_sample, cache_k, cache_v, state_gla, page_table, rel_bias, diff_w_in, diff_lambda, diff_norm_g,
        diff_w_out, gla_w_in, gla_w_g2, gla_b_g, gla_norm_g, gla_w_out, ln_g, ln_b)
    return (y_prompt, y_sample, k_prompt, v_prompt, gla_prompt, k_sample, v_sample, gla_sample)
```

```python
import functools
import math

import jax
import jax.numpy as jnp
import numpy as np
from jax import lax
from jax.experimental import pallas as pl
from jax.experimental.pallas import tpu as pltpu

D_MODEL = 2048
DEPTH = 4
PAGE_SIZE = 128

DIFF_HEADS = 8
DIFF_DH = D_MODEL // (2 * DIFF_HEADS)
DIFF_HW = 2 * DIFF_DH

GLA_HEADS = 4
GLA_DK_W = D_MODEL // 2
GLA_DK = GLA_DK_W // GLA_HEADS
GLA_DV = D_MODEL // GLA_HEADS
GLA_GATE_RANK = 16
GLA_GATE_NORMALIZER = 16.0
GLA_CHUNK = 64

REL_BUCKETS = 32
REL_MAX_DIST = 128

NORM_EPS = 1e-5
DEEPNORM_ALPHA = (2 * DEPTH) ** 0.25

LANES = 128
SAMPLE_ROWS = 16
VMEM_LIMIT = 48 * 1024 * 1024
MASK_VALUE = -1e30

ATT_BLOCK = 256
DEC_PAGES_PER_STEP = 4

_NT = (((1,), (1,)), ((), ()))
_TN = (((0,), (0,)), ((), ()))

bf16 = jnp.bfloat16
f32 = jnp.float32


def _params(*sem):
    return pltpu.CompilerParams(dimension_semantics=sem, vmem_limit_bytes=VMEM_LIMIT)


def _silu(z):
    return z * (1.0 / (1.0 + jnp.exp(-z)))


def _mm_kernel(x_ref, w_ref, o_ref, *, scale):
    acc = jnp.dot(x_ref[...], w_ref[...], preferred_element_type=f32)
    if scale != 1.0:
        acc = acc * scale
    o_ref[...] = acc.astype(o_ref.dtype)


def _matmul(x, w, col_off, n, out_dtype, scale=1.0):
    m, k = x.shape
    tm = min(m, 1024)
    tn = min(n, 1024)
    assert m % tm == 0 and n % tn == 0 and col_off % tn == 0
    off = col_off // tn
    return pl.pallas_call(
        functools.partial(_mm_kernel, scale=scale),
        grid=(n // tn, m // tm),
        in_specs=[pl.BlockSpec((tm, k), lambda j, i: (i, 0)),
                  pl.BlockSpec((k, tn), lambda j, i: (0, j + off))],
        out_specs=pl.BlockSpec((tm, tn), lambda j, i: (i, j)),
        out_shape=jax.ShapeDtypeStruct((m, n), out_dtype),
        compiler_params=_params("parallel", "parallel"),
        name="proj_matmul",
    )(x, w)


def _out_kernel(o_ref, w_ref, x_ref, g_ref, b_ref, xo_ref, xb_ref):
    y = jnp.dot(o_ref[...], w_ref[...], preferred_element_type=f32)
    r = DEEPNORM_ALPHA * x_ref[...] + y
    mu = jnp.mean(r, axis=-1, keepdims=True)
    d = r - mu
    var = jnp.mean(d * d, axis=-1, keepdims=True)
    xn = d * lax.rsqrt(var + NORM_EPS) * g_ref[...] + b_ref[...]
    xo_ref[...] = xn
    xb_ref[...] = xn.astype(bf16)


def _out_proj_norm(o, w, x, g, b):
    m = x.shape[0]
    tm = min(m, 256)
    row = lambda i: (i, 0)
    fixed = lambda i: (0, 0)
    return pl.pallas_call(
        _out_kernel,
        grid=(m // tm,),
        in_specs=[pl.BlockSpec((tm, D_MODEL), row),
                  pl.BlockSpec((D_MODEL, D_MODEL), fixed),
                  pl.BlockSpec((tm, D_MODEL), row),
                  pl.BlockSpec((1, D_MODEL), fixed),
                  pl.BlockSpec((1, D_MODEL), fixed)],
        out_specs=[pl.BlockSpec((tm, D_MODEL), row), pl.BlockSpec((tm, D_MODEL), row)],
        out_shape=[jax.ShapeDtypeStruct((m, D_MODEL), f32),
                   jax.ShapeDtypeStruct((m, D_MODEL), bf16)],
        compiler_params=_params("parallel"),
        name="out_proj_deepnorm",
    )(o, w, x, g.reshape(1, D_MODEL), b.reshape(1, D_MODEL))


def _bias_by_distance(rel_bias, dist):
    n = jnp.asarray(dist, jnp.int32)
    max_exact = REL_BUCKETS // 2
    nf = jnp.maximum(n, 1).astype(f32)
    large = max_exact + (jnp.log(nf / max_exact) / math.log(REL_MAX_DIST / max_exact)
                         * (REL_BUCKETS - max_exact)).astype(jnp.int32)
    large = jnp.minimum(large, REL_BUCKETS - 1)
    bucket = jnp.where(n < max_exact, n, large)
    return jnp.moveaxis(rel_bias.astype(f32)[bucket], -1, 0)


def _lambda_value(lp_ref, lam_init):
    lp = lp_ref[...]
    a = jnp.sum(lp[0:1] * lp[1:2], axis=-1, keepdims=True)
    b = jnp.sum(lp[2:3] * lp[3:4], axis=-1, keepdims=True)
    return jnp.exp(a) - jnp.exp(b) + lam_init


def _head_epilogue(o, g_ref, z, out_scale):
    ms = jnp.mean(o * o, axis=-1, keepdims=True)
    return (o * lax.rsqrt(ms + NORM_EPS) * g_ref[...] * out_scale) * _silu(z)


def _attn_kernel(q_ref, k_ref, v_ref, z_ref, bias_ref, far_ref, lp_ref, g_ref, o_ref,
                 kb_ref, vb_ref, s_ref, m_ref, acc_ref, *, lam_init):
    h = pl.program_id(1)
    i = pl.program_id(2)
    blk = ATT_BLOCK

    @pl.when(i == 0)
    def _():
        kb_ref[...] = k_ref[...].astype(bf16)
        vb_ref[...] = v_ref[...].astype(bf16)

    q = q_ref[...]
    q1 = q[:, :DIFF_DH]
    q2 = q[:, DIFF_DH:]
    far = far_ref[h]

    m_ref[...] = jnp.full(m_ref.shape, MASK_VALUE, f32)

    def logits(j, bias):
        off = pl.multiple_of(j * blk, blk)
        kj = kb_ref[pl.ds(off, blk), :]
        s1 = lax.dot_general(q1, kj[:, :DIFF_DH], _NT, preferred_element_type=f32) + bias
        s2 = lax.dot_general(q2, kj[:, DIFF_DH:], _NT, preferred_element_type=f32) + bias
        s_ref[0, j] = s1
        s_ref[1, j] = s2
        m_ref[0] = jnp.maximum(m_ref[0], jnp.maximum(s1[:, :LANES], s1[:, LANES:]))
        m_ref[1] = jnp.maximum(m_ref[1], jnp.maximum(s2[:, :LANES], s2[:, LANES:]))

    def far_body(j, c):
        logits(j, far)
        return c

    lax.fori_loop(0, i - 1, far_body, 0)

    @pl.when(i > 0)
    def _():
        logits(i - 1, bias_ref[1])

    logits(i, bias_ref[0])

    m1 = jnp.max(m_ref[0], axis=-1, keepdims=True)
    m2 = jnp.max(m_ref[1], axis=-1, keepdims=True)

    def exp_body(j, c):
        l1, l2 = c
        p1 = jnp.exp(s_ref[0, j] - m1)
        p2 = jnp.exp(s_ref[1, j] - m2)
        s_ref[0, j] = p1
        s_ref[1, j] = p2
        return (l1 + p1[:, :LANES] + p1[:, LANES:], l2 + p2[:, :LANES] + p2[:, LANES:])

    zero = jnp.zeros((blk, LANES), f32)
    l1, l2 = lax.fori_loop(0, i + 1, exp_body, (zero, zero))
    lam = _lambda_value(lp_ref, lam_init)
    c1 = 1.0 / jnp.sum(l1, axis=-1, keepdims=True)
    c2 = lam / jnp.sum(l2, axis=-1, keepdims=True)

    acc_ref[...] = jnp.zeros(acc_ref.shape, f32)

    def pv_body(j, c):
        off = pl.multiple_of(j * blk, blk)
        pd = (s_ref[0, j] * c1 - s_ref[1, j] * c2).astype(bf16)
        acc_ref[...] += jnp.dot(pd, vb_ref[pl.ds(off, blk), :], preferred_element_type=f32)
        return c

    lax.fori_loop(0, i + 1, pv_body, 0)
    o_ref[...] = _head_epilogue(acc_ref[...], g_ref, z_ref[...], 1.0 - lam_init).astype(bf16)


def _prompt_attention(q, k, v, z, bias_tiles, far, lam_p, norm_g, batch, seq, lam_init):
    nq = seq // ATT_BLOCK
    blk = ATT_BLOCK
    qmap = lambda b, h, i: (b * nq + i, h)
    kvmap = lambda b, h, i: (b, h)
    return pl.pallas_call(
        functools.partial(_attn_kernel, lam_init=lam_init),
        grid=(batch, DIFF_HEADS, nq),
        in_specs=[pl.BlockSpec((blk, DIFF_HW), qmap),
                  pl.BlockSpec((seq, DIFF_HW), kvmap),
                  pl.BlockSpec((seq, DIFF_HW), kvmap),
                  pl.BlockSpec((blk, DIFF_HW), qmap),
                  pl.BlockSpec((None, 2, blk, blk), lambda b, h, i: (h, 0, 0, 0)),
                  pl.BlockSpec(memory_space=pltpu.SMEM),
                  pl.BlockSpec((4, DIFF_DH), lambda b, h, i: (0, 0)),
                  pl.BlockSpec((1, DIFF_HW), lambda b, h, i: (0, 0))],
        out_specs=pl.BlockSpec((blk, DIFF_HW), qmap),
        out_shape=jax.ShapeDtypeStruct((batch * seq, D_MODEL), bf16),
        scratch_shapes=[pltpu.VMEM((seq, DIFF_HW), bf16),
                        pltpu.VMEM((seq, DIFF_HW), bf16),
                        pltpu.VMEM((2, nq, blk, blk), f32),
                        pltpu.VMEM((2, blk, LANES), f32),
                        pltpu.VMEM((blk, DIFF_HW), f32)],
        compiler_params=_params("parallel", "parallel", "arbitrary"),
        name="prompt_diff_attention",
    )(q, k, v, z, bias_tiles, far, lam_p, norm_g.reshape(1, DIFF_HW))


def _decode_kernel(pt_ref, *refs, lam_init, n_steps):
    del pt_ref
    npg = DEC_PAGES_PER_STEP
    k_pages = refs[:npg]
    v_pages = refs[npg:2 * npg]
    (q_ref, kn_ref, vn_ref, z_ref, bias_ref, lp_ref, g_ref, o_ref,
     m_ref, l_ref, acc_ref) = refs[2 * npg:]
    s = pl.program_id(1)
    rows = q_ref.shape[0]

    @pl.when(s == 0)
    def _():
        m_ref[...] = jnp.full(m_ref.shape, MASK_VALUE, f32)
        l_ref[...] = jnp.zeros(l_ref.shape, f32)
        acc_ref[...] = jnp.zeros(acc_ref.shape, f32)

    def process(k_ref, v_ref, bias_idx):
        for h in range(DIFF_HEADS):
            lo = h * DIFF_HW
            qh = q_ref[:, lo:lo + DIFF_HW]
            bias = bias_ref[bias_idx, h]
            ps = []
            for mp in range(2):
                c = lo + mp * DIFF_DH
                sc = lax.dot_general(qh[:, mp * DIFF_DH:(mp + 1) * DIFF_DH], k_ref[:, c:c + DIFF_DH],
                                     _NT, preferred_element_type=f32) + bias
                idx = 2 * h + mp
                m_old = m_ref[idx]
                m_new = jnp.maximum(m_old, jnp.max(sc, axis=-1, keepdims=True))
                a = jnp.exp(m_old - m_new)
                p = jnp.exp(sc - m_new)
                l_ref[idx] = a * l_ref[idx] + jnp.sum(p, axis=-1, keepdims=True)
                m_ref[idx] = m_new
                acc_ref[idx] = acc_ref[idx] * jnp.concatenate([a, a], axis=-1)
                ps.append(p)
            pv = jnp.dot(jnp.concatenate(ps, axis=0), v_ref[:, lo:lo + DIFF_HW],
                         preferred_element_type=f32)
            acc_ref[2 * h] += pv[:rows]
            acc_ref[2 * h + 1] += pv[rows:]

    for pg in range(npg):
        if pg == npg - 1:
            bias_idx = jnp.where(s == n_steps - 1, 1, 0)
        else:
            bias_idx = 0
        process(k_pages[pg], v_pages[pg], bias_idx)

    @pl.when(s == n_steps - 1)
    def _():
        process(kn_ref, vn_ref, 2)
        lam = _lambda_value(lp_ref, lam_init)
        for h in range(DIFF_HEADS):
            lo = h * DIFF_HW
            c1 = 1.0 / l_ref[2 * h]
            c2 = lam / l_ref[2 * h + 1]
            o = (acc_ref[2 * h] * jnp.concatenate([c1, c1], axis=-1)
                 - acc_ref[2 * h + 1] * jnp.concatenate([c2, c2], axis=-1))
            o_ref[:, lo:lo + DIFF_HW] = _head_epilogue(
                o, g_ref, z_ref[:, lo:lo + DIFF_HW], 1.0 - lam_init).astype(bf16)


def _decode_attention(page_table, cache_k, cache_v, layer, q, k_new, v_new, z, bias, lam_p, norm_g,
                      lam_init):
    nb, n_pages = page_table.shape
    npg = DEC_PAGES_PER_STEP
    n_steps = n_pages // npg
    rows = q.shape[0] // nb
    ck = cache_k.reshape(cache_k.shape[0], cache_k.shape[1], PAGE_SIZE, D_MODEL)
    cv = cache_v.reshape(cache_v.shape[0], cache_v.shape[1], PAGE_SIZE, D_MODEL)

    def page_spec(pg):
        return pl.BlockSpec((None, None, PAGE_SIZE, D_MODEL),
                            lambda b, s, pt: (layer, pt[b, s * npg + pg], 0, 0))

    row = lambda b, s, pt: (b, 0)
    fixed2 = lambda b, s, pt: (0, 0)
    grid_spec = pltpu.PrefetchScalarGridSpec(
        num_scalar_prefetch=1,
        grid=(nb, n_steps),
        in_specs=([page_spec(pg) for pg in range(npg)] + [page_spec(pg) for pg in range(npg)]
                  + [pl.BlockSpec((rows, D_MODEL), row),
                     pl.BlockSpec((PAGE_SIZE, D_MODEL), row),
                     pl.BlockSpec((PAGE_SIZE, D_MODEL), row),
                     pl.BlockSpec((rows, D_MODEL), row),
                     pl.BlockSpec(bias.shape, lambda b, s, pt: (0, 0, 0, 0)),
                     pl.BlockSpec((4, DIFF_DH), fixed2),
                     pl.BlockSpec((1, DIFF_HW), fixed2)]),
        out_specs=pl.BlockSpec((rows, D_MODEL), row),
        scratch_shapes=[pltpu.VMEM((2 * DIFF_HEADS, rows, LANES), f32),
                        pltpu.VMEM((2 * DIFF_HEADS, rows, LANES), f32),
                        pltpu.VMEM((2 * DIFF_HEADS, rows, DIFF_HW), f32)])
    return pl.pallas_call(
        functools.partial(_decode_kernel, lam_init=lam_init, n_steps=n_steps),
        grid_spec=grid_spec,
        out_shape=jax.ShapeDtypeStruct(q.shape, bf16),
        compiler_params=_params("parallel", "arbitrary"),
        name="decode_diff_attention",
    )(page_table, *([ck] * npg), *([cv] * npg), q, k_new, v_new, z, bias, lam_p,
      norm_g.reshape(1, DIFF_HW))


def _gate_kernel(x_ref, w1_ref, w2_ref, b_ref, g_ref):
    gl = jnp.dot(x_ref[...], w1_ref[...], preferred_element_type=f32)
    u = jnp.dot(gl.astype(bf16), w2_ref[...], preferred_element_type=f32) + b_ref[...]
    ls = jnp.minimum(u, 0.0) - jnp.log(1.0 + jnp.exp(-jnp.abs(u)))
    g_ref[...] = ls * (1.0 / GLA_GATE_NORMALIZER)


def _gla_gate(x, w1, w2, b):
    m = x.shape[0]
    tm = min(m, 512)
    return pl.pallas_call(
        _gate_kernel,
        grid=(m // tm,),
        in_specs=[pl.BlockSpec((tm, D_MODEL), lambda i: (i, 0)),
                  pl.BlockSpec((D_MODEL, LANES), lambda i: (0, 0)),
                  pl.BlockSpec((LANES, GLA_DK_W), lambda i: (0, 0)),
                  pl.BlockSpec((1, GLA_DK_W), lambda i: (0, 0))],
        out_specs=pl.BlockSpec((tm, GLA_DK_W), lambda i: (i, 0)),
        out_shape=jax.ShapeDtypeStruct((m, GLA_DK_W), f32),
        compiler_params=_params("parallel"),
        name="gla_gate",
    )(x, w1, w2, b.reshape(1, GLA_DK_W))


def _gla_kernel(*refs, chunk, t_valid, n_chunks, has_state):
    if has_state:
        q_ref, k_ref, v_ref, g_ref, z_ref, ng_ref, s0_ref, o_ref, so_ref, st_ref = refs
    else:
        q_ref, k_ref, v_ref, g_ref, z_ref, ng_ref, o_ref, so_ref, st_ref = refs
    n = pl.program_id(2)

    @pl.when(n == 0)
    def _():
        if has_state:
            st_ref[...] = s0_ref[...].T
        else:
            st_ref[...] = jnp.zeros(st_ref.shape, f32)

    row = lax.broadcasted_iota(jnp.int32, (chunk, chunk), 0)
    col = lax.broadcasted_iota(jnp.int32, (chunk, chunk), 1)
    causal = row >= col
    g = g_ref[...]
    k = k_ref[...]
    if t_valid < chunk:
        valid = lax.broadcasted_iota(jnp.int32, (chunk, GLA_DK), 0) < t_valid
        g = jnp.where(valid, g, 0.0)
        k = jnp.where(valid, k, 0.0)
    tri = causal.astype(f32).astype(bf16)
    g_hi = g.astype(bf16)
    g_lo = (g - g_hi.astype(f32)).astype(bf16)
    b = (jnp.dot(tri, g_hi, preferred_element_type=f32)
         + jnp.dot(tri, g_lo, preferred_element_type=f32))
    b_last = b[chunk - 1:chunk, :]
    qe = (q_ref[...] * jnp.exp(b)).astype(bf16)
    ke = (k * jnp.exp(-b)).astype(bf16)
    kd = (k * jnp.exp(b_last - b)).astype(bf16)
    vb = v_ref[...].astype(bf16)
    a = lax.dot_general(qe, ke, _NT, preferred_element_type=f32)
    a = jnp.where(causal, a, 0.0).astype(bf16)
    st = st_ref[...]
    o = (jnp.dot(a, vb, preferred_element_type=f32)
         + lax.dot_general(qe, st.astype(bf16), _NT, preferred_element_type=f32))
    st_ref[...] = jnp.exp(b_last) * st + lax.dot_general(vb, kd, _TN, preferred_element_type=f32)
    o_ref[...] = _head_epilogue(o, ng_ref, z_ref[...], 1.0).astype(bf16)

    @pl.when(n == n_chunks - 1)
    def _():
        so_ref[...] = st_ref[...].T


def _gla(q, k, v, g, z, norm_g, s0, batch, seq, t_valid):
    chunk = min(GLA_CHUNK, seq)
    nc = seq // chunk
    has_state = s0 is not None
    tmap = lambda b, h, n: (b * nc + n, h)
    smap = lambda b, h, n: (b, h, 0, 0)
    in_specs = [pl.BlockSpec((chunk, GLA_DK), tmap),
                pl.BlockSpec((chunk, GLA_DK), tmap),
                pl.BlockSpec((chunk, GLA_DV), tmap),
                pl.BlockSpec((chunk, GLA_DK), tmap),
                pl.BlockSpec((chunk, GLA_DV), tmap),
                pl.BlockSpec((1, GLA_DV), lambda b, h, n: (0, 0))]
    args = [q, k, v, g, z, norm_g.reshape(1, GLA_DV)]
    if has_state:
        in_specs.append(pl.BlockSpec((None, None, GLA_DK, GLA_DV), smap))
        args.append(s0)
    return pl.pallas_call(
        functools.partial(_gla_kernel, chunk=chunk, t_valid=t_valid, n_chunks=nc,
                          has_state=has_state),
        grid=(batch, GLA_HEADS, nc),
        in_specs=in_specs,
        out_specs=[pl.BlockSpec((chunk, GLA_DV), tmap),
                   pl.BlockSpec((None, None, GLA_DK, GLA_DV), smap)],
        out_shape=[jax.ShapeDtypeStruct((batch * seq, D_MODEL), bf16),
                   jax.ShapeDtypeStruct((batch, GLA_HEADS, GLA_DK, GLA_DV), f32)],
        scratch_shapes=[pltpu.VMEM((GLA_DV, GLA_DK), f32)],
        compiler_params=_params("parallel", "parallel", "arbitrary"),
        name="gla_chunked",
    )(*args)


def _trunk(x, batch, seq, t_valid, weights, cache=None):
    (rel_bias, diff_w_in, diff_lambda, diff_norm_g, diff_w_out, gla_w_in, gla_w_g1, gla_w_g2,
     gla_b_g, gla_norm_g, gla_w_out, ln_g, ln_b) = weights
    xb = x.astype(bf16)
    k_rows, v_rows, states = [], [], []
    for i in range(DEPTH):
        j = i // 2
        if i % 2 == 0:
            w = diff_w_in[j]
            q = _matmul(xb, w, 0, D_MODEL, bf16, scale=DIFF_DH ** -0.5)
            k = _matmul(xb, w, D_MODEL, D_MODEL, f32)
            v = _matmul(xb, w, 2 * D_MODEL, D_MODEL, f32)
            z = _matmul(xb, w, 3 * D_MODEL, D_MODEL, f32)
            lam_init = 0.8 - 0.6 * math.exp(-0.3 * i)
            if cache is None:
                blk = ATT_BLOCK
                ii = np.arange(blk)[:, None]
                jj = np.arange(blk)[None, :]
                d0 = ii - jj
                diag = jnp.where(jnp.asarray(d0 >= 0)[None],
                                 _bias_by_distance(rel_bias, np.maximum(d0, 0)), MASK_VALUE)
                prev = _bias_by_distance(rel_bias, d0 + blk)
                tiles = jnp.stack([diag, prev], axis=1)
                far = _bias_by_distance(rel_bias, np.array([2 * blk]))[:, 0]
                o = _prompt_attention(q, k, v, z, tiles, far, diff_lambda[j], diff_norm_g[j],
                                      batch, seq, lam_init)
            else:
                cache_k, cache_v, page_table = cache[:3]
                tt = np.arange(seq)[:, None]
                ss = np.arange(PAGE_SIZE)[None, :]
                far = jnp.broadcast_to(
                    _bias_by_distance(rel_bias, np.array([[2 * PAGE_SIZE]])),
                    (DIFF_HEADS, seq, PAGE_SIZE))
                last = _bias_by_distance(rel_bias, PAGE_SIZE + tt - ss)
                vis = (ss <= tt) & (ss < t_valid)
                new = jnp.where(jnp.asarray(vis)[None],
                                _bias_by_distance(rel_bias, np.maximum(tt - ss, 0)), MASK_VALUE)
                bias = jnp.stack([far, last, new])
                pad = lambda a: jnp.pad(a.reshape(batch, seq, D_MODEL),
                                        ((0, 0), (0, PAGE_SIZE - seq), (0, 0))
                                        ).reshape(batch * PAGE_SIZE, D_MODEL)
                o = _decode_attention(page_table, cache_k, cache_v, j, q, pad(k), pad(v), z, bias,
                                      diff_lambda[j], diff_norm_g[j], lam_init)
            k_rows.append(k)
            v_rows.append(v)
            w_out = diff_w_out[j]
        else:
            w = gla_w_in[j]
            q = _matmul(xb, w, 0, GLA_DK_W, f32, scale=GLA_DK ** -0.5)
            k = _matmul(xb, w, GLA_DK_W, GLA_DK_W, f32)
            v = _matmul(xb, w, 2 * GLA_DK_W, D_MODEL, f32)
            z = _matmul(xb, w, 2 * GLA_DK_W + D_MODEL, D_MODEL, f32)
            g = _gla_gate(xb, gla_w_g1[j], gla_w_g2[j], gla_b_g[j])
            s0 = None if cache is None else cache[3][j]
            o, s = _gla(q, k, v, g, z, gla_norm_g[j], s0, batch, seq, t_valid)
            states.append(s)
            w_out = gla_w_out[j]
        x, xb = _out_proj_norm(o, w_out, x, ln_g[i], ln_b[i])
    return x, k_rows, v_rows, states


def kernel(x_prompt, x_sample, cache_k, cache_v, state_gla, page_table, rel_bias, diff_w_in,
           diff_lambda, diff_norm_g, diff_w_out, gla_w_in, gla_w_g2, gla_b_g, gla_norm_g, gla_w_out,
           ln_g, ln_b):
    batch, seq, _ = x_prompt.shape
    dec_batch, dec_seq, _ = x_sample.shape
    n_pages = page_table.shape[1]
    past_len = n_pages * PAGE_SIZE
    assert past_len >= 2 * PAGE_SIZE and dec_seq <= SAMPLE_ROWS
    assert seq % ATT_BLOCK == 0 and seq // ATT_BLOCK >= 2

    gate_off = 2 * GLA_DK_W + 2 * D_MODEL
    w_g1 = jnp.pad(gla_w_in[:, :, gate_off:], ((0, 0), (0, 0), (0, LANES - GLA_GATE_RANK)))
    w_g2 = jnp.pad(gla_w_g2, ((0, 0), (0, LANES - GLA_GATE_RANK), (0, 0)))
    weights = (rel_bias, diff_w_in.astype(bf16), diff_lambda, diff_norm_g, diff_w_out.astype(bf16),
               gla_w_in.astype(bf16), w_g1.astype(bf16), w_g2.astype(bf16), gla_b_g, gla_norm_g,
               gla_w_out.astype(bf16), ln_g, ln_b)

    y_p, k_p, v_p, s_p = _trunk(x_prompt.reshape(batch * seq, D_MODEL), batch, seq, seq, weights)

    rows = SAMPLE_ROWS
    xs = jnp.pad(x_sample, ((0, 0), (0, rows - dec_seq), (0, 0))).reshape(dec_batch * rows, D_MODEL)
    y_s, k_s, v_s, s_s = _trunk(xs, dec_batch, rows, dec_seq, weights,
                                cache=(cache_k, cache_v, page_table, state_gla))

    def rows_out(rows_list, b, t, t_keep):
        a = jnp.stack(rows_list).reshape(len(rows_list), b, t, DIFF_HEADS, DIFF_HW)
        return a[:, :, :t_keep]

    return (y_p.reshape(batch, seq, D_MODEL),
            y_s.reshape(dec_batch, rows, D_MODEL)[:, :dec_seq],
            rows_out(k_p, batch, seq, seq), rows_out(v_p, batch, seq, seq), jnp.stack(s_p),
            rows_out(k_s, dec_batch, rows, dec_seq), rows_out(v_s, dec_batch, rows, dec_seq),
            jnp.stack(s_s))
```

```python
import functools
import math

import jax
import jax.numpy as jnp
import numpy as np
from jax import lax
from jax.experimental import pallas as pl
from jax.experimental.pallas import tpu as pltpu

D_MODEL = 2048
DEPTH = 4
PAGE_SIZE = 128

DIFF_HEADS = 8
DIFF_DH = D_MODEL // (2 * DIFF_HEADS)
DIFF_HW = 2 * DIFF_DH

GLA_HEADS = 4
GLA_DK_W = D_MODEL // 2
GLA_DK = GLA_DK_W // GLA_HEADS
GLA_DV = D_MODEL // GLA_HEADS
GLA_GATE_RANK = 16
GLA_GATE_NORMALIZER = 16.0
GLA_CHUNK = 64

REL_BUCKETS = 32
REL_MAX_DIST = 128

NORM_EPS = 1e-5
DEEPNORM_ALPHA = (2 * DEPTH) ** 0.25

LANES = 128
SUBLANES = 8
VMEM_LIMIT = 48 * 1024 * 1024
MASK_VALUE = -1e30

ATT_BLOCK = 256
DEC_PAGES_PER_STEP = 4
SAMPLE_ROWS = SUBLANES
PAGE_ROWS = PAGE_SIZE * DIFF_HEADS
NEW_ROWS = 2 * SAMPLE_ROWS * DIFF_HEADS
DEC_COLS = 2 * DIFF_HEADS * SAMPLE_ROWS
assert DEC_COLS == LANES and NEW_ROWS == LANES

_NT = (((1,), (1,)), ((), ()))
_TN = (((0,), (0,)), ((), ()))

bf16 = jnp.bfloat16
f32 = jnp.float32


def _params(*sem):
    return pltpu.CompilerParams(dimension_semantics=sem, vmem_limit_bytes=VMEM_LIMIT)


def _silu(z):
    return z * (1.0 / (1.0 + jnp.exp(-z)))


def _mm_kernel(x_ref, w_ref, o_ref, wb_ref, *, scale):
    @pl.when(pl.program_id(1) == 0)
    def _():
        wb_ref[...] = w_ref[...].astype(bf16)

    acc = jnp.dot(x_ref[...], wb_ref[...], preferred_element_type=f32)
    if scale != 1.0:
        acc = acc * scale
    o_ref[...] = acc.astype(o_ref.dtype)


def _matmul(x, w, layer, col_off, n, out_dtype, scale=1.0):
    m, k = x.shape
    tm = min(m, 1024)
    tn = min(n, 1024)
    assert m % tm == 0 and n % tn == 0 and col_off % tn == 0
    off = col_off // tn
    return pl.pallas_call(
        functools.partial(_mm_kernel, scale=scale),
        grid=(n // tn, m // tm),
        in_specs=[pl.BlockSpec((tm, k), lambda j, i: (i, 0)),
                  pl.BlockSpec((None, k, tn), lambda j, i: (layer, 0, j + off))],
        out_specs=pl.BlockSpec((tm, tn), lambda j, i: (i, j)),
        out_shape=jax.ShapeDtypeStruct((m, n), out_dtype),
        scratch_shapes=[pltpu.VMEM((k, tn), bf16)],
        compiler_params=_params("parallel", "arbitrary"),
        name="proj_matmul",
    )(x, w)


def _out_kernel(o_ref, w_ref, x_ref, g_ref, b_ref, xo_ref, xb_ref):
    y = jnp.dot(o_ref[...].astype(bf16), w_ref[...], preferred_element_type=f32)
    r = DEEPNORM_ALPHA * x_ref[...] + y
    mu = jnp.mean(r, axis=-1, keepdims=True)
    d = r - mu
    var = jnp.mean(d * d, axis=-1, keepdims=True)
    xn = d * lax.rsqrt(var + NORM_EPS) * g_ref[...] + b_ref[...]
    xo_ref[...] = xn
    xb_ref[...] = xn.astype(bf16)


def _out_proj_norm(o, w, x, g, b):
    m = x.shape[0]
    tm = min(m, 256)
    row = lambda i: (i, 0)
    fixed = lambda i: (0, 0)
    return pl.pallas_call(
        _out_kernel,
        grid=(m // tm,),
        in_specs=[pl.BlockSpec((tm, D_MODEL), row),
                  pl.BlockSpec((D_MODEL, D_MODEL), fixed),
                  pl.BlockSpec((tm, D_MODEL), row),
                  pl.BlockSpec((1, D_MODEL), fixed),
                  pl.BlockSpec((1, D_MODEL), fixed)],
        out_specs=[pl.BlockSpec((tm, D_MODEL), row), pl.BlockSpec((tm, D_MODEL), row)],
        out_shape=[jax.ShapeDtypeStruct((m, D_MODEL), f32),
                   jax.ShapeDtypeStruct((m, D_MODEL), bf16)],
        compiler_params=_params("parallel"),
        name="out_proj_deepnorm",
    )(o, w, x, g.reshape(1, D_MODEL), b.reshape(1, D_MODEL))


def _bias_by_distance(rel_bias, dist):
    n = jnp.asarray(dist, jnp.int32)
    max_exact = REL_BUCKETS // 2
    nf = jnp.maximum(n, 1).astype(f32)
    large = max_exact + (jnp.log(nf / max_exact) / math.log(REL_MAX_DIST / max_exact)
                         * (REL_BUCKETS - max_exact)).astype(jnp.int32)
    large = jnp.minimum(large, REL_BUCKETS - 1)
    bucket = jnp.where(n < max_exact, n, large)
    return jnp.moveaxis(rel_bias.astype(f32)[bucket], -1, 0)


def _prompt_bias_tiles(rel_bias, blk):
    k = np.arange(3 * blk - 1)
    d = 2 * blk - 1 - k
    u = jnp.where(jnp.asarray(d >= 0), _bias_by_distance(rel_bias, np.maximum(d, 0)), MASK_VALUE)
    period = 3 * blk
    flat = jnp.tile(jnp.pad(u, ((0, 0), (0, 1))), (1, blk))[:, :blk * (period - 1)]
    t = flat.reshape(DIFF_HEADS, blk, period - 1)[:, :, blk - 1:3 * blk - 1]
    return jnp.stack([t[:, :, blk:], t[:, :, :blk]], axis=1)


def _decode_bias_tables(rel_bias, t_valid):
    col = np.arange(DEC_COLS)
    col_h = (col // SAMPLE_ROWS) % DIFF_HEADS
    reps = DEC_COLS // SAMPLE_ROWS
    tq = np.arange(SAMPLE_ROWS)[None, :]
    tab = _bias_by_distance(rel_bias, np.arange(2 * PAGE_SIZE + SAMPLE_ROWS + 1))

    def expand(n_rows, dist, visible):
        r = np.arange(n_rows)
        r_h = r % DIFF_HEADS
        r_s = (r // DIFF_HEADS)[:, None]
        vals = tab[r_h[:, None], np.maximum(dist(r_s), 0)]
        ok = (r_h[:, None] == col_h[None, :]) & np.tile(visible(r_s), (1, reps))
        return jnp.where(jnp.asarray(ok), jnp.tile(vals, (1, reps)), MASK_VALUE)

    everything = lambda s: np.ones((s.shape[0], SAMPLE_ROWS), bool)
    far = expand(PAGE_ROWS, lambda s: 2 * PAGE_SIZE + 0 * tq + 0 * s, everything)
    last = expand(PAGE_ROWS, lambda s: PAGE_SIZE + tq - s, everything)
    new = expand(NEW_ROWS, lambda s: tq - s, lambda s: (s <= tq) & (s < t_valid))
    return jnp.stack([far, last]), new


def _lambda_value(lp_ref, lam_init):
    lp = lp_ref[...]
    a = jnp.sum(lp[0:1] * lp[1:2], axis=-1, keepdims=True)
    b = jnp.sum(lp[2:3] * lp[3:4], axis=-1, keepdims=True)
    return jnp.exp(a) - jnp.exp(b) + lam_init


def _head_epilogue(o, g, z, out_scale):
    ms = jnp.mean(o * o, axis=-1, keepdims=True)
    return (o * lax.rsqrt(ms + NORM_EPS) * g * out_scale) * _silu(z)


def _attn_kernel(q_ref, k_ref, v_ref, z_ref, bias_ref, far_ref, lp_ref, g_ref, o_ref,
                 kb_ref, vb_ref, s_ref, *, lam_init, n_blocks):
    h = pl.program_id(1)
    i = pl.program_id(2)
    blk = ATT_BLOCK

    @pl.when(i == 0)
    def _():
        kb_ref[...] = k_ref[...].astype(bf16)
        vb_ref[...] = v_ref[...].astype(bf16)

    far = far_ref[h]
    lam = _lambda_value(lp_ref, lam_init)

    def lane_fold(x):
        return [x[:, c * LANES:(c + 1) * LANES] for c in range(blk // LANES)]

    def query_block(nvis):
        q = q_ref[...]
        qs = (q[:, :DIFF_DH], q[:, DIFF_DH:])
        mx = [None, None]
        for j in range(nvis):
            kj = kb_ref[j * blk:(j + 1) * blk, :]
            if j == nvis - 1:
                bias = bias_ref[0]
            elif j == nvis - 2:
                bias = bias_ref[1]
            else:
                bias = far
            for mp in range(2):
                s = lax.dot_general(qs[mp], kj[:, mp * DIFF_DH:(mp + 1) * DIFF_DH], _NT,
                                    preferred_element_type=f32) + bias
                s_ref[mp, j] = s
                for part in lane_fold(s):
                    mx[mp] = part if mx[mp] is None else jnp.maximum(mx[mp], part)
        m = [jnp.max(mx[mp], axis=-1, keepdims=True) for mp in range(2)]
        ls = [None, None]
        for j in range(nvis):
            for mp in range(2):
                p = jnp.exp(s_ref[mp, j] - m[mp])
                s_ref[mp, j] = p
                for part in lane_fold(p):
                    ls[mp] = part if ls[mp] is None else ls[mp] + part
        c1 = 1.0 / jnp.sum(ls[0], axis=-1, keepdims=True)
        c2 = lam / jnp.sum(ls[1], axis=-1, keepdims=True)
        acc = None
        for j in range(nvis):
            pd = (s_ref[0, j] * c1 - s_ref[1, j] * c2).astype(bf16)
            pv = jnp.dot(pd, vb_ref[j * blk:(j + 1) * blk, :], preferred_element_type=f32)
            acc = pv if acc is None else acc + pv
        o_ref[...] = _head_epilogue(acc, g_ref[...], z_ref[...], 1.0 - lam_init).astype(bf16)

    for c in range(n_blocks):
        pl.when(i == c)(functools.partial(query_block, c + 1))


def _prompt_attention(q, k, v, z, bias_tiles, far, lam_p, norm_g, batch, seq, lam_init):
    nq = seq // ATT_BLOCK
    blk = ATT_BLOCK
    qmap = lambda b, h, i: (b * nq + i, h)
    kvmap = lambda b, h, i: (b, h)
    return pl.pallas_call(
        functools.partial(_attn_kernel, lam_init=lam_init, n_blocks=nq),
        grid=(batch, DIFF_HEADS, nq),
        in_specs=[pl.BlockSpec((blk, DIFF_HW), qmap),
                  pl.BlockSpec((seq, DIFF_HW), kvmap),
                  pl.BlockSpec((seq, DIFF_HW), kvmap),
                  pl.BlockSpec((blk, DIFF_HW), qmap),
                  pl.BlockSpec((None, 2, blk, blk), lambda b, h, i: (h, 0, 0, 0)),
                  pl.BlockSpec(memory_space=pltpu.SMEM),
                  pl.BlockSpec((4, DIFF_DH), lambda b, h, i: (0, 0)),
                  pl.BlockSpec((1, DIFF_HW), lambda b, h, i: (0, 0))],
        out_specs=pl.BlockSpec((blk, DIFF_HW), qmap),
        out_shape=jax.ShapeDtypeStruct((batch * seq, D_MODEL), bf16),
        scratch_shapes=[pltpu.VMEM((seq, DIFF_HW), bf16),
                        pltpu.VMEM((seq, DIFF_HW), bf16),
                        pltpu.VMEM((2, nq, blk, blk), f32)],
        compiler_params=_params("parallel", "parallel", "arbitrary"),
        name="prompt_diff_attention",
    )(q, k, v, z, bias_tiles, far, lam_p, norm_g.reshape(1, DIFF_HW))


def _decode_kernel(pt_ref, *refs, lam_init, n_steps):
    del pt_ref
    npg = DEC_PAGES_PER_STEP
    k_pages = refs[:npg]
    v_pages = refs[npg:2 * npg]
    (w_ref, kn_ref, vn_ref, z_ref, bm_ref, bmn_ref, lp_ref, g_ref, o_ref,
     m_ref, l_ref, acc_ref) = refs[2 * npg:]
    s = pl.program_id(1)

    @pl.when(s == 0)
    def _():
        m_ref[...] = jnp.full(m_ref.shape, MASK_VALUE, f32)
        l_ref[...] = jnp.zeros(l_ref.shape, f32)
        acc_ref[...] = jnp.zeros(acc_ref.shape, f32)

    w = w_ref[...]
    eye = (lax.broadcasted_iota(jnp.int32, (DEC_COLS, DEC_COLS), 0)
           == lax.broadcasted_iota(jnp.int32, (DEC_COLS, DEC_COLS), 1))

    def to_column(row):
        return jnp.sum(jnp.where(eye, row, 0.0), axis=1, keepdims=True)

    def update(blocks):
        logits = [jnp.dot(k.astype(bf16), w, preferred_element_type=f32) + bm for k, _, bm in blocks]
        cmax = None
        for lg in logits:
            c = jnp.max(lg, axis=0, keepdims=True)
            cmax = c if cmax is None else jnp.maximum(cmax, c)
        m_old = m_ref[...]
        m_new = jnp.maximum(m_old, cmax)
        a = jnp.exp(m_old - m_new)
        lsum = None
        pv = None
        for lg, (_, v, _) in zip(logits, blocks):
            p = jnp.exp(lg - m_new)
            ps = jnp.sum(p, axis=0, keepdims=True)
            lsum = ps if lsum is None else lsum + ps
            d = jnp.dot(p.T.astype(bf16), v.astype(bf16), preferred_element_type=f32)
            pv = d if pv is None else pv + d
        m_ref[...] = m_new
        l_ref[...] = a * l_ref[...] + lsum
        acc_ref[...] = acc_ref[...] * to_column(a) + pv

    blocks = []
    for pg in range(npg):
        if pg == npg - 1:
            bm = bm_ref[jnp.where(s == n_steps - 1, 1, 0)]
        else:
            bm = bm_ref[0]
        blocks.append((k_pages[pg][...], v_pages[pg][...], bm))
    update(blocks)

    @pl.when(s == n_steps - 1)
    def _():
        update([(kn_ref[...], vn_ref[...], bmn_ref[...])])
        on = acc_ref[...] * (1.0 / to_column(l_ref[...]))
        lam = _lambda_value(lp_ref, lam_init)
        half = DEC_COLS // 2
        for h in range(DIFF_HEADS):
            r0 = h * SAMPLE_ROWS
            o = on[r0:r0 + SAMPLE_ROWS] - lam * on[half + r0:half + r0 + SAMPLE_ROWS]
            lo = h * DIFF_HW
            o_ref[:, lo:lo + DIFF_HW] = _head_epilogue(o, g_ref[...], z_ref[:, lo:lo + DIFF_HW],
                                                       1.0 - lam_init)


def _decode_attention(page_table, cache_k, cache_v, layer, q, k_new, v_new, z, bm, bm_new, lam_p,
                      norm_g, lam_init):
    nb, n_pages = page_table.shape
    npg = DEC_PAGES_PER_STEP
    n_steps = n_pages // npg
    ck = cache_k.reshape(cache_k.shape[0], cache_k.shape[1], PAGE_ROWS, DIFF_HW)
    cv = cache_v.reshape(cache_v.shape[0], cache_v.shape[1], PAGE_ROWS, DIFF_HW)

    q5 = q.reshape(nb, SAMPLE_ROWS, DIFF_HEADS, 2, DIFF_DH)
    qt = jnp.transpose(q5, (0, 3, 4, 2, 1)).reshape(nb, 2, DIFF_DH, DEC_COLS // 2)
    zeros = jnp.zeros_like(qt[:, 0])
    w = jnp.concatenate([jnp.concatenate([qt[:, 0], zeros], axis=-1),
                         jnp.concatenate([zeros, qt[:, 1]], axis=-1)], axis=1)

    def new_rows(a):
        a = a.reshape(nb, SAMPLE_ROWS, DIFF_HEADS, DIFF_HW)
        a = jnp.pad(a, ((0, 0), (0, NEW_ROWS // DIFF_HEADS - SAMPLE_ROWS), (0, 0), (0, 0)))
        return a.reshape(nb, NEW_ROWS, DIFF_HW)

    def page_spec(pg):
        return pl.BlockSpec((None, None, PAGE_ROWS, DIFF_HW),
                            lambda b, s, pt: (layer, pt[b, s * npg + pg], 0, 0))

    per_batch3 = lambda b, s, pt: (b, 0, 0)
    row = lambda b, s, pt: (b, 0)
    fixed2 = lambda b, s, pt: (0, 0)
    grid_spec = pltpu.PrefetchScalarGridSpec(
        num_scalar_prefetch=1,
        grid=(nb, n_steps),
        in_specs=([page_spec(pg) for pg in range(npg)] + [page_spec(pg) for pg in range(npg)]
                  + [pl.BlockSpec((None, DIFF_HW, DEC_COLS), per_batch3),
                     pl.BlockSpec((None, NEW_ROWS, DIFF_HW), per_batch3),
                     pl.BlockSpec((None, NEW_ROWS, DIFF_HW), per_batch3),
                     pl.BlockSpec((SAMPLE_ROWS, D_MODEL), row),
                     pl.BlockSpec((2, PAGE_ROWS, DEC_COLS), lambda b, s, pt: (0, 0, 0)),
                     pl.BlockSpec((NEW_ROWS, DEC_COLS), fixed2),
                     pl.BlockSpec((4, DIFF_DH), fixed2),
                     pl.BlockSpec((1, DIFF_HW), fixed2)]),
        out_specs=pl.BlockSpec((SAMPLE_ROWS, D_MODEL), row),
        scratch_shapes=[pltpu.VMEM((1, DEC_COLS), f32),
                        pltpu.VMEM((1, DEC_COLS), f32),
                        pltpu.VMEM((DEC_COLS, DIFF_HW), f32)])
    return pl.pallas_call(
        functools.partial(_decode_kernel, lam_init=lam_init, n_steps=n_steps),
        grid_spec=grid_spec,
        out_shape=jax.ShapeDtypeStruct((nb * SAMPLE_ROWS, D_MODEL), f32),
        compiler_params=_params("parallel", "arbitrary"),
        name="decode_diff_attention",
    )(page_table, *([ck] * npg), *([cv] * npg), w, new_rows(k_new), new_rows(v_new), z, bm, bm_new,
      lam_p, norm_g.reshape(1, DIFF_HW))


def _gate_kernel(x_ref, w1_ref, w2_ref, b_ref, g_ref):
    gl = jnp.dot(x_ref[...], w1_ref[...], preferred_element_type=f32)
    u = jnp.dot(gl.astype(bf16), w2_ref[...], preferred_element_type=f32) + b_ref[...]
    ls = jnp.minimum(u, 0.0) - jnp.log(1.0 + jnp.exp(-jnp.abs(u)))
    g_ref[...] = ls * (1.0 / GLA_GATE_NORMALIZER)


def _gla_gate(x, w1, w2, b):
    m = x.shape[0]
    tm = min(m, 512)
    return pl.pallas_call(
        _gate_kernel,
        grid=(m // tm,),
        in_specs=[pl.BlockSpec((tm, D_MODEL), lambda i: (i, 0)),
                  pl.BlockSpec((D_MODEL, LANES), lambda i: (0, 0)),
                  pl.BlockSpec((LANES, GLA_DK_W), lambda i: (0, 0)),
                  pl.BlockSpec((1, GLA_DK_W), lambda i: (0, 0))],
        out_specs=pl.BlockSpec((tm, GLA_DK_W), lambda i: (i, 0)),
        out_shape=jax.ShapeDtypeStruct((m, GLA_DK_W), f32),
        compiler_params=_params("parallel"),
        name="gla_gate",
    )(x, w1, w2, b.reshape(1, GLA_DK_W))


def _gla_kernel(*refs, chunk, t_valid, n_chunks, has_state):
    if has_state:
        q_ref, k_ref, v_ref, g_ref, z_ref, ng_ref, s0_ref, o_ref, so_ref, st_ref = refs
    else:
        q_ref, k_ref, v_ref, g_ref, z_ref, ng_ref, o_ref, so_ref, st_ref = refs
    n = pl.program_id(1)

    @pl.when(n == 0)
    def _():
        for h in range(GLA_HEADS):
            if has_state:
                st_ref[h] = s0_ref[h].T
            else:
                st_ref[h] = jnp.zeros(st_ref.shape[1:], f32)

    row = lax.broadcasted_iota(jnp.int32, (chunk, chunk), 0)
    col = lax.broadcasted_iota(jnp.int32, (chunk, chunk), 1)
    causal = row >= col
    tri = causal.astype(f32).astype(bf16)
    valid = lax.broadcasted_iota(jnp.int32, (chunk, GLA_DK), 0) < t_valid
    for h in range(GLA_HEADS):
        kc = slice(h * GLA_DK, (h + 1) * GLA_DK)
        vc = slice(h * GLA_DV, (h + 1) * GLA_DV)
        g = g_ref[:, kc]
        k = k_ref[:, kc]
        if t_valid < chunk:
            g = jnp.where(valid, g, 0.0)
            k = jnp.where(valid, k, 0.0)
        g_hi = g.astype(bf16)
        g_lo = (g - g_hi.astype(f32)).astype(bf16)
        b = (jnp.dot(tri, g_hi, preferred_element_type=f32)
             + jnp.dot(tri, g_lo, preferred_element_type=f32))
        b_last = b[chunk - 1:chunk, :]
        qe = (q_ref[:, kc] * jnp.exp(b)).astype(bf16)
        ke = (k * jnp.exp(-b)).astype(bf16)
        kd = (k * jnp.exp(b_last - b)).astype(bf16)
        vb = v_ref[:, vc].astype(bf16)
        a = lax.dot_general(qe, ke, _NT, preferred_element_type=f32)
        a = jnp.where(causal, a, 0.0).astype(bf16)
        st = st_ref[h]
        o = (jnp.dot(a, vb, preferred_element_type=f32)
             + lax.dot_general(qe, st.astype(bf16), _NT, preferred_element_type=f32))
        st_ref[h] = jnp.exp(b_last) * st + lax.dot_general(vb, kd, _TN, preferred_element_type=f32)
        o_ref[:, vc] = _head_epilogue(o, ng_ref[...], z_ref[:, vc], 1.0).astype(o_ref.dtype)

    @pl.when(n == n_chunks - 1)
    def _():
        for h in range(GLA_HEADS):
            so_ref[h] = st_ref[h].T


def _gla(q, k, v, g, z, norm_g, s0, batch, seq, t_valid):
    chunk = min(GLA_CHUNK, seq)
    nc = seq // chunk
    has_state = s0 is not None
    out_dtype = bf16 if chunk % (2 * SUBLANES) == 0 else f32
    tmap = lambda b, n: (b * nc + n, 0)
    smap = lambda b, n: (b, 0, 0, 0)
    state_spec = pl.BlockSpec((None, GLA_HEADS, GLA_DK, GLA_DV), smap)
    in_specs = [pl.BlockSpec((chunk, GLA_DK_W), tmap),
                pl.BlockSpec((chunk, GLA_DK_W), tmap),
                pl.BlockSpec((chunk, D_MODEL), tmap),
                pl.BlockSpec((chunk, GLA_DK_W), tmap),
                pl.BlockSpec((chunk, D_MODEL), tmap),
                pl.BlockSpec((1, GLA_DV), lambda b, n: (0, 0))]
    args = [q, k, v, g, z, norm_g.reshape(1, GLA_DV)]
    if has_state:
        in_specs.append(state_spec)
        args.append(s0)
    return pl.pallas_call(
        functools.partial(_gla_kernel, chunk=chunk, t_valid=t_valid, n_chunks=nc,
                          has_state=has_state),
        grid=(batch, nc),
        in_specs=in_specs,
        out_specs=[pl.BlockSpec((chunk, D_MODEL), tmap), state_spec],
        out_shape=[jax.ShapeDtypeStruct((batch * seq, D_MODEL), out_dtype),
                   jax.ShapeDtypeStruct((batch, GLA_HEADS, GLA_DK, GLA_DV), f32)],
        scratch_shapes=[pltpu.VMEM((GLA_HEADS, GLA_DV, GLA_DK), f32)],
        compiler_params=_params("parallel", "arbitrary"),
        name="gla_chunked",
    )(*args)


def _trunk(x, batch, seq, t_valid, weights, cache=None):
    (rel_bias, diff_w_in, diff_lambda, diff_norm_g, diff_w_out, gla_w_in, gla_w_g1, gla_w_g2,
     gla_b_g, gla_norm_g, gla_w_out, ln_g, ln_b) = weights
    xb = x.astype(bf16)
    k_rows, v_rows, states = [], [], []
    if cache is None:
        tiles = _prompt_bias_tiles(rel_bias, ATT_BLOCK)
        far = _bias_by_distance(rel_bias, np.array([2 * ATT_BLOCK]))[:, 0]
    else:
        bm, bm_new = _decode_bias_tables(rel_bias, t_valid)
    for i in range(DEPTH):
        j = i // 2
        if i % 2 == 0:
            w = diff_w_in
            q = _matmul(xb, w, j, 0, D_MODEL, bf16, scale=DIFF_DH ** -0.5)
            k = _matmul(xb, w, j, D_MODEL, D_MODEL, f32)
            v = _matmul(xb, w, j, 2 * D_MODEL, D_MODEL, f32)
            z = _matmul(xb, w, j, 3 * D_MODEL, D_MODEL, f32)
            lam_init = 0.8 - 0.6 * math.exp(-0.3 * i)
            if cache is None:
                o = _prompt_attention(q, k, v, z, tiles, far, diff_lambda[j], diff_norm_g[j],
                                      batch, seq, lam_init)
            else:
                cache_k, cache_v, page_table = cache[:3]
                o = _decode_attention(page_table, cache_k, cache_v, j, q, k, v, z, bm, bm_new,
                                      diff_lambda[j], diff_norm_g[j], lam_init)
            k_rows.append(k)
            v_rows.append(v)
            w_out = diff_w_out[j]
        else:
            w = gla_w_in
            q = _matmul(xb, w, j, 0, GLA_DK_W, f32, scale=GLA_DK ** -0.5)
            k = _matmul(xb, w, j, GLA_DK_W, GLA_DK_W, f32)
            v = _matmul(xb, w, j, 2 * GLA_DK_W, D_MODEL, f32)
            z = _matmul(xb, w, j, 2 * GLA_DK_W + D_MODEL, D_MODEL, f32)
            g = _gla_gate(xb, gla_w_g1[j], gla_w_g2[j], gla_b_g[j])
            s0 = None if cache is None else cache[3][j]
            o, s = _gla(q, k, v, g, z, gla_norm_g[j], s0, batch, seq, t_valid)
            states.append(s)
            w_out = gla_w_out[j]
        x, xb = _out_proj_norm(o, w_out, x, ln_g[i], ln_b[i])
    return x, k_rows, v_rows, states


def kernel(x_prompt, x_sample, cache_k, cache_v, state_gla, page_table, rel_bias, diff_w_in,
           diff_lambda, diff_norm_g, diff_w_out, gla_w_in, gla_w_g2, gla_b_g, gla_norm_g, gla_w_out,
           ln_g, ln_b):
    batch, seq, _ = x_prompt.shape
    dec_batch, dec_seq, _ = x_sample.shape
    n_pages = page_table.shape[1]
    assert n_pages >= 2 and n_pages % DEC_PAGES_PER_STEP == 0 and dec_seq <= SAMPLE_ROWS
    assert cache_k.shape[2:] == (PAGE_SIZE, DIFF_HEADS, DIFF_HW)
    assert seq % ATT_BLOCK == 0 and seq // ATT_BLOCK >= 2 and seq % GLA_CHUNK == 0

    gate_off = 2 * GLA_DK_W + 2 * D_MODEL
    w_g1 = jnp.pad(gla_w_in[:, :, gate_off:], ((0, 0), (0, 0), (0, LANES - GLA_GATE_RANK)))
    w_g2 = jnp.pad(gla_w_g2, ((0, 0), (0, LANES - GLA_GATE_RANK), (0, 0)))
    weights = (rel_bias, diff_w_in, diff_lambda, diff_norm_g, diff_w_out.astype(bf16),
               gla_w_in, w_g1.astype(bf16), w_g2.astype(bf16), gla_b_g, gla_norm_g,
               gla_w_out.astype(bf16), ln_g, ln_b)

    y_p, k_p, v_p, s_p = _trunk(x_prompt.reshape(batch * seq, D_MODEL), batch, seq, seq, weights)

    rows = SAMPLE_ROWS
    xs = jnp.pad(x_sample, ((0, 0), (0, rows - dec_seq), (0, 0))).reshape(dec_batch * rows, D_MODEL)
    y_s, k_s, v_s, s_s = _trunk(xs, dec_batch, rows, dec_seq, weights,
                                cache=(cache_k, cache_v, page_table, state_gla))

    def rows_out(rows_list, b, t, t_keep):
        a = jnp.stack(rows_list).reshape(len(rows_list), b, t, DIFF_HEADS, DIFF_HW)
        return a[:, :, :t_keep]

    return (y_p.reshape(batch, seq, D_MODEL),
            y_s.reshape(dec_batch, rows, D_MODEL)[:, :dec_seq],
            rows_out(k_p, batch, seq, seq), rows_out(v_p, batch, seq, seq), jnp.stack(s_p),
            rows_out(k_s, dec_batch, rows, dec_seq), rows_out(v_s, dec_batch, rows, dec_seq),
            jnp.stack(s_s))
```

```python
import functools
import math

import jax
import jax.numpy as jnp
import numpy as np
from jax import lax
from jax.experimental import pallas as pl
from jax.experimental.pallas import tpu as pltpu

D_MODEL = 2048
DEPTH = 4
PAGE_SIZE = 128

DIFF_HEADS = 8
DIFF_DH = D_MODEL // (2 * DIFF_HEADS)
DIFF_HW = 2 * DIFF_DH

GLA_HEADS = 4
GLA_DK_W = D_MODEL // 2
GLA_DK = GLA_DK_W // GLA_HEADS
GLA_DV = D_MODEL // GLA_HEADS
GLA_GATE_RANK = 16
GLA_GATE_NORMALIZER = 16.0
GLA_CHUNK = 64

REL_BUCKETS = 32
REL_MAX_DIST = 128

NORM_EPS = 1e-5
DEEPNORM_ALPHA = (2 * DEPTH) ** 0.25

LANES = 128
SUBLANES = 8
VMEM_LIMIT = 48 * 1024 * 1024
MASK_VALUE = -1e30

ATT_BLOCK = 256
DEC_PAGES_PER_STEP = 4
SAMPLE_ROWS = SUBLANES
PAGE_ROWS = PAGE_SIZE * DIFF_HEADS
NEW_ROWS = 2 * SAMPLE_ROWS * DIFF_HEADS
DEC_COLS = 2 * DIFF_HEADS * SAMPLE_ROWS
assert DEC_COLS == LANES and NEW_ROWS == LANES

_NT = (((1,), (1,)), ((), ()))
_TN = (((0,), (0,)), ((), ()))

bf16 = jnp.bfloat16
f32 = jnp.float32


def _params(*sem):
    return pltpu.CompilerParams(dimension_semantics=sem, vmem_limit_bytes=VMEM_LIMIT)


def _silu(z):
    return z * (1.0 / (1.0 + jnp.exp(-z)))


def _mm_kernel(xp_ref, xs_ref, w_ref, *rest, scale):
    op_ref, os_ref, wb_ref = rest[-3:]

    def project(x_ref, o_ref):
        acc = jnp.dot(x_ref[...], wb_ref[...], preferred_element_type=f32)
        if scale != 1.0:
            acc = acc * scale
        o_ref[...] = acc.astype(o_ref.dtype)

    @pl.when(pl.program_id(1) == 0)
    def _():
        wb_ref[...] = w_ref[...].astype(bf16)
        project(xs_ref, os_ref)

    project(xp_ref, op_ref)


def _matmul(xp, xs, w, layer, col_off, n, out_dtype, scale=1.0, slots=None, slot=0, into=None):
    m, k = xp.shape
    ms = xs.shape[0]
    tm = min(m, 1024)
    tn = min(n, 1024)
    assert m % tm == 0 and n % tn == 0 and col_off % tn == 0
    off = col_off // tn
    in_specs = [pl.BlockSpec((tm, k), lambda j, i: (i, 0)),
                pl.BlockSpec((ms, k), lambda j, i: (0, 0)),
                pl.BlockSpec((None, k, tn), lambda j, i: (layer, 0, j + off))]
    args = [xp, xs, w]
    aliases = {}
    if slots is None:
        out_specs = [pl.BlockSpec((tm, tn), lambda j, i: (i, j)),
                     pl.BlockSpec((ms, tn), lambda j, i: (0, j))]
        out_shape = [jax.ShapeDtypeStruct((m, n), out_dtype),
                     jax.ShapeDtypeStruct((ms, n), out_dtype)]
    else:
        out_specs = [pl.BlockSpec((None, tm, tn), lambda j, i: (slot, i, j)),
                     pl.BlockSpec((None, ms, tn), lambda j, i: (slot, 0, j))]
        out_shape = [jax.ShapeDtypeStruct((slots, m, n), out_dtype),
                     jax.ShapeDtypeStruct((slots, ms, n), out_dtype)]
        if into is not None:
            in_specs += [pl.BlockSpec(memory_space=pl.ANY)] * 2
            args += list(into)
            aliases = {3: 0, 4: 1}
    return pl.pallas_call(
        functools.partial(_mm_kernel, scale=scale),
        grid=(n // tn, m // tm),
        in_specs=in_specs,
        out_specs=out_specs,
        out_shape=out_shape,
        input_output_aliases=aliases,
        scratch_shapes=[pltpu.VMEM((k, tn), bf16)],
        compiler_params=_params("parallel", "arbitrary"),
        name="proj_matmul",
    )(*args)


def _out_kernel(o_ref, w_ref, x_ref, g_ref, b_ref, xo_ref, xb_ref):
    y = jnp.dot(o_ref[...].astype(bf16), w_ref[...], preferred_element_type=f32)
    r = DEEPNORM_ALPHA * x_ref[...] + y
    mu = jnp.mean(r, axis=-1, keepdims=True)
    d = r - mu
    var = jnp.mean(d * d, axis=-1, keepdims=True)
    xn = d * lax.rsqrt(var + NORM_EPS) * g_ref[...] + b_ref[...]
    xo_ref[...] = xn
    xb_ref[...] = xn.astype(bf16)


def _out_proj_norm(o, w, x, g, b):
    m = x.shape[0]
    tm = min(m, 512)
    row = lambda i: (i, 0)
    fixed = lambda i: (0, 0)
    return pl.pallas_call(
        _out_kernel,
        grid=(m // tm,),
        in_specs=[pl.BlockSpec((tm, D_MODEL), row),
                  pl.BlockSpec((D_MODEL, D_MODEL), fixed),
                  pl.BlockSpec((tm, D_MODEL), row),
                  pl.BlockSpec((1, D_MODEL), fixed),
                  pl.BlockSpec((1, D_MODEL), fixed)],
        out_specs=[pl.BlockSpec((tm, D_MODEL), row), pl.BlockSpec((tm, D_MODEL), row)],
        out_shape=[jax.ShapeDtypeStruct((m, D_MODEL), f32),
                   jax.ShapeDtypeStruct((m, D_MODEL), bf16)],
        compiler_params=_params("parallel"),
        name="out_proj_deepnorm",
    )(o, w, x, g.reshape(1, D_MODEL), b.reshape(1, D_MODEL))


def _bias_by_distance(rel_bias, dist):
    n = jnp.asarray(dist, jnp.int32)
    max_exact = REL_BUCKETS // 2
    nf = jnp.maximum(n, 1).astype(f32)
    large = max_exact + (jnp.log(nf / max_exact) / math.log(REL_MAX_DIST / max_exact)
                         * (REL_BUCKETS - max_exact)).astype(jnp.int32)
    large = jnp.minimum(large, REL_BUCKETS - 1)
    bucket = jnp.where(n < max_exact, n, large)
    return jnp.moveaxis(rel_bias.astype(f32)[bucket], -1, 0)


def _prompt_bias_tiles(rel_bias, blk):
    k = np.arange(3 * blk - 1)
    d = 2 * blk - 1 - k
    u = jnp.where(jnp.asarray(d >= 0), _bias_by_distance(rel_bias, np.maximum(d, 0)), MASK_VALUE)
    period = 3 * blk
    flat = jnp.tile(jnp.pad(u, ((0, 0), (0, 1))), (1, blk))[:, :blk * (period - 1)]
    return flat.reshape(DIFF_HEADS, blk, period - 1)[:, :, blk - 1:3 * blk - 1]


def _decode_bias_tables(rel_bias, t_valid):
    col = np.arange(DEC_COLS)
    col_h = (col // SAMPLE_ROWS) % DIFF_HEADS
    reps = DEC_COLS // SAMPLE_ROWS
    tq = np.arange(SAMPLE_ROWS)[None, :]
    new_tokens = NEW_ROWS // DIFF_HEADS
    tab = _bias_by_distance(rel_bias, np.arange(2 * PAGE_SIZE + SAMPLE_ROWS + 1))

    def toeplitz_rows(ext, n_tokens):
        wins = [ext[:, t + 1:t + 1 + n_tokens][:, ::-1] for t in range(SAMPLE_ROWS)]
        return jnp.transpose(jnp.stack(wins), (2, 1, 0)).reshape(n_tokens * DIFF_HEADS, SAMPLE_ROWS)

    def expand(vals, visible):
        n_rows = vals.shape[0]
        r = np.arange(n_rows)
        ok = ((r % DIFF_HEADS)[:, None] == col_h[None, :]) & np.tile(visible, (1, reps))
        return jnp.where(jnp.asarray(ok), jnp.tile(vals, (1, reps)), MASK_VALUE)

    everything = np.ones((PAGE_ROWS, SAMPLE_ROWS), bool)
    far_vals = jnp.broadcast_to(jnp.tile(tab[:, 2 * PAGE_SIZE], PAGE_SIZE)[:, None],
                                (PAGE_ROWS, SAMPLE_ROWS))
    far = expand(far_vals, everything)
    last = expand(toeplitz_rows(tab, PAGE_SIZE), everything)
    s_new = (np.arange(NEW_ROWS) // DIFF_HEADS)[:, None]
    ext = jnp.pad(tab[:, :SAMPLE_ROWS], ((0, 0), (new_tokens, 0)))
    new = expand(toeplitz_rows(ext, new_tokens), (s_new <= tq) & (s_new < t_valid))
    return jnp.stack([far, last]), new


def _lambda_value(lp_ref, lam_init):
    lp = lp_ref[...]
    a = jnp.sum(lp[0:1] * lp[1:2], axis=-1, keepdims=True)
    b = jnp.sum(lp[2:3] * lp[3:4], axis=-1, keepdims=True)
    return jnp.exp(a) - jnp.exp(b) + lam_init


def _head_epilogue(o, g, z, out_scale):
    ms = jnp.mean(o * o, axis=-1, keepdims=True)
    return (o * lax.rsqrt(ms + NORM_EPS) * g * out_scale) * _silu(z)


def _attn_kernel(q_ref, k_ref, v_ref, z_ref, bias_ref, far_ref, lp_ref, g_ref, o_ref,
                 kb_ref, vb_ref, s_ref, *, lam_init, n_blocks):
    h = pl.program_id(1)
    i = pl.program_id(2)
    blk = ATT_BLOCK

    @pl.when(i == 0)
    def _():
        kb_ref[...] = k_ref[...].astype(bf16)
        vb_ref[...] = v_ref[...].astype(bf16)

    far = far_ref[h]
    lam = _lambda_value(lp_ref, lam_init)

    def lane_fold(x):
        return [x[:, c * LANES:(c + 1) * LANES] for c in range(blk // LANES)]

    def query_block(nvis):
        q = q_ref[...]
        qs = (q[:, :DIFF_DH], q[:, DIFF_DH:])
        mx = [None, None]
        for j in range(nvis):
            kj = kb_ref[j * blk:(j + 1) * blk, :]
            if j == nvis - 1:
                bias = bias_ref[:, blk:]
            elif j == nvis - 2:
                bias = bias_ref[:, :blk]
            else:
                bias = far
            for mp in range(2):
                s = lax.dot_general(qs[mp], kj[:, mp * DIFF_DH:(mp + 1) * DIFF_DH], _NT,
                                    preferred_element_type=f32) + bias
                s_ref[mp, j] = s
                for part in lane_fold(s):
                    mx[mp] = part if mx[mp] is None else jnp.maximum(mx[mp], part)
        m = [jnp.max(mx[mp], axis=-1, keepdims=True) for mp in range(2)]
        ls = [None, None]
        for j in range(nvis):
            for mp in range(2):
                p = jnp.exp(s_ref[mp, j] - m[mp])
                s_ref[mp, j] = p
                for part in lane_fold(p):
                    ls[mp] = part if ls[mp] is None else ls[mp] + part
        c1 = 1.0 / jnp.sum(ls[0], axis=-1, keepdims=True)
        c2 = lam / jnp.sum(ls[1], axis=-1, keepdims=True)
        acc = None
        for j in range(nvis):
            pd = (s_ref[0, j] * c1 - s_ref[1, j] * c2).astype(bf16)
            pv = jnp.dot(pd, vb_ref[j * blk:(j + 1) * blk, :], preferred_element_type=f32)
            acc = pv if acc is None else acc + pv
        o_ref[...] = _head_epilogue(acc, g_ref[...], z_ref[...], 1.0 - lam_init).astype(bf16)

    for c in range(n_blocks):
        pl.when(i == c)(functools.partial(query_block, c + 1))


def _prompt_attention(q, k, v, slot, z, bias_tiles, far, lam_p, norm_g, batch, seq, lam_init):
    nq = seq // ATT_BLOCK
    blk = ATT_BLOCK
    qmap = lambda b, h, i: (b * nq + i, h)
    kvmap = lambda b, h, i: (slot, b, h)
    return pl.pallas_call(
        functools.partial(_attn_kernel, lam_init=lam_init, n_blocks=nq),
        grid=(batch, DIFF_HEADS, nq),
        in_specs=[pl.BlockSpec((blk, DIFF_HW), qmap),
                  pl.BlockSpec((None, seq, DIFF_HW), kvmap),
                  pl.BlockSpec((None, seq, DIFF_HW), kvmap),
                  pl.BlockSpec((blk, DIFF_HW), qmap),
                  pl.BlockSpec((None, blk, 2 * blk), lambda b, h, i: (h, 0, 0)),
                  pl.BlockSpec(memory_space=pltpu.SMEM),
                  pl.BlockSpec((4, DIFF_DH), lambda b, h, i: (0, 0)),
                  pl.BlockSpec((1, DIFF_HW), lambda b, h, i: (0, 0))],
        out_specs=pl.BlockSpec((blk, DIFF_HW), qmap),
        out_shape=jax.ShapeDtypeStruct((batch * seq, D_MODEL), bf16),
        scratch_shapes=[pltpu.VMEM((seq, DIFF_HW), bf16),
                        pltpu.VMEM((seq, DIFF_HW), bf16),
                        pltpu.VMEM((2, nq, blk, blk), f32)],
        compiler_params=_params("parallel", "parallel", "arbitrary"),
        name="prompt_diff_attention",
    )(q, k, v, z, bias_tiles, far, lam_p, norm_g.reshape(1, DIFF_HW))


def _decode_kernel(pt_ref, *refs, lam_init, n_steps):
    del pt_ref
    npg = DEC_PAGES_PER_STEP
    k_pages = refs[:npg]
    v_pages = refs[npg:2 * npg]
    (w_ref, kn_ref, vn_ref, z_ref, bm_ref, bmn_ref, lp_ref, g_ref, o_ref,
     m_ref, l_ref, acc_ref) = refs[2 * npg:]
    s = pl.program_id(1)

    @pl.when(s == 0)
    def _():
        m_ref[...] = jnp.full(m_ref.shape, MASK_VALUE, f32)
        l_ref[...] = jnp.zeros(l_ref.shape, f32)
        acc_ref[...] = jnp.zeros(acc_ref.shape, f32)

    w = w_ref[...]
    eye = (lax.broadcasted_iota(jnp.int32, (DEC_COLS, DEC_COLS), 0)
           == lax.broadcasted_iota(jnp.int32, (DEC_COLS, DEC_COLS), 1))

    def to_column(row):
        return jnp.sum(jnp.where(eye, row, 0.0), axis=1, keepdims=True)

    def update(blocks):
        logits = [jnp.dot(k.astype(bf16), w, preferred_element_type=f32) + bm for k, _, bm in blocks]
        cmax = None
        for lg in logits:
            c = jnp.max(lg, axis=0, keepdims=True)
            cmax = c if cmax is None else jnp.maximum(cmax, c)
        m_old = m_ref[...]
        m_new = jnp.maximum(m_old, cmax)
        a = jnp.exp(m_old - m_new)
        lsum = None
        pv = None
        for lg, (_, v, _) in zip(logits, blocks):
            p = jnp.exp(lg - m_new)
            ps = jnp.sum(p, axis=0, keepdims=True)
            lsum = ps if lsum is None else lsum + ps
            d = jnp.dot(p.T.astype(bf16), v.astype(bf16), preferred_element_type=f32)
            pv = d if pv is None else pv + d
        m_ref[...] = m_new
        l_ref[...] = a * l_ref[...] + lsum
        acc_ref[...] = acc_ref[...] * to_column(a) + pv

    blocks = []
    for pg in range(npg):
        if pg == npg - 1:
            bm = bm_ref[jnp.where(s == n_steps - 1, 1, 0)]
        else:
            bm = bm_ref[0]
        blocks.append((k_pages[pg][...], v_pages[pg][...], bm))
    update(blocks)

    @pl.when(s == n_steps - 1)
    def _():
        update([(kn_ref[...], vn_ref[...], bmn_ref[...])])
        on = acc_ref[...] * (1.0 / to_column(l_ref[...]))
        lam = _lambda_value(lp_ref, lam_init)
        half = DEC_COLS // 2
        for h in range(DIFF_HEADS):
            r0 = h * SAMPLE_ROWS
            o = on[r0:r0 + SAMPLE_ROWS] - lam * on[half + r0:half + r0 + SAMPLE_ROWS]
            lo = h * DIFF_HW
            o_ref[:, lo:lo + DIFF_HW] = _head_epilogue(o, g_ref[...], z_ref[:, lo:lo + DIFF_HW],
                                                       1.0 - lam_init)


def _decode_attention(page_table, cache_k, cache_v, layer, q, k_new, v_new, z, bm, bm_new, lam_p,
                      norm_g, lam_init):
    nb, n_pages = page_table.shape
    npg = DEC_PAGES_PER_STEP
    n_steps = n_pages // npg
    ck = cache_k.reshape(cache_k.shape[0], cache_k.shape[1], PAGE_ROWS, DIFF_HW)
    cv = cache_v.reshape(cache_v.shape[0], cache_v.shape[1], PAGE_ROWS, DIFF_HW)

    q5 = q.reshape(nb, SAMPLE_ROWS, DIFF_HEADS, 2, DIFF_DH)
    qt = jnp.transpose(q5, (0, 3, 4, 2, 1)).reshape(nb, 2, DIFF_DH, DEC_COLS // 2)
    zeros = jnp.zeros_like(qt[:, 0])
    w = jnp.concatenate([jnp.concatenate([qt[:, 0], zeros], axis=-1),
                         jnp.concatenate([zeros, qt[:, 1]], axis=-1)], axis=1)

    def new_rows(a):
        a = a.reshape(nb, SAMPLE_ROWS, DIFF_HEADS, DIFF_HW)
        a = jnp.pad(a, ((0, 0), (0, NEW_ROWS // DIFF_HEADS - SAMPLE_ROWS), (0, 0), (0, 0)))
        return a.reshape(nb, NEW_ROWS, DIFF_HW)

    def page_spec(pg):
        return pl.BlockSpec((None, None, PAGE_ROWS, DIFF_HW),
                            lambda b, s, pt: (layer, pt[b, s * npg + pg], 0, 0))

    per_batch3 = lambda b, s, pt: (b, 0, 0)
    row = lambda b, s, pt: (b, 0)
    fixed2 = lambda b, s, pt: (0, 0)
    grid_spec = pltpu.PrefetchScalarGridSpec(
        num_scalar_prefetch=1,
        grid=(nb, n_steps),
        in_specs=([page_spec(pg) for pg in range(npg)] + [page_spec(pg) for pg in range(npg)]
                  + [pl.BlockSpec((None, DIFF_HW, DEC_COLS), per_batch3),
                     pl.BlockSpec((None, NEW_ROWS, DIFF_HW), per_batch3),
                     pl.BlockSpec((None, NEW_ROWS, DIFF_HW), per_batch3),
                     pl.BlockSpec((SAMPLE_ROWS, D_MODEL), row),
                     pl.BlockSpec((2, PAGE_ROWS, DEC_COLS), lambda b, s, pt: (0, 0, 0)),
                     pl.BlockSpec((NEW_ROWS, DEC_COLS), fixed2),
                     pl.BlockSpec((4, DIFF_DH), fixed2),
                     pl.BlockSpec((1, DIFF_HW), fixed2)]),
        out_specs=pl.BlockSpec((SAMPLE_ROWS, D_MODEL), row),
        scratch_shapes=[pltpu.VMEM((1, DEC_COLS), f32),
                        pltpu.VMEM((1, DEC_COLS), f32),
                        pltpu.VMEM((DEC_COLS, DIFF_HW), f32)])
    return pl.pallas_call(
        functools.partial(_decode_kernel, lam_init=lam_init, n_steps=n_steps),
        grid_spec=grid_spec,
        out_shape=jax.ShapeDtypeStruct((nb * SAMPLE_ROWS, D_MODEL), f32),
        compiler_params=_params("parallel", "arbitrary"),
        name="decode_diff_attention",
    )(page_table, *([ck] * npg), *([cv] * npg), w, new_rows(k_new), new_rows(v_new), z, bm, bm_new,
      lam_p, norm_g.reshape(1, DIFF_HW))


def _gate_kernel(x_ref, w1_ref, w2_ref, b_ref, g_ref):
    gl = jnp.dot(x_ref[...], w1_ref[...], preferred_element_type=f32)
    u = jnp.dot(gl.astype(bf16), w2_ref[...], preferred_element_type=f32) + b_ref[...]
    ls = jnp.minimum(u, 0.0) - jnp.log(1.0 + jnp.exp(-jnp.abs(u)))
    g_ref[...] = ls * (1.0 / GLA_GATE_NORMALIZER)


def _gla_gate(x, w1, w2, b):
    m = x.shape[0]
    tm = min(m, 512)
    return pl.pallas_call(
        _gate_kernel,
        grid=(m // tm,),
        in_specs=[pl.BlockSpec((tm, D_MODEL), lambda i: (i, 0)),
                  pl.BlockSpec((D_MODEL, LANES), lambda i: (0, 0)),
                  pl.BlockSpec((LANES, GLA_DK_W), lambda i: (0, 0)),
                  pl.BlockSpec((1, GLA_DK_W), lambda i: (0, 0))],
        out_specs=pl.BlockSpec((tm, GLA_DK_W), lambda i: (i, 0)),
        out_shape=jax.ShapeDtypeStruct((m, GLA_DK_W), f32),
        compiler_params=_params("parallel"),
        name="gla_gate",
    )(x, w1, w2, b.reshape(1, GLA_DK_W))


def _gla_kernel(*refs, chunk, t_valid, n_chunks, has_state):
    if has_state:
        q_ref, k_ref, v_ref, g_ref, z_ref, ng_ref, s0_ref, o_ref, so_ref, st_ref = refs
    else:
        q_ref, k_ref, v_ref, g_ref, z_ref, ng_ref, o_ref, so_ref, st_ref = refs
    n = pl.program_id(1)

    @pl.when(n == 0)
    def _():
        for h in range(GLA_HEADS):
            if has_state:
                st_ref[h] = s0_ref[h].T
            else:
                st_ref[h] = jnp.zeros(st_ref.shape[1:], f32)

    row = lax.broadcasted_iota(jnp.int32, (chunk, chunk), 0)
    col = lax.broadcasted_iota(jnp.int32, (chunk, chunk), 1)
    causal = row >= col
    tri = causal.astype(f32).astype(bf16)
    valid = lax.broadcasted_iota(jnp.int32, (chunk, GLA_DK), 0) < t_valid
    for h in range(GLA_HEADS):
        kc = slice(h * GLA_DK, (h + 1) * GLA_DK)
        vc = slice(h * GLA_DV, (h + 1) * GLA_DV)
        g = g_ref[:, kc]
        k = k_ref[:, kc]
        if t_valid < chunk:
            g = jnp.where(valid, g, 0.0)
            k = jnp.where(valid, k, 0.0)
        g_hi = g.astype(bf16)
        g_lo = (g - g_hi.astype(f32)).astype(bf16)
        b = (jnp.dot(tri, g_hi, preferred_element_type=f32)
             + jnp.dot(tri, g_lo, preferred_element_type=f32))
        b_last = b[chunk - 1:chunk, :]
        qe = (q_ref[:, kc] * jnp.exp(b)).astype(bf16)
        ke = (k * jnp.exp(-b)).astype(bf16)
        kd = (k * jnp.exp(b_last - b)).astype(bf16)
        vb = v_ref[:, vc].astype(bf16)
        a = lax.dot_general(qe, ke, _NT, preferred_element_type=f32)
        a = jnp.where(causal, a, 0.0).astype(bf16)
        st = st_ref[h]
        o = (jnp.dot(a, vb, preferred_element_type=f32)
             + lax.dot_general(qe, st.astype(bf16), _NT, preferred_element_type=f32))
        st_ref[h] = jnp.exp(b_last) * st + lax.dot_general(vb, kd, _TN, preferred_element_type=f32)
        o_ref[:, vc] = _head_epilogue(o, ng_ref[...], z_ref[:, vc], 1.0).astype(o_ref.dtype)

    @pl.when(n == n_chunks - 1)
    def _():
        for h in range(GLA_HEADS):
            so_ref[h] = st_ref[h].T


def _gla(q, k, v, g, z, norm_g, s0, layer, batch, seq, t_valid):
    chunk = min(GLA_CHUNK, seq)
    nc = seq // chunk
    has_state = s0 is not None
    out_dtype = bf16 if chunk % (2 * SUBLANES) == 0 else f32
    tmap = lambda b, n: (b * nc + n, 0)
    smap = lambda b, n: (b, 0, 0, 0)
    state_spec = pl.BlockSpec((None, GLA_HEADS, GLA_DK, GLA_DV), smap)
    in_specs = [pl.BlockSpec((chunk, GLA_DK_W), tmap),
                pl.BlockSpec((chunk, GLA_DK_W), tmap),
                pl.BlockSpec((chunk, D_MODEL), tmap),
                pl.BlockSpec((chunk, GLA_DK_W), tmap),
                pl.BlockSpec((chunk, D_MODEL), tmap),
                pl.BlockSpec((1, GLA_DV), lambda b, n: (0, 0))]
    args = [q, k, v, g, z, norm_g.reshape(1, GLA_DV)]
    if has_state:
        in_specs.append(pl.BlockSpec((None, None, GLA_HEADS, GLA_DK, GLA_DV),
                                     lambda b, n: (layer, b, 0, 0, 0)))
        args.append(s0)
    return pl.pallas_call(
        functools.partial(_gla_kernel, chunk=chunk, t_valid=t_valid, n_chunks=nc,
                          has_state=has_state),
        grid=(batch, nc),
        in_specs=in_specs,
        out_specs=[pl.BlockSpec((chunk, D_MODEL), tmap), state_spec],
        out_shape=[jax.ShapeDtypeStruct((batch * seq, D_MODEL), out_dtype),
                   jax.ShapeDtypeStruct((batch, GLA_HEADS, GLA_DK, GLA_DV), f32)],
        scratch_shapes=[pltpu.VMEM((GLA_HEADS, GLA_DV, GLA_DK), f32)],
        compiler_params=_params("parallel", "arbitrary"),
        name="gla_chunked",
    )(*args)


def _trunk(xp, xs, batch, seq, dec_batch, dec_seq, cache_k, cache_v, state_gla, page_table, weights):
    (rel_bias, diff_w_in, diff_lambda, diff_norm_g, diff_w_out, gla_w_in, gla_w_g1, gla_w_g2,
     gla_b_g, gla_norm_g, gla_w_out, ln_g, ln_b) = weights
    n_diff = (DEPTH + 1) // 2
    xpb = xp.astype(bf16)
    xsb = xs.astype(bf16)
    kbuf = vbuf = None
    states_p, states_s = [], []
    tiles = _prompt_bias_tiles(rel_bias, ATT_BLOCK)
    far = _bias_by_distance(rel_bias, np.array([2 * ATT_BLOCK]))[:, 0]
    bm, bm_new = _decode_bias_tables(rel_bias, dec_seq)
    for i in range(DEPTH):
        j = i // 2
        if i % 2 == 0:
            w = diff_w_in
            qp, qs = _matmul(xpb, xsb, w, j, 0, D_MODEL, bf16, scale=DIFF_DH ** -0.5)
            kbuf = _matmul(xpb, xsb, w, j, D_MODEL, D_MODEL, f32, slots=n_diff, slot=j, into=kbuf)
            vbuf = _matmul(xpb, xsb, w, j, 2 * D_MODEL, D_MODEL, f32, slots=n_diff, slot=j,
                           into=vbuf)
            zp, zs = _matmul(xpb, xsb, w, j, 3 * D_MODEL, D_MODEL, f32)
            lam_init = 0.8 - 0.6 * math.exp(-0.3 * i)
            op = _prompt_attention(qp, kbuf[0], vbuf[0], j, zp, tiles, far, diff_lambda[j],
                                   diff_norm_g[j], batch, seq, lam_init)
            os_ = _decode_attention(page_table, cache_k, cache_v, j, qs, kbuf[1][j], vbuf[1][j], zs,
                                    bm, bm_new, diff_lambda[j], diff_norm_g[j], lam_init)
            w_out = diff_w_out[j]
        else:
            w = gla_w_in
            qp, qs = _matmul(xpb, xsb, w, j, 0, GLA_DK_W, f32, scale=GLA_DK ** -0.5)
            kp, ks = _matmul(xpb, xsb, w, j, GLA_DK_W, GLA_DK_W, f32)
            vp, vs = _matmul(xpb, xsb, w, j, 2 * GLA_DK_W, D_MODEL, f32)
            zp, zs = _matmul(xpb, xsb, w, j, 2 * GLA_DK_W + D_MODEL, D_MODEL, f32)
            gp = _gla_gate(xpb, gla_w_g1[j], gla_w_g2[j], gla_b_g[j])
            gs = _gla_gate(xsb, gla_w_g1[j], gla_w_g2[j], gla_b_g[j])
            op, sp = _gla(qp, kp, vp, gp, zp, gla_norm_g[j], None, j, batch, seq, seq)
            os_, ss = _gla(qs, ks, vs, gs, zs, gla_norm_g[j], state_gla, j, dec_batch, SAMPLE_ROWS,
                           dec_seq)
            states_p.append(sp)
            states_s.append(ss)
            w_out = gla_w_out[j]
        xp, xpb = _out_proj_norm(op, w_out, xp, ln_g[i], ln_b[i])
        xs, xsb = _out_proj_norm(os_, w_out, xs, ln_g[i], ln_b[i])
    return xp, xs, kbuf, vbuf, states_p, states_s


def kernel(x_prompt, x_sample, cache_k, cache_v, state_gla, page_table, rel_bias, diff_w_in,
           diff_lambda, diff_norm_g, diff_w_out, gla_w_in, gla_w_g2, gla_b_g, gla_norm_g, gla_w_out,
           ln_g, ln_b):
    batch, seq, _ = x_prompt.shape
    dec_batch, dec_seq, _ = x_sample.shape
    n_pages = page_table.shape[1]
    assert n_pages >= 2 and n_pages % DEC_PAGES_PER_STEP == 0 and dec_seq <= SAMPLE_ROWS
    assert cache_k.shape[2:] == (PAGE_SIZE, DIFF_HEADS, DIFF_HW)
    assert seq % ATT_BLOCK == 0 and seq // ATT_BLOCK >= 2 and seq % GLA_CHUNK == 0

    gate_off = 2 * GLA_DK_W + 2 * D_MODEL
    w_g1 = jnp.pad(gla_w_in[:, :, gate_off:], ((0, 0), (0, 0), (0, LANES - GLA_GATE_RANK)))
    w_g2 = jnp.pad(gla_w_g2, ((0, 0), (0, LANES - GLA_GATE_RANK), (0, 0)))
    weights = (rel_bias, diff_w_in, diff_lambda, diff_norm_g, diff_w_out.astype(bf16),
               gla_w_in[:, :, :gate_off].astype(bf16), w_g1.astype(bf16), w_g2.astype(bf16), gla_b_g,
               gla_norm_g,
               gla_w_out.astype(bf16), ln_g, ln_b)

    rows = SAMPLE_ROWS
    xs = jnp.pad(x_sample, ((0, 0), (0, rows - dec_seq), (0, 0))).reshape(dec_batch * rows, D_MODEL)
    y_p, y_s, kbuf, vbuf, s_p, s_s = _trunk(
        x_prompt.reshape(batch * seq, D_MODEL), xs, batch, seq, dec_batch, dec_seq,
        cache_k, cache_v, state_gla, page_table, weights)

    def rows_out(a, b, t, t_keep):
        return a.reshape(a.shape[0], b, t, DIFF_HEADS, DIFF_HW)[:, :, :t_keep]

    return (y_p.reshape(batch, seq, D_MODEL),
            y_s.reshape(dec_batch, rows, D_MODEL)[:, :dec_seq],
            rows_out(kbuf[0], batch, seq, seq), rows_out(vbuf[0], batch, seq, seq), jnp.stack(s_p),
            rows_out(kbuf[1], dec_batch, rows, dec_seq), rows_out(vbuf[1], dec_batch, rows, dec_seq),
            jnp.stack(s_s))
```

```python
import functools
import math

import jax
import jax.numpy as jnp
import numpy as np
from jax import lax
from jax.experimental import pallas as pl
from jax.experimental.pallas import tpu as pltpu

D_MODEL = 2048
DEPTH = 4
PAGE_SIZE = 128

DIFF_HEADS = 8
DIFF_DH = D_MODEL // (2 * DIFF_HEADS)
DIFF_HW = 2 * DIFF_DH

GLA_HEADS = 4
GLA_DK_W = D_MODEL // 2
GLA_DK = GLA_DK_W // GLA_HEADS
GLA_DV = D_MODEL // GLA_HEADS
GLA_GATE_RANK = 16
GLA_GATE_NORMALIZER = 16.0
GLA_CHUNK = 64

REL_BUCKETS = 32
REL_MAX_DIST = 128

NORM_EPS = 1e-5
DEEPNORM_ALPHA = (2 * DEPTH) ** 0.25

LANES = 128
SUBLANES = 8
VMEM_LIMIT = 48 * 1024 * 1024
MASK_VALUE = -1e30

ATT_BLOCK = 256
DEC_PAGES_PER_STEP = 8
DEC_GROUPS = 4
DEC_VMEM_LIMIT = 56 * 1024 * 1024
SAMPLE_ROWS = SUBLANES
PAGE_ROWS = PAGE_SIZE * DIFF_HEADS
NEW_ROWS = 2 * SAMPLE_ROWS * DIFF_HEADS
DEC_COLS = 2 * DIFF_HEADS * SAMPLE_ROWS
assert DEC_COLS == LANES and NEW_ROWS == LANES

_NT = (((1,), (1,)), ((), ()))
_TN = (((0,), (0,)), ((), ()))

bf16 = jnp.bfloat16
f32 = jnp.float32


def _params(*sem):
    return pltpu.CompilerParams(dimension_semantics=sem, vmem_limit_bytes=VMEM_LIMIT)


def _silu(z):
    return z * (1.0 / (1.0 + jnp.exp(-z)))


def _mm_kernel(xp_ref, xs_ref, w_ref, *rest, scale, w_transposed):
    op_ref, os_ref, wb_ref = rest[-3:]
    dims = _NT if w_transposed else (((1,), (0,)), ((), ()))

    def project(x_ref, o_ref):
        acc = lax.dot_general(x_ref[...], wb_ref[...], dims, preferred_element_type=f32)
        if scale != 1.0:
            acc = acc * scale
        o_ref[...] = acc.astype(o_ref.dtype)

    @pl.when(pl.program_id(1) == 0)
    def _():
        wb_ref[...] = w_ref[...].astype(bf16)
        project(xs_ref, os_ref)

    project(xp_ref, op_ref)


def _matmul(xp, xs, w, layer, col_off, n, out_dtype, scale=1.0, slots=None, slot=0, into=None,
            w_transposed=False, after=()):
    m, k = xp.shape
    ms = xs.shape[0]
    tm = min(m, 1024)
    tn = min(n, 1024)
    assert m % tm == 0 and n % tn == 0 and col_off % tn == 0
    off = col_off // tn
    if w_transposed:
        w_spec = pl.BlockSpec((None, tn, k), lambda j, i: (layer, j + off, 0))
        w_tile = (tn, k)
    else:
        w_spec = pl.BlockSpec((None, k, tn), lambda j, i: (layer, 0, j + off))
        w_tile = (k, tn)
    in_specs = [pl.BlockSpec((tm, k), lambda j, i: (i, 0)),
                pl.BlockSpec((ms, k), lambda j, i: (0, 0)),
                w_spec]
    args = [xp, xs, w]
    aliases = {}
    if slots is None:
        out_specs = [pl.BlockSpec((tm, tn), lambda j, i: (i, j)),
                     pl.BlockSpec((ms, tn), lambda j, i: (0, j))]
        out_shape = [jax.ShapeDtypeStruct((m, n), out_dtype),
                     jax.ShapeDtypeStruct((ms, n), out_dtype)]
    else:
        out_specs = [pl.BlockSpec((None, tm, tn), lambda j, i: (slot, i, j)),
                     pl.BlockSpec((None, ms, tn), lambda j, i: (slot, 0, j))]
        out_shape = [jax.ShapeDtypeStruct((slots, m, n), out_dtype),
                     jax.ShapeDtypeStruct((slots, ms, n), out_dtype)]
        if into is not None:
            in_specs += [pl.BlockSpec(memory_space=pl.ANY)] * 2
            args += list(into)
            aliases = {3: 0, 4: 1}
    in_specs += [pl.BlockSpec(memory_space=pl.ANY)] * len(after)
    args += list(after)
    return pl.pallas_call(
        functools.partial(_mm_kernel, scale=scale, w_transposed=w_transposed),
        grid=(n // tn, m // tm),
        in_specs=in_specs,
        out_specs=out_specs,
        out_shape=out_shape,
        input_output_aliases=aliases,
        scratch_shapes=[pltpu.VMEM(w_tile, bf16)],
        compiler_params=_params("parallel", "arbitrary"),
        name="proj_matmul",
    )(*args)


def _out_kernel(*refs, tiles_per_batch):
    if tiles_per_batch is None:
        o_ref, w_ref, x_ref, g_ref, b_ref, xo_ref, xb_ref = refs
        o = o_ref[...].astype(bf16)
    else:
        olo_ref, ohi_ref, w_ref, x_ref, g_ref, b_ref, xo_ref, xb_ref = refs
        first_half = (pl.program_id(0) % tiles_per_batch) < tiles_per_batch // 2
        o = jnp.where(first_half, olo_ref[...], ohi_ref[...])
    y = jnp.dot(o, w_ref[...], preferred_element_type=f32)
    r = DEEPNORM_ALPHA * x_ref[...] + y
    mu = jnp.mean(r, axis=-1, keepdims=True)
    d = r - mu
    var = jnp.mean(d * d, axis=-1, keepdims=True)
    xn = d * lax.rsqrt(var + NORM_EPS) * g_ref[...] + b_ref[...]
    xo_ref[...] = xn
    xb_ref[...] = xn.astype(bf16)


def _out_proj_norm(o, w, x, g, b, seq=None):
    m = x.shape[0]
    tm = min(m, 512)
    row = lambda i: (i, 0)
    fixed = lambda i: (0, 0)
    if isinstance(o, (tuple, list)):
        tpb = seq // tm
        half = tpb // 2
        assert seq % tm == 0 and tpb % 2 == 0
        o_specs = [pl.BlockSpec((tm, D_MODEL),
                                lambda i: ((i // tpb) * half + jnp.minimum(i % tpb, half - 1), 0)),
                   pl.BlockSpec((tm, D_MODEL),
                                lambda i: ((i // tpb) * half + jnp.maximum(i % tpb - half, 0), 0))]
        o_args = list(o)
    else:
        tpb = None
        o_specs = [pl.BlockSpec((tm, D_MODEL), row)]
        o_args = [o]
    return pl.pallas_call(
        functools.partial(_out_kernel, tiles_per_batch=tpb),
        grid=(m // tm,),
        in_specs=o_specs + [
                  pl.BlockSpec((D_MODEL, D_MODEL), fixed),
                  pl.BlockSpec((tm, D_MODEL), row),
                  pl.BlockSpec((1, D_MODEL), fixed),
                  pl.BlockSpec((1, D_MODEL), fixed)],
        out_specs=[pl.BlockSpec((tm, D_MODEL), row), pl.BlockSpec((tm, D_MODEL), row)],
        out_shape=[jax.ShapeDtypeStruct((m, D_MODEL), f32),
                   jax.ShapeDtypeStruct((m, D_MODEL), bf16)],
        compiler_params=_params("parallel"),
        name="out_proj_deepnorm",
    )(*o_args, w, x, g.reshape(1, D_MODEL), b.reshape(1, D_MODEL))


def _bias_by_distance(rel_bias, dist):
    n = jnp.asarray(dist, jnp.int32)
    max_exact = REL_BUCKETS // 2
    nf = jnp.maximum(n, 1).astype(f32)
    large = max_exact + (jnp.log(nf / max_exact) / math.log(REL_MAX_DIST / max_exact)
                         * (REL_BUCKETS - max_exact)).astype(jnp.int32)
    large = jnp.minimum(large, REL_BUCKETS - 1)
    bucket = jnp.where(n < max_exact, n, large)
    return jnp.moveaxis(rel_bias.astype(f32)[bucket], -1, 0)


def _prompt_bias_tiles(rel_bias, blk):
    k = np.arange(3 * blk - 1)
    d = 2 * blk - 1 - k
    u = jnp.where(jnp.asarray(d >= 0), _bias_by_distance(rel_bias, np.maximum(d, 0)), MASK_VALUE)
    period = 3 * blk
    flat = jnp.tile(jnp.pad(u, ((0, 0), (0, 1))), (1, blk))[:, :blk * (period - 1)]
    return flat.reshape(DIFF_HEADS, blk, period - 1)[:, :, blk - 1:3 * blk - 1]


def _decode_bias_tables(rel_bias, t_valid):
    col = np.arange(DEC_COLS)
    col_h = (col // SAMPLE_ROWS) % DIFF_HEADS
    reps = DEC_COLS // SAMPLE_ROWS
    tq = np.arange(SAMPLE_ROWS)[None, :]
    new_tokens = NEW_ROWS // DIFF_HEADS
    tab = _bias_by_distance(rel_bias, np.arange(2 * PAGE_SIZE + SAMPLE_ROWS + 1))

    def toeplitz_rows(ext, n_tokens):
        wins = [ext[:, t + 1:t + 1 + n_tokens][:, ::-1] for t in range(SAMPLE_ROWS)]
        return jnp.transpose(jnp.stack(wins), (2, 1, 0)).reshape(n_tokens * DIFF_HEADS, SAMPLE_ROWS)

    def expand(vals, visible):
        n_rows = vals.shape[0]
        r = np.arange(n_rows)
        ok = ((r % DIFF_HEADS)[:, None] == col_h[None, :]) & np.tile(visible, (1, reps))
        return jnp.where(jnp.asarray(ok), jnp.tile(vals, (1, reps)), MASK_VALUE)

    everything = np.ones((PAGE_ROWS, SAMPLE_ROWS), bool)
    far_vals = jnp.broadcast_to(jnp.tile(tab[:, 2 * PAGE_SIZE], PAGE_SIZE)[:, None],
                                (PAGE_ROWS, SAMPLE_ROWS))
    far = expand(far_vals, everything)
    last = expand(toeplitz_rows(tab, PAGE_SIZE), everything)
    s_new = (np.arange(NEW_ROWS) // DIFF_HEADS)[:, None]
    ext = jnp.pad(tab[:, :SAMPLE_ROWS], ((0, 0), (new_tokens, 0)))
    new = expand(toeplitz_rows(ext, new_tokens), (s_new <= tq) & (s_new < t_valid))
    return jnp.stack([far, last]), new


def _lambda_value(lp_ref, lam_init):
    lp = lp_ref[...]
    a = jnp.sum(lp[0:1] * lp[1:2], axis=-1, keepdims=True)
    b = jnp.sum(lp[2:3] * lp[3:4], axis=-1, keepdims=True)
    return jnp.exp(a) - jnp.exp(b) + lam_init


def _head_epilogue(o, g, z, out_scale):
    ms = jnp.mean(o * o, axis=-1, keepdims=True)
    return (o * lax.rsqrt(ms + NORM_EPS) * g * out_scale) * _silu(z)


def _attn_kernel(qlo_ref, qhi_ref, k_ref, v_ref, zlo_ref, zhi_ref, bias_ref, far_ref, lp_ref, g_ref,
                 olo_ref, ohi_ref, kb_ref, vb_ref, s_ref, *, lam_init, n_blocks):
    h = pl.program_id(1)
    i = pl.program_id(2)
    blk = ATT_BLOCK

    @pl.when(i == 0)
    def _():
        kb_ref[...] = k_ref[...].astype(bf16)
        vb_ref[...] = v_ref[...].astype(bf16)

    far = far_ref[h]
    lam = _lambda_value(lp_ref, lam_init)

    def lane_fold(x):
        return [x[:, c * LANES:(c + 1) * LANES] for c in range(blk // LANES)]

    def logits_and_max(q_ref, nvis, base):
        q = q_ref[...]
        qs = (q[:, :DIFF_DH], q[:, DIFF_DH:])
        mx = [None, None]
        for j in range(nvis):
            kj = kb_ref[j * blk:(j + 1) * blk, :]
            if j == nvis - 1:
                bias = bias_ref[:, blk:]
            elif j == nvis - 2:
                bias = bias_ref[:, :blk]
            else:
                bias = far
            for mp in range(2):
                s = lax.dot_general(qs[mp], kj[:, mp * DIFF_DH:(mp + 1) * DIFF_DH], _NT,
                                    preferred_element_type=f32) + bias
                s_ref[mp, base + j] = s
                for part in lane_fold(s):
                    mx[mp] = part if mx[mp] is None else jnp.maximum(mx[mp], part)
        return [jnp.max(mx[mp], axis=-1, keepdims=True) for mp in range(2)]

    def exponentials(nvis, base, m):
        ls = [None, None]
        for j in range(nvis):
            for mp in range(2):
                p = jnp.exp(s_ref[mp, base + j] - m[mp])
                s_ref[mp, base + j] = p
                for part in lane_fold(p):
                    ls[mp] = part if ls[mp] is None else ls[mp] + part
        return (1.0 / jnp.sum(ls[0], axis=-1, keepdims=True),
                lam / jnp.sum(ls[1], axis=-1, keepdims=True))

    def weighted_values(nvis, base, c, z_ref, o_ref):
        acc = None
        for j in range(nvis):
            pd = (s_ref[0, base + j] * c[0] - s_ref[1, base + j] * c[1]).astype(bf16)
            pv = jnp.dot(pd, vb_ref[j * blk:(j + 1) * blk, :], preferred_element_type=f32)
            acc = pv if acc is None else acc + pv
        o_ref[...] = _head_epilogue(acc, g_ref[...], z_ref[...], 1.0 - lam_init).astype(bf16)

    def block_pair(lo):
        n_lo, n_hi = lo + 1, n_blocks - lo
        m_lo = logits_and_max(qlo_ref, n_lo, 0)
        m_hi = logits_and_max(qhi_ref, n_hi, n_lo)
        c_lo = exponentials(n_lo, 0, m_lo)
        c_hi = exponentials(n_hi, n_lo, m_hi)
        weighted_values(n_lo, 0, c_lo, zlo_ref, olo_ref)
        weighted_values(n_hi, n_lo, c_hi, zhi_ref, ohi_ref)

    for lo in range(n_blocks // 2):
        pl.when(i == lo)(functools.partial(block_pair, lo))


def _prompt_attention(q, k, v, slot, z, bias_tiles, far, lam_p, norm_g, batch, seq, lam_init):
    nq = seq // ATT_BLOCK
    half = nq // 2
    blk = ATT_BLOCK
    lo_in = lambda b, h, i: (b * nq + i, h)
    hi_in = lambda b, h, i: (b * nq + nq - 1 - i, h)
    lo_out = lambda b, h, i: (b * half + i, h)
    hi_out = lambda b, h, i: (b * half + half - 1 - i, h)
    kvmap = lambda b, h, i: (slot, b, h)
    out = jax.ShapeDtypeStruct((batch * seq // 2, D_MODEL), bf16)
    return pl.pallas_call(
        functools.partial(_attn_kernel, lam_init=lam_init, n_blocks=nq),
        grid=(batch, DIFF_HEADS, half),
        in_specs=[pl.BlockSpec((blk, DIFF_HW), lo_in),
                  pl.BlockSpec((blk, DIFF_HW), hi_in),
                  pl.BlockSpec((None, seq, DIFF_HW), kvmap),
                  pl.BlockSpec((None, seq, DIFF_HW), kvmap),
                  pl.BlockSpec((blk, DIFF_HW), lo_in),
                  pl.BlockSpec((blk, DIFF_HW), hi_in),
                  pl.BlockSpec((None, blk, 2 * blk), lambda b, h, i: (h, 0, 0)),
                  pl.BlockSpec(memory_space=pltpu.SMEM),
                  pl.BlockSpec((4, DIFF_DH), lambda b, h, i: (0, 0)),
                  pl.BlockSpec((1, DIFF_HW), lambda b, h, i: (0, 0))],
        out_specs=[pl.BlockSpec((blk, DIFF_HW), lo_out), pl.BlockSpec((blk, DIFF_HW), hi_out)],
        out_shape=[out, out],
        scratch_shapes=[pltpu.VMEM((seq, DIFF_HW), bf16),
                        pltpu.VMEM((seq, DIFF_HW), bf16),
                        pltpu.VMEM((2, nq + 1, blk, blk), f32)],
        compiler_params=_params("parallel", "parallel", "arbitrary"),
        name="prompt_diff_attention",
    )(q, q, k, v, z, z, bias_tiles, far, lam_p, norm_g.reshape(1, DIFF_HW))


def _decode_kernel(pt_ref, *refs, lam_init, n_steps):
    del pt_ref
    npg = DEC_PAGES_PER_STEP
    k_pages = refs[:npg]
    v_pages = refs[npg:2 * npg]
    (w_ref, kn_ref, vn_ref, z_ref, bm_ref, bmn_ref, lp_ref, g_ref, o_ref,
     m_ref, l_ref, acc_ref) = refs[2 * npg:]
    s = pl.program_id(1)

    @pl.when(s == 0)
    def _():
        m_ref[...] = jnp.full(m_ref.shape, MASK_VALUE, f32)
        l_ref[...] = jnp.zeros(l_ref.shape, f32)
        acc_ref[...] = jnp.zeros(acc_ref.shape, f32)

    w = w_ref[...]
    eye = (lax.broadcasted_iota(jnp.int32, (DEC_COLS, DEC_COLS), 0)
           == lax.broadcasted_iota(jnp.int32, (DEC_COLS, DEC_COLS), 1))

    def to_column(row):
        return jnp.sum(jnp.where(eye, row, 0.0), axis=1, keepdims=True)

    def update(grp, blocks):
        logits = [jnp.dot(k, w, preferred_element_type=f32) + bm for k, _, bm in blocks]
        cmax = None
        for lg in logits:
            c = jnp.max(lg, axis=0, keepdims=True)
            cmax = c if cmax is None else jnp.maximum(cmax, c)
        m_old = m_ref[grp]
        m_new = jnp.maximum(m_old, cmax)
        a = jnp.exp(m_old - m_new)
        lsum = None
        pv = None
        for lg, (_, v, _) in zip(logits, blocks):
            p = jnp.exp(lg - m_new)
            ps = jnp.sum(p, axis=0, keepdims=True)
            lsum = ps if lsum is None else lsum + ps
            d = jnp.dot(p.T.astype(bf16), v.astype(bf16), preferred_element_type=f32)
            pv = d if pv is None else pv + d
        m_ref[grp] = m_new
        l_ref[grp] = a * l_ref[grp] + lsum
        acc_ref[grp] = acc_ref[grp] * to_column(a) + pv

    per_group = npg // DEC_GROUPS
    for grp in range(DEC_GROUPS):
        blocks = []
        for pg in range(grp * per_group, (grp + 1) * per_group):
            if pg == npg - 1:
                bm = bm_ref[jnp.where(s == n_steps - 1, 1, 0)]
            else:
                bm = bm_ref[0]
            blocks.append((k_pages[pg][...], v_pages[pg][...], bm))
        update(grp, blocks)

    @pl.when(s == n_steps - 1)
    def _():
        update(0, [(kn_ref[...], vn_ref[...], bmn_ref[...])])
        m_all = m_ref[0]
        for grp in range(1, DEC_GROUPS):
            m_all = jnp.maximum(m_all, m_ref[grp])
        l_all = None
        acc_all = None
        for grp in range(DEC_GROUPS):
            a = jnp.exp(m_ref[grp] - m_all)
            lg = a * l_ref[grp]
            ag = acc_ref[grp] * to_column(a)
            l_all = lg if l_all is None else l_all + lg
            acc_all = ag if acc_all is None else acc_all + ag
        on = acc_all * (1.0 / to_column(l_all))
        lam = _lambda_value(lp_ref, lam_init)
        half = DEC_COLS // 2
        for h in range(DIFF_HEADS):
            r0 = h * SAMPLE_ROWS
            o = on[r0:r0 + SAMPLE_ROWS] - lam * on[half + r0:half + r0 + SAMPLE_ROWS]
            lo = h * DIFF_HW
            o_ref[:, lo:lo + DIFF_HW] = _head_epilogue(o, g_ref[...], z_ref[:, lo:lo + DIFF_HW],
                                                       1.0 - lam_init)


def _decode_attention(page_table, cache_k, cache_v, layer, q, k_new, v_new, z, bm, bm_new, lam_p,
                      norm_g, lam_init):
    nb, n_pages = page_table.shape
    npg = DEC_PAGES_PER_STEP
    n_steps = n_pages // npg
    ck = cache_k.reshape(cache_k.shape[0], cache_k.shape[1], PAGE_ROWS, DIFF_HW)
    cv = cache_v.reshape(cache_v.shape[0], cache_v.shape[1], PAGE_ROWS, DIFF_HW)

    q5 = q.reshape(nb, SAMPLE_ROWS, DIFF_HEADS, 2, DIFF_DH)
    qt = jnp.transpose(q5, (0, 3, 4, 2, 1)).reshape(nb, 2, DIFF_DH, DEC_COLS // 2)
    zeros = jnp.zeros_like(qt[:, 0])
    w = jnp.concatenate([jnp.concatenate([qt[:, 0], zeros], axis=-1),
                         jnp.concatenate([zeros, qt[:, 1]], axis=-1)], axis=1)

    def new_rows(a):
        a = a.reshape(nb, SAMPLE_ROWS, DIFF_HEADS, DIFF_HW)
        a = jnp.pad(a, ((0, 0), (0, NEW_ROWS // DIFF_HEADS - SAMPLE_ROWS), (0, 0), (0, 0)))
        return a.reshape(nb, NEW_ROWS, DIFF_HW)

    def page_spec(pg):
        return pl.BlockSpec((None, None, PAGE_ROWS, DIFF_HW),
                            lambda b, s, pt: (layer, pt[b, s * npg + pg], 0, 0))

    per_batch3 = lambda b, s, pt: (b, 0, 0)
    row = lambda b, s, pt: (b, 0)
    fixed2 = lambda b, s, pt: (0, 0)
    grid_spec = pltpu.PrefetchScalarGridSpec(
        num_scalar_prefetch=1,
        grid=(nb, n_steps),
        in_specs=([page_spec(pg) for pg in range(npg)] + [page_spec(pg) for pg in range(npg)]
                  + [pl.BlockSpec((None, DIFF_HW, DEC_COLS), per_batch3),
                     pl.BlockSpec((None, NEW_ROWS, DIFF_HW), per_batch3),
                     pl.BlockSpec((None, NEW_ROWS, DIFF_HW), per_batch3),
                     pl.BlockSpec((SAMPLE_ROWS, D_MODEL), row),
                     pl.BlockSpec((2, PAGE_ROWS, DEC_COLS), lambda b, s, pt: (0, 0, 0)),
                     pl.BlockSpec((NEW_ROWS, DEC_COLS), fixed2),
                     pl.BlockSpec((4, DIFF_DH), fixed2),
                     pl.BlockSpec((1, DIFF_HW), fixed2)]),
        out_specs=pl.BlockSpec((SAMPLE_ROWS, D_MODEL), row),
        scratch_shapes=[pltpu.VMEM((DEC_GROUPS, 1, DEC_COLS), f32),
                        pltpu.VMEM((DEC_GROUPS, 1, DEC_COLS), f32),
                        pltpu.VMEM((DEC_GROUPS, DEC_COLS, DIFF_HW), f32)])
    return pl.pallas_call(
        functools.partial(_decode_kernel, lam_init=lam_init, n_steps=n_steps),
        grid_spec=grid_spec,
        out_shape=jax.ShapeDtypeStruct((nb * SAMPLE_ROWS, D_MODEL), f32),
        compiler_params=pltpu.CompilerParams(dimension_semantics=("parallel", "arbitrary"),
                                             vmem_limit_bytes=DEC_VMEM_LIMIT),
        name="decode_diff_attention",
    )(page_table, *([ck] * npg), *([cv] * npg), w, new_rows(k_new), new_rows(v_new), z, bm, bm_new,
      lam_p, norm_g.reshape(1, DIFF_HW))


def _gate_kernel(x_ref, w1_ref, w2_ref, b_ref, g_ref):
    gl = lax.dot_general(x_ref[...], w1_ref[...], _NT, preferred_element_type=f32)
    u = jnp.dot(gl.astype(bf16), w2_ref[...], preferred_element_type=f32) + b_ref[...]
    ls = jnp.minimum(u, 0.0) - jnp.log(1.0 + jnp.exp(-jnp.abs(u)))
    g_ref[...] = ls * (1.0 / GLA_GATE_NORMALIZER)


def _gla_gate(x, w1, w2, b):
    m = x.shape[0]
    tm = min(m, 512)
    return pl.pallas_call(
        _gate_kernel,
        grid=(m // tm,),
        in_specs=[pl.BlockSpec((tm, D_MODEL), lambda i: (i, 0)),
                  pl.BlockSpec((LANES, D_MODEL), lambda i: (0, 0)),
                  pl.BlockSpec((LANES, GLA_DK_W), lambda i: (0, 0)),
                  pl.BlockSpec((1, GLA_DK_W), lambda i: (0, 0))],
        out_specs=pl.BlockSpec((tm, GLA_DK_W), lambda i: (i, 0)),
        out_shape=jax.ShapeDtypeStruct((m, GLA_DK_W), f32),
        compiler_params=_params("parallel"),
        name="gla_gate",
    )(x, w1, w2, b.reshape(1, GLA_DK_W))


def _gla_kernel(*refs, chunk, t_valid, n_chunks, has_state):
    if has_state:
        q_ref, k_ref, v_ref, g_ref, z_ref, ng_ref, s0_ref, o_ref, so_ref, st_ref = refs
    else:
        q_ref, k_ref, v_ref, g_ref, z_ref, ng_ref, o_ref, so_ref, st_ref = refs
    n = pl.program_id(1)

    @pl.when(n == 0)
    def _():
        for h in range(GLA_HEADS):
            if has_state:
                st_ref[h] = s0_ref[h]
            else:
                st_ref[h] = jnp.zeros(st_ref.shape[1:], f32)

    row = lax.broadcasted_iota(jnp.int32, (chunk, chunk), 0)
    col = lax.broadcasted_iota(jnp.int32, (chunk, chunk), 1)
    causal = row >= col
    tri = causal.astype(f32).astype(bf16)
    heads = range(GLA_HEADS)
    kc = [slice(h * GLA_DK, (h + 1) * GLA_DK) for h in heads]
    vc = [slice(h * GLA_DV, (h + 1) * GLA_DV) for h in heads]
    g = g_ref[...]
    k = k_ref[...]
    if t_valid < chunk:
        valid = lax.broadcasted_iota(jnp.int32, (chunk, GLA_DK_W), 0) < t_valid
        g = jnp.where(valid, g, 0.0)
        k = jnp.where(valid, k, 0.0)
    g_hi = g.astype(bf16)
    g_lo = (g - g_hi.astype(f32)).astype(bf16)
    b = (jnp.dot(tri, g_hi, preferred_element_type=f32)
         + jnp.dot(tri, g_lo, preferred_element_type=f32))
    b_last = b[chunk - 1:chunk, :]
    qe = (q_ref[...] * jnp.exp(b)).astype(bf16)
    ke = (k * jnp.exp(-b)).astype(bf16)
    kd = (k * jnp.exp(b_last - b)).astype(bf16)
    vb = v_ref[...].astype(bf16)
    decay = jnp.broadcast_to(jnp.exp(b_last), (SUBLANES, GLA_DK_W)).T[:, :1]
    a = [lax.dot_general(qe[:, kc[h]], ke[:, kc[h]], _NT, preferred_element_type=f32) for h in heads]
    a = [jnp.where(causal, a[h], 0.0).astype(bf16) for h in heads]
    st = [st_ref[h] for h in heads]
    o = [jnp.dot(a[h], vb[:, vc[h]], preferred_element_type=f32)
         + jnp.dot(qe[:, kc[h]], st[h].astype(bf16), preferred_element_type=f32) for h in heads]
    for h in heads:
        st_ref[h] = decay[kc[h]] * st[h] + lax.dot_general(kd[:, kc[h]], vb[:, vc[h]], _TN,
                                                           preferred_element_type=f32)
    for h in heads:
        o_ref[:, vc[h]] = _head_epilogue(o[h], ng_ref[...], z_ref[:, vc[h]], 1.0).astype(o_ref.dtype)

    @pl.when(n == n_chunks - 1)
    def _():
        for h in range(GLA_HEADS):
            so_ref[h] = st_ref[h]


def _gla(q, k, v, g, z, norm_g, s0, layer, batch, seq, t_valid):
    chunk = min(GLA_CHUNK, seq)
    nc = seq // chunk
    has_state = s0 is not None
    out_dtype = bf16 if chunk % (2 * SUBLANES) == 0 else f32
    tmap = lambda b, n: (b * nc + n, 0)
    smap = lambda b, n: (b, 0, 0, 0)
    state_spec = pl.BlockSpec((None, GLA_HEADS, GLA_DK, GLA_DV), smap)
    in_specs = [pl.BlockSpec((chunk, GLA_DK_W), tmap),
                pl.BlockSpec((chunk, GLA_DK_W), tmap),
                pl.BlockSpec((chunk, D_MODEL), tmap),
                pl.BlockSpec((chunk, GLA_DK_W), tmap),
                pl.BlockSpec((chunk, D_MODEL), tmap),
                pl.BlockSpec((1, GLA_DV), lambda b, n: (0, 0))]
    args = [q, k, v, g, z, norm_g.reshape(1, GLA_DV)]
    if has_state:
        in_specs.append(pl.BlockSpec((None, None, GLA_HEADS, GLA_DK, GLA_DV),
                                     lambda b, n: (layer, b, 0, 0, 0)))
        args.append(s0)
    return pl.pallas_call(
        functools.partial(_gla_kernel, chunk=chunk, t_valid=t_valid, n_chunks=nc,
                          has_state=has_state),
        grid=(batch, nc),
        in_specs=in_specs,
        out_specs=[pl.BlockSpec((chunk, D_MODEL), tmap), state_spec],
        out_shape=[jax.ShapeDtypeStruct((batch * seq, D_MODEL), out_dtype),
                   jax.ShapeDtypeStruct((batch, GLA_HEADS, GLA_DK, GLA_DV), f32)],
        scratch_shapes=[pltpu.VMEM((GLA_HEADS, GLA_DK, GLA_DV), f32)],
        compiler_params=_params("parallel", "arbitrary"),
        name="gla_chunked",
    )(*args)


def _trunk(xp, xs, batch, seq, dec_batch, dec_seq, cache_k, cache_v, state_gla, page_table, weights):
    (rel_bias, diff_w_in, diff_lambda, diff_norm_g, diff_w_out, gla_w_in, gla_w_g1, gla_w_g2,
     gla_b_g, gla_norm_g, gla_w_out, ln_g, ln_b) = weights
    n_diff = (DEPTH + 1) // 2
    xpb = xp.astype(bf16)
    xsb = xs.astype(bf16)
    kbuf = vbuf = None
    states_p, states_s = [], []
    tiles = _prompt_bias_tiles(rel_bias, ATT_BLOCK)
    far = _bias_by_distance(rel_bias, np.array([2 * ATT_BLOCK]))[:, 0]
    bm, bm_new = _decode_bias_tables(rel_bias, dec_seq)
    for i in range(DEPTH):
        j = i // 2
        if i % 2 == 0:
            w = diff_w_in
            qp, qs = _matmul(xpb, xsb, w, j, 0, D_MODEL, bf16, scale=DIFF_DH ** -0.5)
            zp, zs = _matmul(xpb, xsb, w, j, 3 * D_MODEL, D_MODEL, f32, after=(qp,))
            kbuf = _matmul(xpb, xsb, w, j, D_MODEL, D_MODEL, f32, slots=n_diff, slot=j, into=kbuf,
                           after=(zp,))
            vbuf = _matmul(xpb, xsb, w, j, 2 * D_MODEL, D_MODEL, f32, slots=n_diff, slot=j,
                           into=vbuf, after=(kbuf[0],))
            lam_init = 0.8 - 0.6 * math.exp(-0.3 * i)
            op = _prompt_attention(qp, kbuf[0], vbuf[0], j, zp, tiles, far, diff_lambda[j],
                                   diff_norm_g[j], batch, seq, lam_init)
            os_ = _decode_attention(page_table, cache_k, cache_v, j, qs, kbuf[1][j], vbuf[1][j], zs,
                                    bm, bm_new, diff_lambda[j], diff_norm_g[j], lam_init)
            w_out = diff_w_out[j]
        else:
            w = gla_w_in
            proj = functools.partial(_matmul, xpb, xsb, w, j, w_transposed=True)
            qp, qs = proj(0, GLA_DK_W, f32, scale=GLA_DK ** -0.5)
            kp, ks = proj(GLA_DK_W, GLA_DK_W, f32)
            vp, vs = proj(2 * GLA_DK_W, D_MODEL, f32)
            zp, zs = proj(2 * GLA_DK_W + D_MODEL, D_MODEL, f32)
            gp = _gla_gate(xpb, gla_w_g1[j], gla_w_g2[j], gla_b_g[j])
            gs = _gla_gate(xsb, gla_w_g1[j], gla_w_g2[j], gla_b_g[j])
            op, sp = _gla(qp, kp, vp, gp, zp, gla_norm_g[j], None, j, batch, seq, seq)
            os_, ss = _gla(qs, ks, vs, gs, zs, gla_norm_g[j], state_gla, j, dec_batch, SAMPLE_ROWS,
                           dec_seq)
            states_p.append(sp)
            states_s.append(ss)
            w_out = gla_w_out[j]
        xp, xpb = _out_proj_norm(op, w_out, xp, ln_g[i], ln_b[i], seq=seq)
        xs, xsb = _out_proj_norm(os_, w_out, xs, ln_g[i], ln_b[i])
    return xp, xs, kbuf, vbuf, states_p, states_s


def kernel(x_prompt, x_sample, cache_k, cache_v, state_gla, page_table, rel_bias, diff_w_in,
           diff_lambda, diff_norm_g, diff_w_out, gla_w_in, gla_w_g2, gla_b_g, gla_norm_g, gla_w_out,
           ln_g, ln_b):
    batch, seq, _ = x_prompt.shape
    dec_batch, dec_seq, _ = x_sample.shape
    n_pages = page_table.shape[1]
    assert n_pages >= 2 and n_pages % DEC_PAGES_PER_STEP == 0 and dec_seq <= SAMPLE_ROWS
    assert cache_k.shape[2:] == (PAGE_SIZE, DIFF_HEADS, DIFF_HW)
    assert seq % (4 * ATT_BLOCK) == 0 and seq % GLA_CHUNK == 0

    gate_off = 2 * GLA_DK_W + 2 * D_MODEL
    gla_w_t = jnp.swapaxes(gla_w_in, 1, 2)
    w_g1 = jnp.pad(gla_w_t[:, gate_off:, :], ((0, 0), (0, LANES - GLA_GATE_RANK), (0, 0)))
    w_g2 = jnp.pad(gla_w_g2, ((0, 0), (0, LANES - GLA_GATE_RANK), (0, 0)))
    weights = (rel_bias, diff_w_in, diff_lambda, diff_norm_g, diff_w_out.astype(bf16),
               gla_w_t, w_g1.astype(bf16), w_g2.astype(bf16), gla_b_g,
               gla_norm_g,
               gla_w_out.astype(bf16), ln_g, ln_b)

    rows = SAMPLE_ROWS
    xs = jnp.pad(x_sample, ((0, 0), (0, rows - dec_seq), (0, 0))).reshape(dec_batch * rows, D_MODEL)
    y_p, y_s, kbuf, vbuf, s_p, s_s = _trunk(
        x_prompt.reshape(batch * seq, D_MODEL), xs, batch, seq, dec_batch, dec_seq,
        cache_k, cache_v, state_gla, page_table, weights)

    def rows_out(a, b, t, t_keep):
        return a.reshape(a.shape[0], b, t, DIFF_HEADS, DIFF_HW)[:, :, :t_keep]

    return (y_p.reshape(batch, seq, D_MODEL),
            y_s.reshape(dec_batch, rows, D_MODEL)[:, :dec_seq],
            rows_out(kbuf[0], batch, seq, seq), rows_out(vbuf[0], batch, seq, seq), jnp.stack(s_p),
            rows_out(kbuf[1], dec_batch, rows, dec_seq), rows_out(vbuf[1], dec_batch, rows, dec_seq),
            jnp.stack(s_s))
```

```python
import functools
import math

import jax
import jax.numpy as jnp
import numpy as np
from jax import lax
from jax.experimental import pallas as pl
from jax.experimental.pallas import tpu as pltpu

D_MODEL = 2048
DEPTH = 4
PAGE_SIZE = 128

DIFF_HEADS = 8
DIFF_DH = D_MODEL // (2 * DIFF_HEADS)
DIFF_HW = 2 * DIFF_DH

GLA_HEADS = 4
GLA_DK_W = D_MODEL // 2
GLA_DK = GLA_DK_W // GLA_HEADS
GLA_DV = D_MODEL // GLA_HEADS
GLA_GATE_RANK = 16
GLA_GATE_NORMALIZER = 16.0
GLA_CHUNK = 64
GLA_CHUNKS_PER_STEP = 2

REL_BUCKETS = 32
REL_MAX_DIST = 128

NORM_EPS = 1e-5
DEEPNORM_ALPHA = (2 * DEPTH) ** 0.25

LANES = 128
SUBLANES = 8
VMEM_LIMIT = 48 * 1024 * 1024
MASK_VALUE = -1e30
LOG2E = math.log2(math.e)

ATT_BLOCK = 256
OUT_ROW_GROUPS = 4
DEC_PAGES_PER_STEP = 8
DEC_GROUPS = 4
DEC_VMEM_LIMIT = 56 * 1024 * 1024
SAMPLE_ROWS = SUBLANES
PAGE_ROWS = PAGE_SIZE * DIFF_HEADS
NEW_ROWS = 2 * SAMPLE_ROWS * DIFF_HEADS
DEC_COLS = 2 * DIFF_HEADS * SAMPLE_ROWS
assert DEC_COLS == LANES and NEW_ROWS == LANES

_NT = (((1,), (1,)), ((), ()))
_TN = (((0,), (0,)), ((), ()))

bf16 = jnp.bfloat16
f32 = jnp.float32


def _params(*sem):
    return pltpu.CompilerParams(dimension_semantics=sem, vmem_limit_bytes=VMEM_LIMIT)


def _silu(z):
    return z * (1.0 / (1.0 + jnp.exp(-z)))


def _mm_kernel(xp_ref, xs_ref, w_ref, *rest, scale, w_transposed):
    op_ref, os_ref, wb_ref = rest[-3:]
    dims = _NT if w_transposed else (((1,), (0,)), ((), ()))

    def project(x_ref, o_ref):
        acc = lax.dot_general(x_ref[...], wb_ref[...], dims, preferred_element_type=f32)
        if scale != 1.0:
            acc = acc * scale
        o_ref[...] = acc.astype(o_ref.dtype)

    @pl.when(pl.program_id(1) == 0)
    def _():
        wb_ref[...] = w_ref[...].astype(bf16)
        project(xs_ref, os_ref)

    project(xp_ref, op_ref)


def _matmul(xp, xs, w, layer, col_off, n, out_dtype, scale=1.0, slots=None, slot=0, into=None,
            w_transposed=False, after=()):
    m, k = xp.shape
    ms = xs.shape[0]
    tm = min(m, 1024)
    tn = min(n, 1024)
    assert m % tm == 0 and n % tn == 0 and col_off % tn == 0
    off = col_off // tn
    if w_transposed:
        w_spec = pl.BlockSpec((None, tn, k), lambda j, i: (layer, j + off, 0))
        w_tile = (tn, k)
    else:
        w_spec = pl.BlockSpec((None, k, tn), lambda j, i: (layer, 0, j + off))
        w_tile = (k, tn)
    in_specs = [pl.BlockSpec((tm, k), lambda j, i: (i, 0)),
                pl.BlockSpec((ms, k), lambda j, i: (0, 0)),
                w_spec]
    args = [xp, xs, w]
    aliases = {}
    if slots is None:
        out_specs = [pl.BlockSpec((tm, tn), lambda j, i: (i, j)),
                     pl.BlockSpec((ms, tn), lambda j, i: (0, j))]
        out_shape = [jax.ShapeDtypeStruct((m, n), out_dtype),
                     jax.ShapeDtypeStruct((ms, n), out_dtype)]
    else:
        out_specs = [pl.BlockSpec((None, tm, tn), lambda j, i: (slot, i, j)),
                     pl.BlockSpec((None, ms, tn), lambda j, i: (slot, 0, j))]
        out_shape = [jax.ShapeDtypeStruct((slots, m, n), out_dtype),
                     jax.ShapeDtypeStruct((slots, ms, n), out_dtype)]
        if into is not None:
            in_specs += [pl.BlockSpec(memory_space=pl.ANY)] * 2
            args += list(into)
            aliases = {3: 0, 4: 1}
    in_specs += [pl.BlockSpec(memory_space=pl.ANY)] * len(after)
    args += list(after)
    return pl.pallas_call(
        functools.partial(_mm_kernel, scale=scale, w_transposed=w_transposed),
        grid=(n // tn, m // tm),
        in_specs=in_specs,
        out_specs=out_specs,
        out_shape=out_shape,
        input_output_aliases=aliases,
        scratch_shapes=[pltpu.VMEM(w_tile, bf16)],
        compiler_params=_params("parallel", "arbitrary"),
        name="proj_matmul",
    )(*args)


def _out_kernel(*refs, tiles_per_batch):
    if tiles_per_batch is None:
        o_ref, w_ref, x_ref, g_ref, b_ref, xo_ref, xb_ref = refs
        o = o_ref[...].astype(bf16)
    else:
        olo_ref, ohi_ref, w_ref, x_ref, g_ref, b_ref, xo_ref, xb_ref = refs
        first_half = (pl.program_id(0) % tiles_per_batch) < tiles_per_batch // 2
        o = jnp.where(first_half, olo_ref[...], ohi_ref[...])
    tm = o.shape[0]
    group = tm // OUT_ROW_GROUPS if tm >= OUT_ROW_GROUPS * LANES else tm
    for r0 in range(0, tm, group):
        rows = slice(r0, r0 + group)
        y = jnp.dot(o[rows], w_ref[...], preferred_element_type=f32)
        r = DEEPNORM_ALPHA * x_ref[rows, :] + y
        mu = jnp.mean(r, axis=-1, keepdims=True)
        d = r - mu
        var = jnp.mean(d * d, axis=-1, keepdims=True)
        xn = d * lax.rsqrt(var + NORM_EPS) * g_ref[...] + b_ref[...]
        xo_ref[rows, :] = xn
        xb_ref[rows, :] = xn.astype(bf16)


def _out_proj_norm(o, w, x, g, b, seq=None):
    m = x.shape[0]
    tm = min(m, 512)
    row = lambda i: (i, 0)
    fixed = lambda i: (0, 0)
    if isinstance(o, (tuple, list)):
        tpb = seq // tm
        half = tpb // 2
        assert seq % tm == 0 and tpb % 2 == 0
        o_specs = [pl.BlockSpec((tm, D_MODEL),
                                lambda i: ((i // tpb) * half + jnp.minimum(i % tpb, half - 1), 0)),
                   pl.BlockSpec((tm, D_MODEL),
                                lambda i: ((i // tpb) * half + jnp.maximum(i % tpb - half, 0), 0))]
        o_args = list(o)
    else:
        tpb = None
        o_specs = [pl.BlockSpec((tm, D_MODEL), row)]
        o_args = [o]
    return pl.pallas_call(
        functools.partial(_out_kernel, tiles_per_batch=tpb),
        grid=(m // tm,),
        in_specs=o_specs + [
                  pl.BlockSpec((D_MODEL, D_MODEL), fixed),
                  pl.BlockSpec((tm, D_MODEL), row),
                  pl.BlockSpec((1, D_MODEL), fixed),
                  pl.BlockSpec((1, D_MODEL), fixed)],
        out_specs=[pl.BlockSpec((tm, D_MODEL), row), pl.BlockSpec((tm, D_MODEL), row)],
        out_shape=[jax.ShapeDtypeStruct((m, D_MODEL), f32),
                   jax.ShapeDtypeStruct((m, D_MODEL), bf16)],
        compiler_params=_params("parallel"),
        name="out_proj_deepnorm",
    )(*o_args, w, x, g.reshape(1, D_MODEL), b.reshape(1, D_MODEL))


def _bias_by_distance(rel_bias, dist):
    n = jnp.asarray(dist, jnp.int32)
    max_exact = REL_BUCKETS // 2
    nf = jnp.maximum(n, 1).astype(f32)
    large = max_exact + (jnp.log(nf / max_exact) / math.log(REL_MAX_DIST / max_exact)
                         * (REL_BUCKETS - max_exact)).astype(jnp.int32)
    large = jnp.minimum(large, REL_BUCKETS - 1)
    bucket = jnp.where(n < max_exact, n, large)
    return jnp.moveaxis(rel_bias.astype(f32)[bucket], -1, 0)


def _prompt_bias_tiles(rel_bias, blk):
    k = np.arange(3 * blk - 1)
    d = 2 * blk - 1 - k
    u = jnp.where(jnp.asarray(d >= 0), _bias_by_distance(rel_bias, np.maximum(d, 0)), MASK_VALUE)
    period = 3 * blk
    flat = jnp.tile(jnp.pad(u, ((0, 0), (0, 1))), (1, blk))[:, :blk * (period - 1)]
    return flat.reshape(DIFF_HEADS, blk, period - 1)[:, :, blk - 1:3 * blk - 1]


def _decode_bias_tables(rel_bias, t_valid):
    col = np.arange(DEC_COLS)
    col_h = (col // SAMPLE_ROWS) % DIFF_HEADS
    reps = DEC_COLS // SAMPLE_ROWS
    tq = np.arange(SAMPLE_ROWS)[None, :]
    new_tokens = NEW_ROWS // DIFF_HEADS
    tab = _bias_by_distance(rel_bias, np.arange(2 * PAGE_SIZE + SAMPLE_ROWS + 1))

    def toeplitz_rows(ext, n_tokens):
        wins = [ext[:, t + 1:t + 1 + n_tokens][:, ::-1] for t in range(SAMPLE_ROWS)]
        return jnp.transpose(jnp.stack(wins), (2, 1, 0)).reshape(n_tokens * DIFF_HEADS, SAMPLE_ROWS)

    def expand(vals, visible):
        n_rows = vals.shape[0]
        r = np.arange(n_rows)
        ok = ((r % DIFF_HEADS)[:, None] == col_h[None, :]) & np.tile(visible, (1, reps))
        return jnp.where(jnp.asarray(ok), jnp.tile(vals, (1, reps)), MASK_VALUE)

    everything = np.ones((PAGE_ROWS, SAMPLE_ROWS), bool)
    far_vals = jnp.broadcast_to(jnp.tile(tab[:, 2 * PAGE_SIZE], PAGE_SIZE)[:, None],
                                (PAGE_ROWS, SAMPLE_ROWS))
    far = expand(far_vals, everything)
    last = expand(toeplitz_rows(tab, PAGE_SIZE), everything)
    s_new = (np.arange(NEW_ROWS) // DIFF_HEADS)[:, None]
    ext = jnp.pad(tab[:, :SAMPLE_ROWS], ((0, 0), (new_tokens, 0)))
    new = expand(toeplitz_rows(ext, new_tokens), (s_new <= tq) & (s_new < t_valid))
    return jnp.stack([far, last]), new


def _lambda_value(lp_ref, lam_init):
    lp = lp_ref[...]
    a = jnp.sum(lp[0:1] * lp[1:2], axis=-1, keepdims=True)
    b = jnp.sum(lp[2:3] * lp[3:4], axis=-1, keepdims=True)
    return jnp.exp(a) - jnp.exp(b) + lam_init


def _head_epilogue(o, g, z, out_scale):
    ms = jnp.mean(o * o, axis=-1, keepdims=True)
    return (o * lax.rsqrt(ms + NORM_EPS) * g * out_scale) * _silu(z)


def _attn_kernel(qlo_ref, qhi_ref, k_ref, v_ref, zlo_ref, zhi_ref, bias_ref, lp_ref, g_ref,
                 olo_ref, ohi_ref, kb_ref, vb_ref, s_ref, *, lam_init, n_blocks):
    i = pl.program_id(2)
    blk = ATT_BLOCK

    @pl.when(i == 0)
    def _():
        kb_ref[...] = k_ref[...].astype(bf16)
        vb_ref[...] = v_ref[...].astype(bf16)

    lam = _lambda_value(lp_ref, lam_init)

    def lane_fold(x):
        return [x[:, c * LANES:(c + 1) * LANES] for c in range(blk // LANES)]

    def logits_and_max(q_ref, nvis, base):
        q = q_ref[...]
        qs = (q[:, :DIFF_DH], q[:, DIFF_DH:])
        mx = [None, None]
        for j in range(nvis):
            kj = kb_ref[j * blk:(j + 1) * blk, :]
            if j == nvis - 1:
                bias = bias_ref[:, blk:]
            elif j == nvis - 2:
                bias = bias_ref[:, :blk]
            else:
                bias = None
            for mp in range(2):
                s = lax.dot_general(qs[mp], kj[:, mp * DIFF_DH:(mp + 1) * DIFF_DH], _NT,
                                    preferred_element_type=f32)
                if bias is not None:
                    s = s + bias
                s_ref[mp, base + j] = s
                for part in lane_fold(s):
                    mx[mp] = part if mx[mp] is None else jnp.maximum(mx[mp], part)
        return [jnp.max(mx[mp], axis=-1, keepdims=True) for mp in range(2)]

    def exponentials(nvis, base, m):
        ls = [None, None]
        for j in range(nvis):
            for mp in range(2):
                p = jnp.exp2(s_ref[mp, base + j] - m[mp])
                s_ref[mp, base + j] = p
                for part in lane_fold(p):
                    ls[mp] = part if ls[mp] is None else ls[mp] + part
        return (1.0 / jnp.sum(ls[0], axis=-1, keepdims=True),
                lam / jnp.sum(ls[1], axis=-1, keepdims=True))

    def weighted_values(nvis, base, c, z_ref, o_ref):
        acc = None
        for j in range(nvis):
            pd = (s_ref[0, base + j] * c[0] - s_ref[1, base + j] * c[1]).astype(bf16)
            pv = jnp.dot(pd, vb_ref[j * blk:(j + 1) * blk, :], preferred_element_type=f32)
            acc = pv if acc is None else acc + pv
        o_ref[...] = _head_epilogue(acc, g_ref[...], z_ref[...], 1.0 - lam_init).astype(bf16)

    def block_pair(lo):
        n_lo, n_hi = lo + 1, n_blocks - lo
        m_lo = logits_and_max(qlo_ref, n_lo, 0)
        m_hi = logits_and_max(qhi_ref, n_hi, n_lo)
        c_lo = exponentials(n_lo, 0, m_lo)
        c_hi = exponentials(n_hi, n_lo, m_hi)
        weighted_values(n_lo, 0, c_lo, zlo_ref, olo_ref)
        weighted_values(n_hi, n_lo, c_hi, zhi_ref, ohi_ref)

    for lo in range(n_blocks // 2):
        pl.when(i == lo)(functools.partial(block_pair, lo))


def _prompt_attention(q, k, v, slot, z, bias_tiles, lam_p, norm_g, batch, seq, lam_init):
    nq = seq // ATT_BLOCK
    half = nq // 2
    blk = ATT_BLOCK
    lo_in = lambda b, h, i: (b * nq + i, h)
    hi_in = lambda b, h, i: (b * nq + nq - 1 - i, h)
    lo_out = lambda b, h, i: (b * half + i, h)
    hi_out = lambda b, h, i: (b * half + half - 1 - i, h)
    kvmap = lambda b, h, i: (slot, b, h)
    out = jax.ShapeDtypeStruct((batch * seq // 2, D_MODEL), bf16)
    return pl.pallas_call(
        functools.partial(_attn_kernel, lam_init=lam_init, n_blocks=nq),
        grid=(batch, DIFF_HEADS, half),
        in_specs=[pl.BlockSpec((blk, DIFF_HW), lo_in),
                  pl.BlockSpec((blk, DIFF_HW), hi_in),
                  pl.BlockSpec((None, seq, DIFF_HW), kvmap),
                  pl.BlockSpec((None, seq, DIFF_HW), kvmap),
                  pl.BlockSpec((blk, DIFF_HW), lo_in),
                  pl.BlockSpec((blk, DIFF_HW), hi_in),
                  pl.BlockSpec((None, blk, 2 * blk), lambda b, h, i: (h, 0, 0)),
                  pl.BlockSpec((4, DIFF_DH), lambda b, h, i: (0, 0)),
                  pl.BlockSpec((1, DIFF_HW), lambda b, h, i: (0, 0))],
        out_specs=[pl.BlockSpec((blk, DIFF_HW), lo_out), pl.BlockSpec((blk, DIFF_HW), hi_out)],
        out_shape=[out, out],
        scratch_shapes=[pltpu.VMEM((seq, DIFF_HW), bf16),
                        pltpu.VMEM((seq, DIFF_HW), bf16),
                        pltpu.VMEM((2, nq + 1, blk, blk), f32)],
        compiler_params=_params("parallel", "parallel", "arbitrary"),
        name="prompt_diff_attention",
    )(q, q, k, v, z, z, bias_tiles, lam_p, norm_g.reshape(1, DIFF_HW))


def _decode_kernel(pt_ref, *refs, lam_init, n_steps):
    del pt_ref
    npg = DEC_PAGES_PER_STEP
    k_pages = refs[:npg]
    v_pages = refs[npg:2 * npg]
    (w_ref, kn_ref, vn_ref, z_ref, bm_ref, bmn_ref, lp_ref, g_ref, o_ref,
     m_ref, l_ref, acc_ref) = refs[2 * npg:]
    s = pl.program_id(1)

    @pl.when(s == 0)
    def _():
        m_ref[...] = jnp.full(m_ref.shape, MASK_VALUE, f32)
        l_ref[...] = jnp.zeros(l_ref.shape, f32)
        acc_ref[...] = jnp.zeros(acc_ref.shape, f32)

    w = w_ref[...]
    eye = (lax.broadcasted_iota(jnp.int32, (DEC_COLS, DEC_COLS), 0)
           == lax.broadcasted_iota(jnp.int32, (DEC_COLS, DEC_COLS), 1))

    def to_column(row):
        return jnp.sum(jnp.where(eye, row, 0.0), axis=1, keepdims=True)

    def update(grp, blocks):
        logits = [jnp.dot(k, w, preferred_element_type=f32) + bm for k, _, bm in blocks]
        cmax = None
        for lg in logits:
            c = jnp.max(lg, axis=0, keepdims=True)
            cmax = c if cmax is None else jnp.maximum(cmax, c)
        m_old = m_ref[grp]
        m_new = jnp.maximum(m_old, cmax)
        a = jnp.exp2(m_old - m_new)
        lsum = None
        pv = None
        for lg, (_, v, _) in zip(logits, blocks):
            p = jnp.exp2(lg - m_new)
            ps = jnp.sum(p, axis=0, keepdims=True)
            lsum = ps if lsum is None else lsum + ps
            d = jnp.dot(p.T.astype(bf16), v.astype(bf16), preferred_element_type=f32)
            pv = d if pv is None else pv + d
        m_ref[grp] = m_new
        l_ref[grp] = a * l_ref[grp] + lsum
        acc_ref[grp] = acc_ref[grp] * to_column(a) + pv

    per_group = npg // DEC_GROUPS
    for grp in range(DEC_GROUPS):
        blocks = []
        for pg in range(grp * per_group, (grp + 1) * per_group):
            if pg == npg - 1:
                bm = bm_ref[jnp.where(s == n_steps - 1, 1, 0)]
            else:
                bm = bm_ref[0]
            blocks.append((k_pages[pg][...], v_pages[pg][...], bm))
        update(grp, blocks)

    @pl.when(s == n_steps - 1)
    def _():
        update(0, [(kn_ref[...], vn_ref[...], bmn_ref[...])])
        m_all = m_ref[0]
        for grp in range(1, DEC_GROUPS):
            m_all = jnp.maximum(m_all, m_ref[grp])
        l_all = None
        acc_all = None
        for grp in range(DEC_GROUPS):
            a = jnp.exp2(m_ref[grp] - m_all)
            lg = a * l_ref[grp]
            ag = acc_ref[grp] * to_column(a)
            l_all = lg if l_all is None else l_all + lg
            acc_all = ag if acc_all is None else acc_all + ag
        on = acc_all * (1.0 / to_column(l_all))
        lam = _lambda_value(lp_ref, lam_init)
        half = DEC_COLS // 2
        for h in range(DIFF_HEADS):
            r0 = h * SAMPLE_ROWS
            o = on[r0:r0 + SAMPLE_ROWS] - lam * on[half + r0:half + r0 + SAMPLE_ROWS]
            lo = h * DIFF_HW
            o_ref[:, lo:lo + DIFF_HW] = _head_epilogue(o, g_ref[...], z_ref[:, lo:lo + DIFF_HW],
                                                       1.0 - lam_init)


def _decode_attention(page_table, cache_k, cache_v, layer, q, k_new, v_new, z, bm, bm_new, lam_p,
                      norm_g, lam_init):
    nb, n_pages = page_table.shape
    npg = DEC_PAGES_PER_STEP
    n_steps = n_pages // npg
    ck = cache_k.reshape(cache_k.shape[0], cache_k.shape[1], PAGE_ROWS, DIFF_HW)
    cv = cache_v.reshape(cache_v.shape[0], cache_v.shape[1], PAGE_ROWS, DIFF_HW)

    q5 = q.reshape(nb, SAMPLE_ROWS, DIFF_HEADS, 2, DIFF_DH)
    qt = jnp.transpose(q5, (0, 3, 4, 2, 1)).reshape(nb, 2, DIFF_DH, DEC_COLS // 2)
    zeros = jnp.zeros_like(qt[:, 0])
    w = jnp.concatenate([jnp.concatenate([qt[:, 0], zeros], axis=-1),
                         jnp.concatenate([zeros, qt[:, 1]], axis=-1)], axis=1)

    def new_rows(a):
        a = a.reshape(nb, SAMPLE_ROWS, DIFF_HEADS, DIFF_HW)
        a = jnp.pad(a, ((0, 0), (0, NEW_ROWS // DIFF_HEADS - SAMPLE_ROWS), (0, 0), (0, 0)))
        return a.reshape(nb, NEW_ROWS, DIFF_HW)

    def page_spec(pg):
        return pl.BlockSpec((None, None, PAGE_ROWS, DIFF_HW),
                            lambda b, s, pt: (layer, pt[b, s * npg + pg], 0, 0))

    per_batch3 = lambda b, s, pt: (b, 0, 0)
    row = lambda b, s, pt: (b, 0)
    fixed2 = lambda b, s, pt: (0, 0)
    grid_spec = pltpu.PrefetchScalarGridSpec(
        num_scalar_prefetch=1,
        grid=(nb, n_steps),
        in_specs=([page_spec(pg) for pg in range(npg)] + [page_spec(pg) for pg in range(npg)]
                  + [pl.BlockSpec((None, DIFF_HW, DEC_COLS), per_batch3),
                     pl.BlockSpec((None, NEW_ROWS, DIFF_HW), per_batch3),
                     pl.BlockSpec((None, NEW_ROWS, DIFF_HW), per_batch3),
                     pl.BlockSpec((SAMPLE_ROWS, D_MODEL), row),
                     pl.BlockSpec((2, PAGE_ROWS, DEC_COLS), lambda b, s, pt: (0, 0, 0)),
                     pl.BlockSpec((NEW_ROWS, DEC_COLS), fixed2),
                     pl.BlockSpec((4, DIFF_DH), fixed2),
                     pl.BlockSpec((1, DIFF_HW), fixed2)]),
        out_specs=pl.BlockSpec((SAMPLE_ROWS, D_MODEL), row),
        scratch_shapes=[pltpu.VMEM((DEC_GROUPS, 1, DEC_COLS), f32),
                        pltpu.VMEM((DEC_GROUPS, 1, DEC_COLS), f32),
                        pltpu.VMEM((DEC_GROUPS, DEC_COLS, DIFF_HW), f32)])
    return pl.pallas_call(
        functools.partial(_decode_kernel, lam_init=lam_init, n_steps=n_steps),
        grid_spec=grid_spec,
        out_shape=jax.ShapeDtypeStruct((nb * SAMPLE_ROWS, D_MODEL), f32),
        compiler_params=pltpu.CompilerParams(dimension_semantics=("parallel", "arbitrary"),
                                             vmem_limit_bytes=DEC_VMEM_LIMIT),
        name="decode_diff_attention",
    )(page_table, *([ck] * npg), *([cv] * npg), w, new_rows(k_new), new_rows(v_new), z, bm, bm_new,
      lam_p, norm_g.reshape(1, DIFF_HW))


def _gate_kernel(x_ref, w1_ref, w2_ref, b_ref, g_ref):
    gl = lax.dot_general(x_ref[...], w1_ref[...], _NT, preferred_element_type=f32)
    u = jnp.dot(gl.astype(bf16), w2_ref[...], preferred_element_type=f32) + b_ref[...]
    ls = jnp.minimum(u, 0.0) - jnp.log(1.0 + jnp.exp(-jnp.abs(u)))
    g_ref[...] = ls * (1.0 / GLA_GATE_NORMALIZER)


def _gla_gate(x, w1, w2, b):
    m = x.shape[0]
    tm = min(m, 512)
    return pl.pallas_call(
        _gate_kernel,
        grid=(m // tm,),
        in_specs=[pl.BlockSpec((tm, D_MODEL), lambda i: (i, 0)),
                  pl.BlockSpec((LANES, D_MODEL), lambda i: (0, 0)),
                  pl.BlockSpec((LANES, GLA_DK_W), lambda i: (0, 0)),
                  pl.BlockSpec((1, GLA_DK_W), lambda i: (0, 0))],
        out_specs=pl.BlockSpec((tm, GLA_DK_W), lambda i: (i, 0)),
        out_shape=jax.ShapeDtypeStruct((m, GLA_DK_W), f32),
        compiler_params=_params("parallel"),
        name="gla_gate",
    )(x, w1, w2, b.reshape(1, GLA_DK_W))


def _gla_kernel(*refs, chunk, chunks_per_step, t_valid, n_steps, has_state):
    if has_state:
        q_ref, k_ref, v_ref, g_ref, z_ref, ng_ref, s0_ref, o_ref, so_ref, st_ref = refs
    else:
        q_ref, k_ref, v_ref, g_ref, z_ref, ng_ref, o_ref, so_ref, st_ref = refs
    n = pl.program_id(1)

    @pl.when(n == 0)
    def _():
        for h in range(GLA_HEADS):
            if has_state:
                st_ref[h] = s0_ref[h]
            else:
                st_ref[h] = jnp.zeros(st_ref.shape[1:], f32)

    row = lax.broadcasted_iota(jnp.int32, (chunk, chunk), 0)
    col = lax.broadcasted_iota(jnp.int32, (chunk, chunk), 1)
    causal = row >= col
    tri = causal.astype(f32).astype(bf16)
    heads = range(GLA_HEADS)
    kc = [slice(h * GLA_DK, (h + 1) * GLA_DK) for h in heads]
    vc = [slice(h * GLA_DV, (h + 1) * GLA_DV) for h in heads]
    intra = []
    for c in range(chunks_per_step):
        rows = slice(c * chunk, (c + 1) * chunk)
        g = g_ref[rows, :]
        k = k_ref[rows, :]
        if t_valid < chunk:
            valid = lax.broadcasted_iota(jnp.int32, (chunk, GLA_DK_W), 0) < t_valid
            g = jnp.where(valid, g, 0.0)
            k = jnp.where(valid, k, 0.0)
        g_hi = g.astype(bf16)
        g_lo = (g - g_hi.astype(f32)).astype(bf16)
        b = (jnp.dot(tri, g_hi, preferred_element_type=f32)
             + jnp.dot(tri, g_lo, preferred_element_type=f32))
        b_last = b[chunk - 1:chunk, :]
        qe = (q_ref[rows, :] * jnp.exp(b)).astype(bf16)
        ke = (k * jnp.exp(-b)).astype(bf16)
        kd = (k * jnp.exp(b_last - b)).astype(bf16)
        vb = v_ref[rows, :].astype(bf16)
        decay = jnp.broadcast_to(jnp.exp(b_last), (SUBLANES, GLA_DK_W)).T[:, :1]
        a = [lax.dot_general(qe[:, kc[h]], ke[:, kc[h]], _NT, preferred_element_type=f32)
             for h in heads]
        a = [jnp.where(causal, a[h], 0.0).astype(bf16) for h in heads]
        o_intra = [jnp.dot(a[h], vb[:, vc[h]], preferred_element_type=f32) for h in heads]
        intra.append((rows, qe, kd, vb, decay, o_intra))
    st = [st_ref[h] for h in heads]
    for rows, qe, kd, vb, decay, o_intra in intra:
        o = [o_intra[h] + jnp.dot(qe[:, kc[h]], st[h].astype(bf16), preferred_element_type=f32)
             for h in heads]
        st = [decay[kc[h]] * st[h] + lax.dot_general(kd[:, kc[h]], vb[:, vc[h]], _TN,
                                                     preferred_element_type=f32) for h in heads]
        for h in heads:
            o_ref[rows, vc[h]] = _head_epilogue(o[h], ng_ref[...], z_ref[rows, vc[h]],
                                                1.0).astype(o_ref.dtype)
    for h in heads:
        st_ref[h] = st[h]

    @pl.when(n == n_steps - 1)
    def _():
        for h in range(GLA_HEADS):
            so_ref[h] = st_ref[h]


def _gla(q, k, v, g, z, norm_g, s0, layer, batch, seq, t_valid):
    chunk = min(GLA_CHUNK, seq)
    per_step = GLA_CHUNKS_PER_STEP if seq % (GLA_CHUNKS_PER_STEP * chunk) == 0 else 1
    rows = per_step * chunk
    nc = seq // rows
    has_state = s0 is not None
    out_dtype = bf16 if chunk % (2 * SUBLANES) == 0 else f32
    tmap = lambda b, n: (b * nc + n, 0)
    smap = lambda b, n: (b, 0, 0, 0)
    state_spec = pl.BlockSpec((None, GLA_HEADS, GLA_DK, GLA_DV), smap)
    in_specs = [pl.BlockSpec((rows, GLA_DK_W), tmap),
                pl.BlockSpec((rows, GLA_DK_W), tmap),
                pl.BlockSpec((rows, D_MODEL), tmap),
                pl.BlockSpec((rows, GLA_DK_W), tmap),
                pl.BlockSpec((rows, D_MODEL), tmap),
                pl.BlockSpec((1, GLA_DV), lambda b, n: (0, 0))]
    args = [q, k, v, g, z, norm_g.reshape(1, GLA_DV)]
    if has_state:
        in_specs.append(pl.BlockSpec((None, None, GLA_HEADS, GLA_DK, GLA_DV),
                                     lambda b, n: (layer, b, 0, 0, 0)))
        args.append(s0)
    return pl.pallas_call(
        functools.partial(_gla_kernel, chunk=chunk, chunks_per_step=per_step, t_valid=t_valid,
                          n_steps=nc, has_state=has_state),
        grid=(batch, nc),
        in_specs=in_specs,
        out_specs=[pl.BlockSpec((rows, D_MODEL), tmap), state_spec],
        out_shape=[jax.ShapeDtypeStruct((batch * seq, D_MODEL), out_dtype),
                   jax.ShapeDtypeStruct((batch, GLA_HEADS, GLA_DK, GLA_DV), f32)],
        scratch_shapes=[pltpu.VMEM((GLA_HEADS, GLA_DK, GLA_DV), f32)],
        compiler_params=_params("parallel", "arbitrary"),
        name="gla_chunked",
    )(*args)


def _trunk(xp, xs, batch, seq, dec_batch, dec_seq, cache_k, cache_v, state_gla, page_table, weights):
    (rel_bias, diff_w_in, diff_lambda, diff_norm_g, diff_w_out, gla_w_in, gla_w_g1, gla_w_g2,
     gla_b_g, gla_norm_g, gla_w_out, ln_g, ln_b) = weights
    n_diff = (DEPTH + 1) // 2
    xpb = xp.astype(bf16)
    xsb = xs.astype(bf16)
    kbuf = vbuf = None
    states_p, states_s = [], []
    far = _bias_by_distance(rel_bias, np.array([2 * ATT_BLOCK]))[:, 0]
    tiles = (_prompt_bias_tiles(rel_bias, ATT_BLOCK) - far[:, None, None]) * LOG2E
    bm, bm_new = [t * LOG2E for t in _decode_bias_tables(rel_bias, dec_seq)]
    q_scale = DIFF_DH ** -0.5 * LOG2E
    for i in range(DEPTH):
        j = i // 2
        if i % 2 == 0:
            w = diff_w_in
            qp, qs = _matmul(xpb, xsb, w, j, 0, D_MODEL, bf16, scale=q_scale)
            zp, zs = _matmul(xpb, xsb, w, j, 3 * D_MODEL, D_MODEL, f32, after=(qp,))
            kbuf = _matmul(xpb, xsb, w, j, D_MODEL, D_MODEL, f32, slots=n_diff, slot=j, into=kbuf,
                           after=(zp,))
            vbuf = _matmul(xpb, xsb, w, j, 2 * D_MODEL, D_MODEL, f32, slots=n_diff, slot=j,
                           into=vbuf, after=(kbuf[0],))
            lam_init = 0.8 - 0.6 * math.exp(-0.3 * i)
            op = _prompt_attention(qp, kbuf[0], vbuf[0], j, zp, tiles, diff_lambda[j],
                                   diff_norm_g[j], batch, seq, lam_init)
            os_ = _decode_attention(page_table, cache_k, cache_v, j, qs, kbuf[1][j], vbuf[1][j], zs,
                                    bm, bm_new, diff_lambda[j], diff_norm_g[j], lam_init)
            w_out = diff_w_out[j]
        else:
            w = gla_w_in
            proj = functools.partial(_matmul, xpb, xsb, w, j, w_transposed=True)
            qp, qs = proj(0, GLA_DK_W, f32, scale=GLA_DK ** -0.5)
            kp, ks = proj(GLA_DK_W, GLA_DK_W, f32)
            vp, vs = proj(2 * GLA_DK_W, D_MODEL, f32)
            zp, zs = proj(2 * GLA_DK_W + D_MODEL, D_MODEL, f32)
            gp = _gla_gate(xpb, gla_w_g1[j], gla_w_g2[j], gla_b_g[j])
            gs = _gla_gate(xsb, gla_w_g1[j], gla_w_g2[j], gla_b_g[j])
            op, sp = _gla(qp, kp, vp, gp, zp, gla_norm_g[j], None, j, batch, seq, seq)
            os_, ss = _gla(qs, ks, vs, gs, zs, gla_norm_g[j], state_gla, j, dec_batch, SAMPLE_ROWS,
                           dec_seq)
            states_p.append(sp)
            states_s.append(ss)
            w_out = gla_w_out[j]
        xp, xpb = _out_proj_norm(op, w_out, xp, ln_g[i], ln_b[i], seq=seq)
        xs, xsb = _out_proj_norm(os_, w_out, xs, ln_g[i], ln_b[i])
    return xp, xs, kbuf, vbuf, states_p, states_s


def kernel(x_prompt, x_sample, cache_k, cache_v, state_gla, page_table, rel_bias, diff_w_in,
           diff_lambda, diff_norm_g, diff_w_out, gla_w_in, gla_w_g2, gla_b_g, gla_norm_g, gla_w_out,
           ln_g, ln_b):
    batch, seq, _ = x_prompt.shape
    dec_batch, dec_seq, _ = x_sample.shape
    n_pages = page_table.shape[1]
    assert n_pages >= 2 and n_pages % DEC_PAGES_PER_STEP == 0 and dec_seq <= SAMPLE_ROWS
    assert cache_k.shape[2:] == (PAGE_SIZE, DIFF_HEADS, DIFF_HW)
    assert seq % (4 * ATT_BLOCK) == 0 and seq % GLA_CHUNK == 0

    gate_off = 2 * GLA_DK_W + 2 * D_MODEL
    gla_w_t = jnp.swapaxes(gla_w_in, 1, 2)
    w_g1 = jnp.pad(gla_w_t[:, gate_off:, :], ((0, 0), (0, LANES - GLA_GATE_RANK), (0, 0)))
    w_g2 = jnp.pad(gla_w_g2, ((0, 0), (0, LANES - GLA_GATE_RANK), (0, 0)))
    weights = (rel_bias, diff_w_in, diff_lambda, diff_norm_g, diff_w_out.astype(bf16),
               gla_w_t, w_g1.astype(bf16), w_g2.astype(bf16), gla_b_g,
               gla_norm_g,
               gla_w_out.astype(bf16), ln_g, ln_b)

    rows = SAMPLE_ROWS
    xs = jnp.pad(x_sample, ((0, 0), (0, rows - dec_seq), (0, 0))).reshape(dec_batch * rows, D_MODEL)
    y_p, y_s, kbuf, vbuf, s_p, s_s = _trunk(
        x_prompt.reshape(batch * seq, D_MODEL), xs, batch, seq, dec_batch, dec_seq,
        cache_k, cache_v, state_gla, page_table, weights)

    def rows_out(a, b, t, t_keep):
        return a.reshape(a.shape[0], b, t, DIFF_HEADS, DIFF_HW)[:, :, :t_keep]

    return (y_p.reshape(batch, seq, D_MODEL),
            y_s.reshape(dec_batch, rows, D_MODEL)[:, :dec_seq],
            rows_out(kbuf[0], batch, seq, seq), rows_out(vbuf[0], batch, seq, seq), jnp.stack(s_p),
            rows_out(kbuf[1], dec_batch, rows, dec_seq), rows_out(vbuf[1], dec_batch, rows, dec_seq),
            jnp.stack(s_s))
```

```python
import functools
import math

import jax
import jax.numpy as jnp
import numpy as np
from jax import lax
from jax.experimental import pallas as pl
from jax.experimental.pallas import tpu as pltpu

D_MODEL = 2048
DEPTH = 4
PAGE_SIZE = 128

DIFF_HEADS = 8
DIFF_DH = D_MODEL // (2 * DIFF_HEADS)
DIFF_HW = 2 * DIFF_DH

GLA_HEADS = 4
GLA_DK_W = D_MODEL // 2
GLA_DK = GLA_DK_W // GLA_HEADS
GLA_DV = D_MODEL // GLA_HEADS
GLA_GATE_RANK = 16
GLA_GATE_NORMALIZER = 16.0
GLA_CHUNK = 64
GLA_CHUNKS_PER_STEP = 2

REL_BUCKETS = 32
REL_MAX_DIST = 128

NORM_EPS = 1e-5
DEEPNORM_ALPHA = (2 * DEPTH) ** 0.25

LANES = 128
SUBLANES = 8
VMEM_LIMIT = 48 * 1024 * 1024
MASK_VALUE = -1e30
LOG2E = math.log2(math.e)

ATT_BLOCK = 256
OUT_ROW_GROUPS = 4
DEC_PAGES_PER_STEP = 8
DEC_GROUPS = 4
DEC_VMEM_LIMIT = 56 * 1024 * 1024
FUSED_VMEM_LIMIT = 60 * 1024 * 1024
SAMPLE_ROWS = SUBLANES
PAGE_ROWS = PAGE_SIZE * DIFF_HEADS
NEW_ROWS = 2 * SAMPLE_ROWS * DIFF_HEADS
DEC_COLS = 2 * DIFF_HEADS * SAMPLE_ROWS
assert DEC_COLS == LANES and NEW_ROWS == LANES

_NT = (((1,), (1,)), ((), ()))
_TN = (((0,), (0,)), ((), ()))

bf16 = jnp.bfloat16
f32 = jnp.float32


def _params(*sem):
    return pltpu.CompilerParams(dimension_semantics=sem, vmem_limit_bytes=VMEM_LIMIT)


def _silu(z):
    return z * (1.0 / (1.0 + jnp.exp(-z)))


def _mm_kernel(xp_ref, xs_ref, w_ref, *rest, scale, w_transposed):
    op_ref, os_ref, wb_ref = rest[-3:]
    dims = _NT if w_transposed else (((1,), (0,)), ((), ()))

    def project(x_ref, o_ref):
        acc = lax.dot_general(x_ref[...], wb_ref[...], dims, preferred_element_type=f32)
        if scale != 1.0:
            acc = acc * scale
        o_ref[...] = acc.astype(o_ref.dtype)

    @pl.when(pl.program_id(1) == 0)
    def _():
        wb_ref[...] = w_ref[...].astype(bf16)
        project(xs_ref, os_ref)

    project(xp_ref, op_ref)


def _matmul(xp, xs, w, layer, col_off, n, out_dtype, scale=1.0, slots=None, slot=0, into=None,
            w_transposed=False, after=()):
    m, k = xp.shape
    ms = xs.shape[0]
    tm = min(m, 1024)
    tn = min(n, 1024)
    assert m % tm == 0 and n % tn == 0 and col_off % tn == 0
    off = col_off // tn
    if w_transposed:
        w_spec = pl.BlockSpec((None, tn, k), lambda j, i: (layer, j + off, 0))
        w_tile = (tn, k)
    else:
        w_spec = pl.BlockSpec((None, k, tn), lambda j, i: (layer, 0, j + off))
        w_tile = (k, tn)
    in_specs = [pl.BlockSpec((tm, k), lambda j, i: (i, 0)),
                pl.BlockSpec((ms, k), lambda j, i: (0, 0)),
                w_spec]
    args = [xp, xs, w]
    aliases = {}
    if slots is None:
        out_specs = [pl.BlockSpec((tm, tn), lambda j, i: (i, j)),
                     pl.BlockSpec((ms, tn), lambda j, i: (0, j))]
        out_shape = [jax.ShapeDtypeStruct((m, n), out_dtype),
                     jax.ShapeDtypeStruct((ms, n), out_dtype)]
    else:
        out_specs = [pl.BlockSpec((None, tm, tn), lambda j, i: (slot, i, j)),
                     pl.BlockSpec((None, ms, tn), lambda j, i: (slot, 0, j))]
        out_shape = [jax.ShapeDtypeStruct((slots, m, n), out_dtype),
                     jax.ShapeDtypeStruct((slots, ms, n), out_dtype)]
        if into is not None:
            in_specs += [pl.BlockSpec(memory_space=pl.ANY)] * 2
            args += list(into)
            aliases = {3: 0, 4: 1}
    in_specs += [pl.BlockSpec(memory_space=pl.ANY)] * len(after)
    args += list(after)
    return pl.pallas_call(
        functools.partial(_mm_kernel, scale=scale, w_transposed=w_transposed),
        grid=(n // tn, m // tm),
        in_specs=in_specs,
        out_specs=out_specs,
        out_shape=out_shape,
        input_output_aliases=aliases,
        scratch_shapes=[pltpu.VMEM(w_tile, bf16)],
        compiler_params=_params("parallel", "arbitrary"),
        name="proj_matmul",
    )(*args)


def _out_kernel(*refs, tiles_per_batch):
    if tiles_per_batch is None:
        o_ref, w_ref, x_ref, g_ref, b_ref, xo_ref, xb_ref = refs
        o = o_ref[...].astype(bf16)
    else:
        olo_ref, ohi_ref, w_ref, x_ref, g_ref, b_ref, xo_ref, xb_ref = refs
        first_half = (pl.program_id(0) % tiles_per_batch) < tiles_per_batch // 2
        o = jnp.where(first_half, olo_ref[...], ohi_ref[...])
    tm = o.shape[0]
    group = tm // OUT_ROW_GROUPS if tm >= OUT_ROW_GROUPS * LANES else tm
    for r0 in range(0, tm, group):
        rows = slice(r0, r0 + group)
        y = jnp.dot(o[rows], w_ref[...], preferred_element_type=f32)
        r = DEEPNORM_ALPHA * x_ref[rows, :] + y
        mu = jnp.mean(r, axis=-1, keepdims=True)
        d = r - mu
        var = jnp.mean(d * d, axis=-1, keepdims=True)
        xn = d * lax.rsqrt(var + NORM_EPS) * g_ref[...] + b_ref[...]
        xo_ref[rows, :] = xn
        xb_ref[rows, :] = xn.astype(bf16)


def _out_proj_norm(o, w, x, g, b, seq=None):
    m = x.shape[0]
    tm = min(m, 512)
    row = lambda i: (i, 0)
    fixed = lambda i: (0, 0)
    if isinstance(o, (tuple, list)):
        tpb = seq // tm
        half = tpb // 2
        assert seq % tm == 0 and tpb % 2 == 0
        o_specs = [pl.BlockSpec((tm, D_MODEL),
                                lambda i: ((i // tpb) * half + jnp.minimum(i % tpb, half - 1), 0)),
                   pl.BlockSpec((tm, D_MODEL),
                                lambda i: ((i // tpb) * half + jnp.maximum(i % tpb - half, 0), 0))]
        o_args = list(o)
    else:
        tpb = None
        o_specs = [pl.BlockSpec((tm, D_MODEL), row)]
        o_args = [o]
    return pl.pallas_call(
        functools.partial(_out_kernel, tiles_per_batch=tpb),
        grid=(m // tm,),
        in_specs=o_specs + [
                  pl.BlockSpec((D_MODEL, D_MODEL), fixed),
                  pl.BlockSpec((tm, D_MODEL), row),
                  pl.BlockSpec((1, D_MODEL), fixed),
                  pl.BlockSpec((1, D_MODEL), fixed)],
        out_specs=[pl.BlockSpec((tm, D_MODEL), row), pl.BlockSpec((tm, D_MODEL), row)],
        out_shape=[jax.ShapeDtypeStruct((m, D_MODEL), f32),
                   jax.ShapeDtypeStruct((m, D_MODEL), bf16)],
        compiler_params=_params("parallel"),
        name="out_proj_deepnorm",
    )(*o_args, w, x, g.reshape(1, D_MODEL), b.reshape(1, D_MODEL))


def _bias_by_distance(rel_bias, dist):
    n = jnp.asarray(dist, jnp.int32)
    max_exact = REL_BUCKETS // 2
    nf = jnp.maximum(n, 1).astype(f32)
    large = max_exact + (jnp.log(nf / max_exact) / math.log(REL_MAX_DIST / max_exact)
                         * (REL_BUCKETS - max_exact)).astype(jnp.int32)
    large = jnp.minimum(large, REL_BUCKETS - 1)
    bucket = jnp.where(n < max_exact, n, large)
    return jnp.moveaxis(rel_bias.astype(f32)[bucket], -1, 0)


def _prompt_bias_tiles(rel_bias, blk):
    k = np.arange(3 * blk - 1)
    d = 2 * blk - 1 - k
    u = jnp.where(jnp.asarray(d >= 0), _bias_by_distance(rel_bias, np.maximum(d, 0)), MASK_VALUE)
    period = 3 * blk
    flat = jnp.tile(jnp.pad(u, ((0, 0), (0, 1))), (1, blk))[:, :blk * (period - 1)]
    return flat.reshape(DIFF_HEADS, blk, period - 1)[:, :, blk - 1:3 * blk - 1]


def _decode_bias_tables(rel_bias, t_valid):
    col = np.arange(DEC_COLS)
    col_h = (col // SAMPLE_ROWS) % DIFF_HEADS
    reps = DEC_COLS // SAMPLE_ROWS
    tq = np.arange(SAMPLE_ROWS)[None, :]
    new_tokens = NEW_ROWS // DIFF_HEADS
    tab = _bias_by_distance(rel_bias, np.arange(2 * PAGE_SIZE + SAMPLE_ROWS + 1))

    def toeplitz_rows(ext, n_tokens):
        wins = [ext[:, t + 1:t + 1 + n_tokens][:, ::-1] for t in range(SAMPLE_ROWS)]
        return jnp.transpose(jnp.stack(wins), (2, 1, 0)).reshape(n_tokens * DIFF_HEADS, SAMPLE_ROWS)

    def expand(vals, visible):
        n_rows = vals.shape[0]
        r = np.arange(n_rows)
        ok = ((r % DIFF_HEADS)[:, None] == col_h[None, :]) & np.tile(visible, (1, reps))
        return jnp.where(jnp.asarray(ok), jnp.tile(vals, (1, reps)), MASK_VALUE)

    everything = np.ones((PAGE_ROWS, SAMPLE_ROWS), bool)
    far_vals = jnp.broadcast_to(jnp.tile(tab[:, 2 * PAGE_SIZE], PAGE_SIZE)[:, None],
                                (PAGE_ROWS, SAMPLE_ROWS))
    far = expand(far_vals, everything)
    last = expand(toeplitz_rows(tab, PAGE_SIZE), everything)
    s_new = (np.arange(NEW_ROWS) // DIFF_HEADS)[:, None]
    ext = jnp.pad(tab[:, :SAMPLE_ROWS], ((0, 0), (new_tokens, 0)))
    new = expand(toeplitz_rows(ext, new_tokens), (s_new <= tq) & (s_new < t_valid))
    return jnp.stack([far, last]), new


def _lambda_value(lp_ref, lam_init):
    lp = lp_ref[...]
    a = jnp.sum(lp[0:1] * lp[1:2], axis=-1, keepdims=True)
    b = jnp.sum(lp[2:3] * lp[3:4], axis=-1, keepdims=True)
    return jnp.exp(a) - jnp.exp(b) + lam_init


def _head_epilogue(o, g, z, out_scale):
    ms = jnp.mean(o * o, axis=-1, keepdims=True)
    return (o * lax.rsqrt(ms + NORM_EPS) * g * out_scale) * _silu(z)


def _attn_step(i, qlo_ref, qhi_ref, k_ref, v_ref, zlo_ref, zhi_ref, bias_ref, lp_ref, g_ref,
               olo_ref, ohi_ref, kb_ref, vb_ref, s_ref, *, lam_init, n_blocks):
    blk = ATT_BLOCK

    @pl.when(i == 0)
    def _():
        kb_ref[...] = k_ref[...].astype(bf16)
        vb_ref[...] = v_ref[...].astype(bf16)

    lam = _lambda_value(lp_ref, lam_init)

    def lane_fold(x):
        return [x[:, c * LANES:(c + 1) * LANES] for c in range(blk // LANES)]

    def logits_and_max(q_ref, nvis, base):
        q = q_ref[...]
        qs = (q[:, :DIFF_DH], q[:, DIFF_DH:])
        mx = [None, None]
        for j in range(nvis):
            kj = kb_ref[j * blk:(j + 1) * blk, :]
            if j == nvis - 1:
                bias = bias_ref[:, blk:]
            elif j == nvis - 2:
                bias = bias_ref[:, :blk]
            else:
                bias = None
            for mp in range(2):
                s = lax.dot_general(qs[mp], kj[:, mp * DIFF_DH:(mp + 1) * DIFF_DH], _NT,
                                    preferred_element_type=f32)
                if bias is not None:
                    s = s + bias
                s_ref[mp, base + j] = s
                for part in lane_fold(s):
                    mx[mp] = part if mx[mp] is None else jnp.maximum(mx[mp], part)
        return [jnp.max(mx[mp], axis=-1, keepdims=True) for mp in range(2)]

    def exponentials(nvis, base, m):
        ls = [None, None]
        for j in range(nvis):
            for mp in range(2):
                p = jnp.exp2(s_ref[mp, base + j] - m[mp])
                s_ref[mp, base + j] = p
                for part in lane_fold(p):
                    ls[mp] = part if ls[mp] is None else ls[mp] + part
        return (1.0 / jnp.sum(ls[0], axis=-1, keepdims=True),
                lam / jnp.sum(ls[1], axis=-1, keepdims=True))

    def weighted_values(nvis, base, c, z_ref, o_ref):
        acc = None
        for j in range(nvis):
            pd = (s_ref[0, base + j] * c[0] - s_ref[1, base + j] * c[1]).astype(bf16)
            pv = jnp.dot(pd, vb_ref[j * blk:(j + 1) * blk, :], preferred_element_type=f32)
            acc = pv if acc is None else acc + pv
        o_ref[...] = _head_epilogue(acc, g_ref[...], z_ref[...], 1.0 - lam_init).astype(bf16)

    def block_pair(lo):
        n_lo, n_hi = lo + 1, n_blocks - lo
        m_lo = logits_and_max(qlo_ref, n_lo, 0)
        m_hi = logits_and_max(qhi_ref, n_hi, n_lo)
        c_lo = exponentials(n_lo, 0, m_lo)
        c_hi = exponentials(n_hi, n_lo, m_hi)
        weighted_values(n_lo, 0, c_lo, zlo_ref, olo_ref)
        weighted_values(n_hi, n_lo, c_hi, zhi_ref, ohi_ref)

    for lo in range(n_blocks // 2):
        pl.when(i == lo)(functools.partial(block_pair, lo))


def _attn_kernel(*refs, **static):
    _attn_step(pl.program_id(2), *refs, **static)


def _prompt_specs(q, k, v, slot, z, bias_tiles, lam_p, norm_g, batch, seq, coords):
    nq = seq // ATT_BLOCK
    half = nq // 2
    blk = ATT_BLOCK

    def at(f):
        return lambda *g: f(*coords(*g))

    lo_in = at(lambda b, h, i: (b * nq + i, h))
    hi_in = at(lambda b, h, i: (b * nq + nq - 1 - i, h))
    lo_out = at(lambda b, h, i: (b * half + i, h))
    hi_out = at(lambda b, h, i: (b * half + half - 1 - i, h))
    kvmap = at(lambda b, h, i: (slot, b, h))
    out = jax.ShapeDtypeStruct((batch * seq // 2, D_MODEL), bf16)
    args = [q, q, k, v, z, z, bias_tiles, lam_p, norm_g.reshape(1, DIFF_HW)]
    in_specs = [pl.BlockSpec((blk, DIFF_HW), lo_in),
                pl.BlockSpec((blk, DIFF_HW), hi_in),
                pl.BlockSpec((None, seq, DIFF_HW), kvmap),
                pl.BlockSpec((None, seq, DIFF_HW), kvmap),
                pl.BlockSpec((blk, DIFF_HW), lo_in),
                pl.BlockSpec((blk, DIFF_HW), hi_in),
                pl.BlockSpec((None, blk, 2 * blk), at(lambda b, h, i: (h, 0, 0))),
                pl.BlockSpec((4, DIFF_DH), at(lambda b, h, i: (0, 0))),
                pl.BlockSpec((1, DIFF_HW), at(lambda b, h, i: (0, 0)))]
    out_specs = [pl.BlockSpec((blk, DIFF_HW), lo_out), pl.BlockSpec((blk, DIFF_HW), hi_out)]
    scratch = [pltpu.VMEM((seq, DIFF_HW), bf16),
               pltpu.VMEM((seq, DIFF_HW), bf16),
               pltpu.VMEM((2, nq + 1, blk, blk), f32)]
    return args, in_specs, out_specs, [out, out], scratch


def _prompt_attention(q, k, v, slot, z, bias_tiles, lam_p, norm_g, batch, seq, lam_init):
    nq = seq // ATT_BLOCK
    args, in_specs, out_specs, out_shape, scratch = _prompt_specs(
        q, k, v, slot, z, bias_tiles, lam_p, norm_g, batch, seq, lambda b, h, i: (b, h, i))
    return pl.pallas_call(
        functools.partial(_attn_kernel, lam_init=lam_init, n_blocks=nq),
        grid=(batch, DIFF_HEADS, nq // 2),
        in_specs=in_specs,
        out_specs=out_specs,
        out_shape=out_shape,
        scratch_shapes=scratch,
        compiler_params=_params("parallel", "parallel", "arbitrary"),
        name="prompt_diff_attention",
    )(*args)


DEC_INPUTS = 2 * DEC_PAGES_PER_STEP + 8


def _decode_kernel(pt_ref, *refs, **static):
    del pt_ref
    _decode_step(pl.program_id(1), refs, **static)


def _decode_step(s, refs, *, lam_init, n_steps):
    npg = DEC_PAGES_PER_STEP
    k_pages = refs[:npg]
    v_pages = refs[npg:2 * npg]
    (w_ref, kn_ref, vn_ref, z_ref, bm_ref, bmn_ref, lp_ref, g_ref, o_ref,
     m_ref, l_ref, acc_ref) = refs[2 * npg:]

    @pl.when(s == 0)
    def _():
        m_ref[...] = jnp.full(m_ref.shape, MASK_VALUE, f32)
        l_ref[...] = jnp.zeros(l_ref.shape, f32)
        acc_ref[...] = jnp.zeros(acc_ref.shape, f32)

    w = w_ref[...]
    eye = (lax.broadcasted_iota(jnp.int32, (DEC_COLS, DEC_COLS), 0)
           == lax.broadcasted_iota(jnp.int32, (DEC_COLS, DEC_COLS), 1))

    def to_column(row):
        return jnp.sum(jnp.where(eye, row, 0.0), axis=1, keepdims=True)

    def update(grp, blocks):
        logits = [jnp.dot(k, w, preferred_element_type=f32) + bm for k, _, bm in blocks]
        cmax = None
        for lg in logits:
            c = jnp.max(lg, axis=0, keepdims=True)
            cmax = c if cmax is None else jnp.maximum(cmax, c)
        m_old = m_ref[grp]
        m_new = jnp.maximum(m_old, cmax)
        a = jnp.exp2(m_old - m_new)
        lsum = None
        pv = None
        for lg, (_, v, _) in zip(logits, blocks):
            p = jnp.exp2(lg - m_new)
            ps = jnp.sum(p, axis=0, keepdims=True)
            lsum = ps if lsum is None else lsum + ps
            d = jnp.dot(p.T.astype(bf16), v.astype(bf16), preferred_element_type=f32)
            pv = d if pv is None else pv + d
        m_ref[grp] = m_new
        l_ref[grp] = a * l_ref[grp] + lsum
        acc_ref[grp] = acc_ref[grp] * to_column(a) + pv

    per_group = npg // DEC_GROUPS
    for grp in range(DEC_GROUPS):
        blocks = []
        for pg in range(grp * per_group, (grp + 1) * per_group):
            if pg == npg - 1:
                bm = bm_ref[jnp.where(s == n_steps - 1, 1, 0)]
            else:
                bm = bm_ref[0]
            blocks.append((k_pages[pg][...], v_pages[pg][...], bm))
        update(grp, blocks)

    @pl.when(s == n_steps - 1)
    def _():
        update(0, [(kn_ref[...], vn_ref[...], bmn_ref[...])])
        m_all = m_ref[0]
        for grp in range(1, DEC_GROUPS):
            m_all = jnp.maximum(m_all, m_ref[grp])
        l_all = None
        acc_all = None
        for grp in range(DEC_GROUPS):
            a = jnp.exp2(m_ref[grp] - m_all)
            lg = a * l_ref[grp]
            ag = acc_ref[grp] * to_column(a)
            l_all = lg if l_all is None else l_all + lg
            acc_all = ag if acc_all is None else acc_all + ag
        on = acc_all * (1.0 / to_column(l_all))
        lam = _lambda_value(lp_ref, lam_init)
        half = DEC_COLS // 2
        for h in range(DIFF_HEADS):
            r0 = h * SAMPLE_ROWS
            o = on[r0:r0 + SAMPLE_ROWS] - lam * on[half + r0:half + r0 + SAMPLE_ROWS]
            lo = h * DIFF_HW
            o_ref[:, lo:lo + DIFF_HW] = _head_epilogue(o, g_ref[...], z_ref[:, lo:lo + DIFF_HW],
                                                       1.0 - lam_init)


def _decode_specs(page_table, cache_k, cache_v, layer, q, k_new, v_new, z, bm, bm_new, lam_p, norm_g):
    nb, n_pages = page_table.shape
    npg = DEC_PAGES_PER_STEP
    ck = cache_k.reshape(cache_k.shape[0], cache_k.shape[1], PAGE_ROWS, DIFF_HW)
    cv = cache_v.reshape(cache_v.shape[0], cache_v.shape[1], PAGE_ROWS, DIFF_HW)

    q5 = q.reshape(nb, SAMPLE_ROWS, DIFF_HEADS, 2, DIFF_DH)
    qt = jnp.transpose(q5, (0, 3, 4, 2, 1)).reshape(nb, 2, DIFF_DH, DEC_COLS // 2)
    zeros = jnp.zeros_like(qt[:, 0])
    w = jnp.concatenate([jnp.concatenate([qt[:, 0], zeros], axis=-1),
                         jnp.concatenate([zeros, qt[:, 1]], axis=-1)], axis=1)

    def new_rows(a):
        a = a.reshape(nb, SAMPLE_ROWS, DIFF_HEADS, DIFF_HW)
        a = jnp.pad(a, ((0, 0), (0, NEW_ROWS // DIFF_HEADS - SAMPLE_ROWS), (0, 0), (0, 0)))
        return a.reshape(nb, NEW_ROWS, DIFF_HW)

    def page_spec(pg):
        return pl.BlockSpec((None, None, PAGE_ROWS, DIFF_HW),
                            lambda b, s, pt: (layer, pt[b, s * npg + pg], 0, 0))

    per_batch3 = lambda b, s, pt: (b, 0, 0)
    row = lambda b, s, pt: (b, 0)
    fixed2 = lambda b, s, pt: (0, 0)
    in_specs = ([page_spec(pg) for pg in range(npg)] + [page_spec(pg) for pg in range(npg)]
                + [pl.BlockSpec((None, DIFF_HW, DEC_COLS), per_batch3),
                   pl.BlockSpec((None, NEW_ROWS, DIFF_HW), per_batch3),
                   pl.BlockSpec((None, NEW_ROWS, DIFF_HW), per_batch3),
                   pl.BlockSpec((SAMPLE_ROWS, D_MODEL), row),
                   pl.BlockSpec((2, PAGE_ROWS, DEC_COLS), lambda b, s, pt: (0, 0, 0)),
                   pl.BlockSpec((NEW_ROWS, DEC_COLS), fixed2),
                   pl.BlockSpec((4, DIFF_DH), fixed2),
                   pl.BlockSpec((1, DIFF_HW), fixed2)])
    assert len(in_specs) == DEC_INPUTS
    args = [*([ck] * npg), *([cv] * npg), w, new_rows(k_new), new_rows(v_new), z, bm, bm_new,
            lam_p, norm_g.reshape(1, DIFF_HW)]
    out_specs = [pl.BlockSpec((SAMPLE_ROWS, D_MODEL), row)]
    out_shape = [jax.ShapeDtypeStruct((nb * SAMPLE_ROWS, D_MODEL), f32)]
    scratch = [pltpu.VMEM((DEC_GROUPS, 1, DEC_COLS), f32),
               pltpu.VMEM((DEC_GROUPS, 1, DEC_COLS), f32),
               pltpu.VMEM((DEC_GROUPS, DEC_COLS, DIFF_HW), f32)]
    return args, in_specs, out_specs, out_shape, scratch


def _decode_attention(page_table, *operands, lam_init):
    nb, n_pages = page_table.shape
    n_steps = n_pages // DEC_PAGES_PER_STEP
    args, in_specs, out_specs, out_shape, scratch = _decode_specs(page_table, *operands)
    return pl.pallas_call(
        functools.partial(_decode_kernel, lam_init=lam_init, n_steps=n_steps),
        grid_spec=pltpu.PrefetchScalarGridSpec(
            num_scalar_prefetch=1, grid=(nb, n_steps), in_specs=in_specs, out_specs=out_specs,
            scratch_shapes=scratch),
        out_shape=out_shape,
        compiler_params=pltpu.CompilerParams(dimension_semantics=("parallel", "arbitrary"),
                                             vmem_limit_bytes=DEC_VMEM_LIMIT),
        name="decode_diff_attention",
    )(page_table, *args)[0]


def _fused_attn_kernel(pt_ref, *refs, lam_init, n_steps, n_blocks):
    del pt_ref
    n_att_in = 9
    dec_in = refs[:DEC_INPUTS]
    att_in = refs[DEC_INPUTS:DEC_INPUTS + n_att_in]
    o_dec, o_lo, o_hi = refs[DEC_INPUTS + n_att_in:DEC_INPUTS + n_att_in + 3]
    m_ref, l_ref, acc_ref, kb_ref, vb_ref, s_ref = refs[DEC_INPUTS + n_att_in + 3:]
    s = pl.program_id(1)
    step = pl.program_id(0) * n_steps + s
    _decode_step(s, (*dec_in, o_dec, m_ref, l_ref, acc_ref), lam_init=lam_init, n_steps=n_steps)
    _attn_step(step % (n_blocks // 2), *att_in, o_lo, o_hi, kb_ref, vb_ref, s_ref,
               lam_init=lam_init, n_blocks=n_blocks)


def _diff_attention(page_table, decode_operands, prompt_operands, batch, seq, lam_init):
    nb, n_pages = page_table.shape
    n_steps = n_pages // DEC_PAGES_PER_STEP
    nq = seq // ATT_BLOCK
    half = nq // 2
    if nb * n_steps != batch * DIFF_HEADS * half:
        o_s = _decode_attention(page_table, *decode_operands, lam_init=lam_init)
        return o_s, _prompt_attention(*prompt_operands, batch, seq, lam_init)

    def coords(b, s, pt):
        step = b * n_steps + s
        return step // (DIFF_HEADS * half), (step // half) % DIFF_HEADS, step % half

    d_args, d_in, d_out, d_shape, d_scratch = _decode_specs(page_table, *decode_operands)
    p_args, p_in, p_out, p_shape, p_scratch = _prompt_specs(*prompt_operands, batch, seq, coords)
    o_s, o_lo, o_hi = pl.pallas_call(
        functools.partial(_fused_attn_kernel, lam_init=lam_init, n_steps=n_steps, n_blocks=nq),
        grid_spec=pltpu.PrefetchScalarGridSpec(
            num_scalar_prefetch=1, grid=(nb, n_steps), in_specs=d_in + p_in,
            out_specs=d_out + p_out, scratch_shapes=d_scratch + p_scratch),
        out_shape=d_shape + p_shape,
        compiler_params=pltpu.CompilerParams(dimension_semantics=("arbitrary", "arbitrary"),
                                             vmem_limit_bytes=FUSED_VMEM_LIMIT),
        name="diff_attention_both_groups",
    )(page_table, *d_args, *p_args)
    return o_s, (o_lo, o_hi)


def _gate_kernel(x_ref, w1_ref, w2_ref, b_ref, g_ref):
    gl = lax.dot_general(x_ref[...], w1_ref[...], _NT, preferred_element_type=f32)
    u = jnp.dot(gl.astype(bf16), w2_ref[...], preferred_element_type=f32) + b_ref[...]
    ls = jnp.minimum(u, 0.0) - jnp.log(1.0 + jnp.exp(-jnp.abs(u)))
    g_ref[...] = ls * (1.0 / GLA_GATE_NORMALIZER)


def _gla_gate(x, w1, w2, b):
    m = x.shape[0]
    tm = min(m, 512)
    return pl.pallas_call(
        _gate_kernel,
        grid=(m // tm,),
        in_specs=[pl.BlockSpec((tm, D_MODEL), lambda i: (i, 0)),
                  pl.BlockSpec((LANES, D_MODEL), lambda i: (0, 0)),
                  pl.BlockSpec((LANES, GLA_DK_W), lambda i: (0, 0)),
                  pl.BlockSpec((1, GLA_DK_W), lambda i: (0, 0))],
        out_specs=pl.BlockSpec((tm, GLA_DK_W), lambda i: (i, 0)),
        out_shape=jax.ShapeDtypeStruct((m, GLA_DK_W), f32),
        compiler_params=_params("parallel"),
        name="gla_gate",
    )(x, w1, w2, b.reshape(1, GLA_DK_W))


def _gla_kernel(*refs, chunk, chunks_per_step, t_valid, n_steps, has_state):
    if has_state:
        q_ref, k_ref, v_ref, g_ref, z_ref, ng_ref, s0_ref, o_ref, so_ref, st_ref = refs
    else:
        q_ref, k_ref, v_ref, g_ref, z_ref, ng_ref, o_ref, so_ref, st_ref = refs
    n = pl.program_id(1)

    @pl.when(n == 0)
    def _():
        for h in range(GLA_HEADS):
            if has_state:
                st_ref[h] = s0_ref[h]
            else:
                st_ref[h] = jnp.zeros(st_ref.shape[1:], f32)

    row = lax.broadcasted_iota(jnp.int32, (chunk, chunk), 0)
    col = lax.broadcasted_iota(jnp.int32, (chunk, chunk), 1)
    causal = row >= col
    tri = causal.astype(f32).astype(bf16)
    heads = range(GLA_HEADS)
    kc = [slice(h * GLA_DK, (h + 1) * GLA_DK) for h in heads]
    vc = [slice(h * GLA_DV, (h + 1) * GLA_DV) for h in heads]
    intra = []
    for c in range(chunks_per_step):
        rows = slice(c * chunk, (c + 1) * chunk)
        g = g_ref[rows, :]
        k = k_ref[rows, :]
        if t_valid < chunk:
            valid = lax.broadcasted_iota(jnp.int32, (chunk, GLA_DK_W), 0) < t_valid
            g = jnp.where(valid, g, 0.0)
            k = jnp.where(valid, k, 0.0)
        g_hi = g.astype(bf16)
        g_lo = (g - g_hi.astype(f32)).astype(bf16)
        b = (jnp.dot(tri, g_hi, preferred_element_type=f32)
             + jnp.dot(tri, g_lo, preferred_element_type=f32))
        b_last = b[chunk - 1:chunk, :]
        qe = (q_ref[rows, :] * jnp.exp(b)).astype(bf16)
        ke = (k * jnp.exp(-b)).astype(bf16)
        kd = (k * jnp.exp(b_last - b)).astype(bf16)
        vb = v_ref[rows, :].astype(bf16)
        decay = jnp.broadcast_to(jnp.exp(b_last), (SUBLANES, GLA_DK_W)).T[:, :1]
        a = [lax.dot_general(qe[:, kc[h]], ke[:, kc[h]], _NT, preferred_element_type=f32)
             for h in heads]
        a = [jnp.where(causal, a[h], 0.0).astype(bf16) for h in heads]
        o_intra = [jnp.dot(a[h], vb[:, vc[h]], preferred_element_type=f32) for h in heads]
        intra.append((rows, qe, kd, vb, decay, o_intra))
    st = [st_ref[h] for h in heads]
    for rows, qe, kd, vb, decay, o_intra in intra:
        o = [o_intra[h] + jnp.dot(qe[:, kc[h]], st[h].astype(bf16), preferred_element_type=f32)
             for h in heads]
        st = [decay[kc[h]] * st[h] + lax.dot_general(kd[:, kc[h]], vb[:, vc[h]], _TN,
                                                     preferred_element_type=f32) for h in heads]
        for h in heads:
            o_ref[rows, vc[h]] = _head_epilogue(o[h], ng_ref[...], z_ref[rows, vc[h]],
                                                1.0).astype(o_ref.dtype)
    for h in heads:
        st_ref[h] = st[h]

    @pl.when(n == n_steps - 1)
    def _():
        for h in range(GLA_HEADS):
            so_ref[h] = st_ref[h]


def _gla(q, k, v, g, z, norm_g, s0, layer, batch, seq, t_valid):
    chunk = min(GLA_CHUNK, seq)
    per_step = GLA_CHUNKS_PER_STEP if seq % (GLA_CHUNKS_PER_STEP * chunk) == 0 else 1
    rows = per_step * chunk
    nc = seq // rows
    has_state = s0 is not None
    out_dtype = bf16 if chunk % (2 * SUBLANES) == 0 else f32
    tmap = lambda b, n: (b * nc + n, 0)
    smap = lambda b, n: (b, 0, 0, 0)
    state_spec = pl.BlockSpec((None, GLA_HEADS, GLA_DK, GLA_DV), smap)
    in_specs = [pl.BlockSpec((rows, GLA_DK_W), tmap),
                pl.BlockSpec((rows, GLA_DK_W), tmap),
                pl.BlockSpec((rows, D_MODEL), tmap),
                pl.BlockSpec((rows, GLA_DK_W), tmap),
                pl.BlockSpec((rows, D_MODEL), tmap),
                pl.BlockSpec((1, GLA_DV), lambda b, n: (0, 0))]
    args = [q, k, v, g, z, norm_g.reshape(1, GLA_DV)]
    if has_state:
        in_specs.append(pl.BlockSpec((None, None, GLA_HEADS, GLA_DK, GLA_DV),
                                     lambda b, n: (layer, b, 0, 0, 0)))
        args.append(s0)
    return pl.pallas_call(
        functools.partial(_gla_kernel, chunk=chunk, chunks_per_step=per_step, t_valid=t_valid,
                          n_steps=nc, has_state=has_state),
        grid=(batch, nc),
        in_specs=in_specs,
        out_specs=[pl.BlockSpec((rows, D_MODEL), tmap), state_spec],
        out_shape=[jax.ShapeDtypeStruct((batch * seq, D_MODEL), out_dtype),
                   jax.ShapeDtypeStruct((batch, GLA_HEADS, GLA_DK, GLA_DV), f32)],
        scratch_shapes=[pltpu.VMEM((GLA_HEADS, GLA_DK, GLA_DV), f32)],
        compiler_params=_params("parallel", "arbitrary"),
        name="gla_chunked",
    )(*args)


def _trunk(xp, xs, batch, seq, dec_batch, dec_seq, cache_k, cache_v, state_gla, page_table, weights):
    (rel_bias, diff_w_in, diff_lambda, diff_norm_g, diff_w_out, gla_w_in, gla_w_g1, gla_w_g2,
     gla_b_g, gla_norm_g, gla_w_out, ln_g, ln_b) = weights
    n_diff = (DEPTH + 1) // 2
    xpb = xp.astype(bf16)
    xsb = xs.astype(bf16)
    kbuf = vbuf = None
    states_p, states_s = [], []
    far = _bias_by_distance(rel_bias, np.array([2 * ATT_BLOCK]))[:, 0]
    tiles = (_prompt_bias_tiles(rel_bias, ATT_BLOCK) - far[:, None, None]) * LOG2E
    bm, bm_new = [t * LOG2E for t in _decode_bias_tables(rel_bias, dec_seq)]
    q_scale = DIFF_DH ** -0.5 * LOG2E
    for i in range(DEPTH):
        j = i // 2
        if i % 2 == 0:
            w = diff_w_in
            qp, qs = _matmul(xpb, xsb, w, j, 0, D_MODEL, bf16, scale=q_scale)
            zp, zs = _matmul(xpb, xsb, w, j, 3 * D_MODEL, D_MODEL, f32, after=(qp,))
            kbuf = _matmul(xpb, xsb, w, j, D_MODEL, D_MODEL, f32, slots=n_diff, slot=j, into=kbuf,
                           after=(zp,))
            vbuf = _matmul(xpb, xsb, w, j, 2 * D_MODEL, D_MODEL, f32, slots=n_diff, slot=j,
                           into=vbuf, after=(kbuf[0],))
            lam_init = 0.8 - 0.6 * math.exp(-0.3 * i)
            os_, op = _diff_attention(
                page_table,
                (cache_k, cache_v, j, qs, kbuf[1][j], vbuf[1][j], zs, bm, bm_new, diff_lambda[j],
                 diff_norm_g[j]),
                (qp, kbuf[0], vbuf[0], j, zp, tiles, diff_lambda[j], diff_norm_g[j]),
                batch, seq, lam_init)
            w_out = diff_w_out[j]
        else:
            w = gla_w_in
            proj = functools.partial(_matmul, xpb, xsb, w, j, w_transposed=True)
            qp, qs = proj(0, GLA_DK_W, f32, scale=GLA_DK ** -0.5)
            kp, ks = proj(GLA_DK_W, GLA_DK_W, f32)
            vp, vs = proj(2 * GLA_DK_W, D_MODEL, f32)
            zp, zs = proj(2 * GLA_DK_W + D_MODEL, D_MODEL, f32)
            gp = _gla_gate(xpb, gla_w_g1[j], gla_w_g2[j], gla_b_g[j])
            gs = _gla_gate(xsb, gla_w_g1[j], gla_w_g2[j], gla_b_g[j])
            op, sp = _gla(qp, kp, vp, gp, zp, gla_norm_g[j], None, j, batch, seq, seq)
            os_, ss = _gla(qs, ks, vs, gs, zs, gla_norm_g[j], state_gla, j, dec_batch, SAMPLE_ROWS,
                           dec_seq)
            states_p.append(sp)
            states_s.append(ss)
            w_out = gla_w_out[j]
        xp, xpb = _out_proj_norm(op, w_out, xp, ln_g[i], ln_b[i], seq=seq)
        xs, xsb = _out_proj_norm(os_, w_out, xs, ln_g[i], ln_b[i])
    return xp, xs, kbuf, vbuf, states_p, states_s


def kernel(x_prompt, x_sample, cache_k, cache_v, state_gla, page_table, rel_bias, diff_w_in,
           diff_lambda, diff_norm_g, diff_w_out, gla_w_in, gla_w_g2, gla_b_g, gla_norm_g, gla_w_out,
           ln_g, ln_b):
    batch, seq, _ = x_prompt.shape
    dec_batch, dec_seq, _ = x_sample.shape
    n_pages = page_table.shape[1]
    assert n_pages >= 2 and n_pages % DEC_PAGES_PER_STEP == 0 and dec_seq <= SAMPLE_ROWS
    assert cache_k.shape[2:] == (PAGE_SIZE, DIFF_HEADS, DIFF_HW)
    assert seq % (4 * ATT_BLOCK) == 0 and seq % GLA_CHUNK == 0

    gate_off = 2 * GLA_DK_W + 2 * D_MODEL
    gla_w_t = jnp.swapaxes(gla_w_in, 1, 2)
    w_g1 = jnp.pad(gla_w_t[:, gate_off:, :], ((0, 0), (0, LANES - GLA_GATE_RANK), (0, 0)))
    w_g2 = jnp.pad(gla_w_g2, ((0, 0), (0, LANES - GLA_GATE_RANK), (0, 0)))
    weights = (rel_bias, diff_w_in, diff_lambda, diff_norm_g, diff_w_out.astype(bf16),
               gla_w_t, w_g1.astype(bf16), w_g2.astype(bf16), gla_b_g,
               gla_norm_g,
               gla_w_out.astype(bf16), ln_g, ln_b)

    rows = SAMPLE_ROWS
    xs = jnp.pad(x_sample, ((0, 0), (0, rows - dec_seq), (0, 0))).reshape(dec_batch * rows, D_MODEL)
    y_p, y_s, kbuf, vbuf, s_p, s_s = _trunk(
        x_prompt.reshape(batch * seq, D_MODEL), xs, batch, seq, dec_batch, dec_seq,
        cache_k, cache_v, state_gla, page_table, weights)

    def rows_out(a, b, t, t_keep):
        return a.reshape(a.shape[0], b, t, DIFF_HEADS, DIFF_HW)[:, :, :t_keep]

    return (y_p.reshape(batch, seq, D_MODEL),
            y_s.reshape(dec_batch, rows, D_MODEL)[:, :dec_seq],
            rows_out(kbuf[0], batch, seq, seq), rows_out(vbuf[0], batch, seq, seq), jnp.stack(s_p),
            rows_out(kbuf[1], dec_batch, rows, dec_seq), rows_out(vbuf[1], dec_batch, rows, dec_seq),
            jnp.stack(s_s))
```

```python
import functools
import math

import jax
import jax.numpy as jnp
import numpy as np
from jax import lax
from jax.experimental import pallas as pl
from jax.experimental.pallas import tpu as pltpu

D_MODEL = 2048
DEPTH = 4
PAGE_SIZE = 128

DIFF_HEADS = 8
DIFF_DH = D_MODEL // (2 * DIFF_HEADS)
DIFF_HW = 2 * DIFF_DH

GLA_HEADS = 4
GLA_DK_W = D_MODEL // 2
GLA_DK = GLA_DK_W // GLA_HEADS
GLA_DV = D_MODEL // GLA_HEADS
GLA_GATE_RANK = 16
GLA_GATE_NORMALIZER = 16.0
GLA_CHUNK = 64
GLA_CHUNKS_PER_STEP = 2

REL_BUCKETS = 32
REL_MAX_DIST = 128

NORM_EPS = 1e-5
DEEPNORM_ALPHA = (2 * DEPTH) ** 0.25

LANES = 128
SUBLANES = 8
VMEM_LIMIT = 48 * 1024 * 1024
MASK_VALUE = -1e30
LOG2E = math.log2(math.e)

ATT_BLOCK = 256
OUT_ROW_GROUPS = 4
DEC_PAGES_PER_STEP = 8
DEC_GROUPS = 4
DEC_VMEM_LIMIT = 56 * 1024 * 1024
FUSED_VMEM_LIMIT = 60 * 1024 * 1024
SAMPLE_ROWS = SUBLANES
PAGE_ROWS = PAGE_SIZE * DIFF_HEADS
NEW_ROWS = 2 * SAMPLE_ROWS * DIFF_HEADS
DEC_COLS = 2 * DIFF_HEADS * SAMPLE_ROWS
assert DEC_COLS == LANES and NEW_ROWS == LANES

_NT = (((1,), (1,)), ((), ()))
_TN = (((0,), (0,)), ((), ()))

bf16 = jnp.bfloat16
f32 = jnp.float32


def _params(*sem):
    return pltpu.CompilerParams(dimension_semantics=sem, vmem_limit_bytes=VMEM_LIMIT)


def _silu(z):
    return z * (1.0 / (1.0 + jnp.exp(-z)))


def _mm_kernel(xp_ref, xs_ref, w_ref, *rest, scale, w_transposed):
    op_ref, os_ref, wb_ref = rest[-3:]
    dims = _NT if w_transposed else (((1,), (0,)), ((), ()))

    def project(x_ref, o_ref):
        acc = lax.dot_general(x_ref[...], wb_ref[...], dims, preferred_element_type=f32)
        if scale != 1.0:
            acc = acc * scale
        o_ref[...] = acc.astype(o_ref.dtype)

    @pl.when(pl.program_id(1) == 0)
    def _():
        wb_ref[...] = w_ref[...].astype(bf16)
        project(xs_ref, os_ref)

    project(xp_ref, op_ref)


def _matmul(xp, xs, w, layer, col_off, n, out_dtype, scale=1.0, slots=None, slot=0, into=None,
            w_transposed=False, after=()):
    m, k = xp.shape
    ms = xs.shape[0]
    tm = min(m, 1024)
    tn = min(n, 1024)
    assert m % tm == 0 and n % tn == 0 and col_off % tn == 0
    off = col_off // tn
    if w_transposed:
        w_spec = pl.BlockSpec((None, tn, k), lambda j, i: (layer, j + off, 0))
        w_tile = (tn, k)
    else:
        w_spec = pl.BlockSpec((None, k, tn), lambda j, i: (layer, 0, j + off))
        w_tile = (k, tn)
    in_specs = [pl.BlockSpec((tm, k), lambda j, i: (i, 0)),
                pl.BlockSpec((ms, k), lambda j, i: (0, 0)),
                w_spec]
    args = [xp, xs, w]
    aliases = {}
    if slots is None:
        out_specs = [pl.BlockSpec((tm, tn), lambda j, i: (i, j)),
                     pl.BlockSpec((ms, tn), lambda j, i: (0, j))]
        out_shape = [jax.ShapeDtypeStruct((m, n), out_dtype),
                     jax.ShapeDtypeStruct((ms, n), out_dtype)]
    else:
        out_specs = [pl.BlockSpec((None, tm, tn), lambda j, i: (slot, i, j)),
                     pl.BlockSpec((None, ms, tn), lambda j, i: (slot, 0, j))]
        out_shape = [jax.ShapeDtypeStruct((slots, m, n), out_dtype),
                     jax.ShapeDtypeStruct((slots, ms, n), out_dtype)]
        if into is not None:
            in_specs += [pl.BlockSpec(memory_space=pl.ANY)] * 2
            args += list(into)
            aliases = {3: 0, 4: 1}
    in_specs += [pl.BlockSpec(memory_space=pl.ANY)] * len(after)
    args += list(after)
    return pl.pallas_call(
        functools.partial(_mm_kernel, scale=scale, w_transposed=w_transposed),
        grid=(n // tn, m // tm),
        in_specs=in_specs,
        out_specs=out_specs,
        out_shape=out_shape,
        input_output_aliases=aliases,
        scratch_shapes=[pltpu.VMEM(w_tile, bf16)],
        compiler_params=_params("parallel", "arbitrary"),
        name="proj_matmul",
    )(*args)


def _hold_kernel(*refs):
    del refs


def _hold_until(bufs, anchor):
    any_spec = pl.BlockSpec(memory_space=pl.ANY)
    return pl.pallas_call(
        _hold_kernel,
        in_specs=[any_spec] * (len(bufs) + 1),
        out_specs=[any_spec] * len(bufs),
        out_shape=[jax.ShapeDtypeStruct(b.shape, b.dtype) for b in bufs],
        input_output_aliases={i: i for i in range(len(bufs))},
        name="hold_row_buffers",
    )(*bufs, anchor)


def _out_kernel(*refs, tiles_per_batch):
    if tiles_per_batch is None:
        o_ref, w_ref, x_ref, g_ref, b_ref, xo_ref, xb_ref = refs
        o = o_ref[...].astype(bf16)
    else:
        olo_ref, ohi_ref, w_ref, x_ref, g_ref, b_ref, xo_ref, xb_ref = refs
        first_half = (pl.program_id(0) % tiles_per_batch) < tiles_per_batch // 2
        o = jnp.where(first_half, olo_ref[...], ohi_ref[...])
    tm = o.shape[0]
    group = tm // OUT_ROW_GROUPS if tm >= OUT_ROW_GROUPS * LANES else tm
    for r0 in range(0, tm, group):
        rows = slice(r0, r0 + group)
        y = jnp.dot(o[rows], w_ref[...], preferred_element_type=f32)
        r = DEEPNORM_ALPHA * x_ref[rows, :] + y
        mu = jnp.mean(r, axis=-1, keepdims=True)
        d = r - mu
        var = jnp.mean(d * d, axis=-1, keepdims=True)
        xn = d * lax.rsqrt(var + NORM_EPS) * g_ref[...] + b_ref[...]
        xo_ref[rows, :] = xn
        xb_ref[rows, :] = xn.astype(bf16)


def _out_proj_norm(o, w, x, g, b, seq=None):
    m = x.shape[0]
    tm = min(m, 512)
    row = lambda i: (i, 0)
    fixed = lambda i: (0, 0)
    if isinstance(o, (tuple, list)):
        tpb = seq // tm
        half = tpb // 2
        assert seq % tm == 0 and tpb % 2 == 0
        o_specs = [pl.BlockSpec((tm, D_MODEL),
                                lambda i: ((i // tpb) * half + jnp.minimum(i % tpb, half - 1), 0)),
                   pl.BlockSpec((tm, D_MODEL),
                                lambda i: ((i // tpb) * half + jnp.maximum(i % tpb - half, 0), 0))]
        o_args = list(o)
    else:
        tpb = None
        o_specs = [pl.BlockSpec((tm, D_MODEL), row)]
        o_args = [o]
    return pl.pallas_call(
        functools.partial(_out_kernel, tiles_per_batch=tpb),
        grid=(m // tm,),
        in_specs=o_specs + [
                  pl.BlockSpec((D_MODEL, D_MODEL), fixed),
                  pl.BlockSpec((tm, D_MODEL), row),
                  pl.BlockSpec((1, D_MODEL), fixed),
                  pl.BlockSpec((1, D_MODEL), fixed)],
        out_specs=[pl.BlockSpec((tm, D_MODEL), row), pl.BlockSpec((tm, D_MODEL), row)],
        out_shape=[jax.ShapeDtypeStruct((m, D_MODEL), f32),
                   jax.ShapeDtypeStruct((m, D_MODEL), bf16)],
        compiler_params=_params("parallel"),
        name="out_proj_deepnorm",
    )(*o_args, w, x, g.reshape(1, D_MODEL), b.reshape(1, D_MODEL))


def _bias_by_distance(rel_bias, dist):
    n = jnp.asarray(dist, jnp.int32)
    max_exact = REL_BUCKETS // 2
    nf = jnp.maximum(n, 1).astype(f32)
    large = max_exact + (jnp.log(nf / max_exact) / math.log(REL_MAX_DIST / max_exact)
                         * (REL_BUCKETS - max_exact)).astype(jnp.int32)
    large = jnp.minimum(large, REL_BUCKETS - 1)
    bucket = jnp.where(n < max_exact, n, large)
    return jnp.moveaxis(rel_bias.astype(f32)[bucket], -1, 0)


def _prompt_bias_tiles(rel_bias, blk):
    k = np.arange(3 * blk - 1)
    d = 2 * blk - 1 - k
    u = jnp.where(jnp.asarray(d >= 0), _bias_by_distance(rel_bias, np.maximum(d, 0)), MASK_VALUE)
    period = 3 * blk
    flat = jnp.tile(jnp.pad(u, ((0, 0), (0, 1))), (1, blk))[:, :blk * (period - 1)]
    return flat.reshape(DIFF_HEADS, blk, period - 1)[:, :, blk - 1:3 * blk - 1]


def _decode_bias_tables(rel_bias, t_valid):
    col = np.arange(DEC_COLS)
    col_h = (col // SAMPLE_ROWS) % DIFF_HEADS
    reps = DEC_COLS // SAMPLE_ROWS
    tq = np.arange(SAMPLE_ROWS)[None, :]
    new_tokens = NEW_ROWS // DIFF_HEADS
    tab = _bias_by_distance(rel_bias, np.arange(2 * PAGE_SIZE + SAMPLE_ROWS + 1))

    def toeplitz_rows(ext, n_tokens):
        wins = [ext[:, t + 1:t + 1 + n_tokens][:, ::-1] for t in range(SAMPLE_ROWS)]
        return jnp.transpose(jnp.stack(wins), (2, 1, 0)).reshape(n_tokens * DIFF_HEADS, SAMPLE_ROWS)

    def expand(vals, visible):
        n_rows = vals.shape[0]
        r = np.arange(n_rows)
        ok = ((r % DIFF_HEADS)[:, None] == col_h[None, :]) & np.tile(visible, (1, reps))
        return jnp.where(jnp.asarray(ok), jnp.tile(vals, (1, reps)), MASK_VALUE)

    everything = np.ones((PAGE_ROWS, SAMPLE_ROWS), bool)
    far_vals = jnp.broadcast_to(jnp.tile(tab[:, 2 * PAGE_SIZE], PAGE_SIZE)[:, None],
                                (PAGE_ROWS, SAMPLE_ROWS))
    far = expand(far_vals, everything)
    last = expand(toeplitz_rows(tab, PAGE_SIZE), everything)
    s_new = (np.arange(NEW_ROWS) // DIFF_HEADS)[:, None]
    ext = jnp.pad(tab[:, :SAMPLE_ROWS], ((0, 0), (new_tokens, 0)))
    new = expand(toeplitz_rows(ext, new_tokens), (s_new <= tq) & (s_new < t_valid))
    return jnp.stack([far, last]), new


def _lambda_value(lp_ref, lam_init):
    lp = lp_ref[...]
    a = jnp.sum(lp[0:1] * lp[1:2], axis=-1, keepdims=True)
    b = jnp.sum(lp[2:3] * lp[3:4], axis=-1, keepdims=True)
    return jnp.exp(a) - jnp.exp(b) + lam_init


def _head_epilogue(o, g, z, out_scale):
    ms = jnp.mean(o * o, axis=-1, keepdims=True)
    return (o * lax.rsqrt(ms + NORM_EPS) * g * out_scale) * _silu(z)


def _attn_step(i, qlo_ref, qhi_ref, k_ref, v_ref, zlo_ref, zhi_ref, bias_ref, lp_ref, g_ref,
               olo_ref, ohi_ref, kb_ref, vb_ref, s_ref, *, lam_init, n_blocks):
    blk = ATT_BLOCK

    @pl.when(i == 0)
    def _():
        kb_ref[...] = k_ref[...].astype(bf16)
        vb_ref[...] = v_ref[...].astype(bf16)

    lam = _lambda_value(lp_ref, lam_init)

    def lane_fold(x):
        return [x[:, c * LANES:(c + 1) * LANES] for c in range(blk // LANES)]

    def logits_and_max(q_ref, nvis, base):
        q = q_ref[...]
        qs = (q[:, :DIFF_DH], q[:, DIFF_DH:])
        mx = [None, None]
        for j in range(nvis):
            kj = kb_ref[j * blk:(j + 1) * blk, :]
            if j == nvis - 1:
                bias = bias_ref[:, blk:]
            elif j == nvis - 2:
                bias = bias_ref[:, :blk]
            else:
                bias = None
            for mp in range(2):
                s = lax.dot_general(qs[mp], kj[:, mp * DIFF_DH:(mp + 1) * DIFF_DH], _NT,
                                    preferred_element_type=f32)
                if bias is not None:
                    s = s + bias
                s_ref[mp, base + j] = s
                for part in lane_fold(s):
                    mx[mp] = part if mx[mp] is None else jnp.maximum(mx[mp], part)
        return [jnp.max(mx[mp], axis=-1, keepdims=True) for mp in range(2)]

    def exponentials(nvis, base, m):
        ls = [None, None]
        for j in range(nvis):
            for mp in range(2):
                p = jnp.exp2(s_ref[mp, base + j] - m[mp])
                s_ref[mp, base + j] = p
                for part in lane_fold(p):
                    ls[mp] = part if ls[mp] is None else ls[mp] + part
        return (1.0 / jnp.sum(ls[0], axis=-1, keepdims=True),
                lam / jnp.sum(ls[1], axis=-1, keepdims=True))

    def weighted_values(nvis, base, c, z_ref, o_ref):
        acc = None
        for j in range(nvis):
            pd = (s_ref[0, base + j] * c[0] - s_ref[1, base + j] * c[1]).astype(bf16)
            pv = jnp.dot(pd, vb_ref[j * blk:(j + 1) * blk, :], preferred_element_type=f32)
            acc = pv if acc is None else acc + pv
        o_ref[...] = _head_epilogue(acc, g_ref[...], z_ref[...], 1.0 - lam_init).astype(bf16)

    def block_pair(lo):
        n_lo, n_hi = lo + 1, n_blocks - lo
        m_lo = logits_and_max(qlo_ref, n_lo, 0)
        m_hi = logits_and_max(qhi_ref, n_hi, n_lo)
        c_lo = exponentials(n_lo, 0, m_lo)
        c_hi = exponentials(n_hi, n_lo, m_hi)
        weighted_values(n_lo, 0, c_lo, zlo_ref, olo_ref)
        weighted_values(n_hi, n_lo, c_hi, zhi_ref, ohi_ref)

    for lo in range(n_blocks // 2):
        pl.when(i == lo)(functools.partial(block_pair, lo))


def _attn_kernel(*refs, **static):
    _attn_step(pl.program_id(2), *refs, **static)


def _prompt_specs(q, k, v, slot, z, bias_tiles, lam_p, norm_g, batch, seq, coords):
    nq = seq // ATT_BLOCK
    half = nq // 2
    blk = ATT_BLOCK

    def at(f):
        return lambda *g: f(*coords(*g))

    lo_in = at(lambda b, h, i: (b * nq + i, h))
    hi_in = at(lambda b, h, i: (b * nq + nq - 1 - i, h))
    lo_out = at(lambda b, h, i: (b * half + i, h))
    hi_out = at(lambda b, h, i: (b * half + half - 1 - i, h))
    kvmap = at(lambda b, h, i: (slot, b, h))
    out = jax.ShapeDtypeStruct((batch * seq // 2, D_MODEL), bf16)
    args = [q, q, k, v, z, z, bias_tiles, lam_p, norm_g.reshape(1, DIFF_HW)]
    in_specs = [pl.BlockSpec((blk, DIFF_HW), lo_in),
                pl.BlockSpec((blk, DIFF_HW), hi_in),
                pl.BlockSpec((None, seq, DIFF_HW), kvmap),
                pl.BlockSpec((None, seq, DIFF_HW), kvmap),
                pl.BlockSpec((blk, DIFF_HW), lo_in),
                pl.BlockSpec((blk, DIFF_HW), hi_in),
                pl.BlockSpec((None, blk, 2 * blk), at(lambda b, h, i: (h, 0, 0))),
                pl.BlockSpec((4, DIFF_DH), at(lambda b, h, i: (0, 0))),
                pl.BlockSpec((1, DIFF_HW), at(lambda b, h, i: (0, 0)))]
    out_specs = [pl.BlockSpec((blk, DIFF_HW), lo_out), pl.BlockSpec((blk, DIFF_HW), hi_out)]
    scratch = [pltpu.VMEM((seq, DIFF_HW), bf16),
               pltpu.VMEM((seq, DIFF_HW), bf16),
               pltpu.VMEM((2, nq + 1, blk, blk), f32)]
    return args, in_specs, out_specs, [out, out], scratch


def _prompt_attention(q, k, v, slot, z, bias_tiles, lam_p, norm_g, batch, seq, lam_init):
    nq = seq // ATT_BLOCK
    args, in_specs, out_specs, out_shape, scratch = _prompt_specs(
        q, k, v, slot, z, bias_tiles, lam_p, norm_g, batch, seq, lambda b, h, i: (b, h, i))
    return pl.pallas_call(
        functools.partial(_attn_kernel, lam_init=lam_init, n_blocks=nq),
        grid=(batch, DIFF_HEADS, nq // 2),
        in_specs=in_specs,
        out_specs=out_specs,
        out_shape=out_shape,
        scratch_shapes=scratch,
        compiler_params=_params("parallel", "parallel", "arbitrary"),
        name="prompt_diff_attention",
    )(*args)


DEC_INPUTS = 2 * DEC_PAGES_PER_STEP + 8


def _decode_kernel(pt_ref, *refs, **static):
    del pt_ref
    _decode_step(pl.program_id(1), refs, **static)


def _decode_step(s, refs, *, lam_init, n_steps):
    npg = DEC_PAGES_PER_STEP
    k_pages = refs[:npg]
    v_pages = refs[npg:2 * npg]
    (w_ref, kn_ref, vn_ref, z_ref, bm_ref, bmn_ref, lp_ref, g_ref, o_ref,
     m_ref, l_ref, acc_ref) = refs[2 * npg:]

    @pl.when(s == 0)
    def _():
        m_ref[...] = jnp.full(m_ref.shape, MASK_VALUE, f32)
        l_ref[...] = jnp.zeros(l_ref.shape, f32)
        acc_ref[...] = jnp.zeros(acc_ref.shape, f32)

    w = w_ref[...]
    eye = (lax.broadcasted_iota(jnp.int32, (DEC_COLS, DEC_COLS), 0)
           == lax.broadcasted_iota(jnp.int32, (DEC_COLS, DEC_COLS), 1))

    def to_column(row):
        return jnp.sum(jnp.where(eye, row, 0.0), axis=1, keepdims=True)

    def update(grp, blocks):
        logits = [jnp.dot(k, w, preferred_element_type=f32) + bm for k, _, bm in blocks]
        cmax = None
        for lg in logits:
            c = jnp.max(lg, axis=0, keepdims=True)
            cmax = c if cmax is None else jnp.maximum(cmax, c)
        m_old = m_ref[grp]
        m_new = jnp.maximum(m_old, cmax)
        a = jnp.exp2(m_old - m_new)
        lsum = None
        pv = None
        for lg, (_, v, _) in zip(logits, blocks):
            p = jnp.exp2(lg - m_new)
            ps = jnp.sum(p, axis=0, keepdims=True)
            lsum = ps if lsum is None else lsum + ps
            d = jnp.dot(p.T.astype(bf16), v.astype(bf16), preferred_element_type=f32)
            pv = d if pv is None else pv + d
        m_ref[grp] = m_new
        l_ref[grp] = a * l_ref[grp] + lsum
        acc_ref[grp] = acc_ref[grp] * to_column(a) + pv

    per_group = npg // DEC_GROUPS
    for grp in range(DEC_GROUPS):
        blocks = []
        for pg in range(grp * per_group, (grp + 1) * per_group):
            if pg == npg - 1:
                bm = bm_ref[jnp.where(s == n_steps - 1, 1, 0)]
            else:
                bm = bm_ref[0]
            blocks.append((k_pages[pg][...], v_pages[pg][...], bm))
        update(grp, blocks)

    @pl.when(s == n_steps - 1)
    def _():
        update(0, [(kn_ref[...], vn_ref[...], bmn_ref[...])])
        m_all = m_ref[0]
        for grp in range(1, DEC_GROUPS):
            m_all = jnp.maximum(m_all, m_ref[grp])
        l_all = None
        acc_all = None
        for grp in range(DEC_GROUPS):
            a = jnp.exp2(m_ref[grp] - m_all)
            lg = a * l_ref[grp]
            ag = acc_ref[grp] * to_column(a)
            l_all = lg if l_all is None else l_all + lg
            acc_all = ag if acc_all is None else acc_all + ag
        on = acc_all * (1.0 / to_column(l_all))
        lam = _lambda_value(lp_ref, lam_init)
        half = DEC_COLS // 2
        for h in range(DIFF_HEADS):
            r0 = h * SAMPLE_ROWS
            o = on[r0:r0 + SAMPLE_ROWS] - lam * on[half + r0:half + r0 + SAMPLE_ROWS]
            lo = h * DIFF_HW
            o_ref[:, lo:lo + DIFF_HW] = _head_epilogue(o, g_ref[...], z_ref[:, lo:lo + DIFF_HW],
                                                       1.0 - lam_init)


def _decode_specs(page_table, cache_k, cache_v, layer, q, k_new, v_new, z, bm, bm_new, lam_p, norm_g):
    nb, n_pages = page_table.shape
    npg = DEC_PAGES_PER_STEP
    ck = cache_k.reshape(cache_k.shape[0], cache_k.shape[1], PAGE_ROWS, DIFF_HW)
    cv = cache_v.reshape(cache_v.shape[0], cache_v.shape[1], PAGE_ROWS, DIFF_HW)

    q5 = q.reshape(nb, SAMPLE_ROWS, DIFF_HEADS, 2, DIFF_DH)
    qt = jnp.transpose(q5, (0, 3, 4, 2, 1)).reshape(nb, 2, DIFF_DH, DEC_COLS // 2)
    zeros = jnp.zeros_like(qt[:, 0])
    w = jnp.concatenate([jnp.concatenate([qt[:, 0], zeros], axis=-1),
                         jnp.concatenate([zeros, qt[:, 1]], axis=-1)], axis=1)

    def new_rows(a):
        a = a.reshape(nb, SAMPLE_ROWS, DIFF_HEADS, DIFF_HW)
        a = jnp.pad(a, ((0, 0), (0, NEW_ROWS // DIFF_HEADS - SAMPLE_ROWS), (0, 0), (0, 0)))
        return a.reshape(nb, NEW_ROWS, DIFF_HW)

    def page_spec(pg):
        return pl.BlockSpec((None, None, PAGE_ROWS, DIFF_HW),
                            lambda b, s, pt: (layer, pt[b, s * npg + pg], 0, 0))

    per_batch3 = lambda b, s, pt: (b, 0, 0)
    row = lambda b, s, pt: (b, 0)
    fixed2 = lambda b, s, pt: (0, 0)
    in_specs = ([page_spec(pg) for pg in range(npg)] + [page_spec(pg) for pg in range(npg)]
                + [pl.BlockSpec((None, DIFF_HW, DEC_COLS), per_batch3),
                   pl.BlockSpec((None, NEW_ROWS, DIFF_HW), per_batch3),
                   pl.BlockSpec((None, NEW_ROWS, DIFF_HW), per_batch3),
                   pl.BlockSpec((SAMPLE_ROWS, D_MODEL), row),
                   pl.BlockSpec((2, PAGE_ROWS, DEC_COLS), lambda b, s, pt: (0, 0, 0)),
                   pl.BlockSpec((NEW_ROWS, DEC_COLS), fixed2),
                   pl.BlockSpec((4, DIFF_DH), fixed2),
                   pl.BlockSpec((1, DIFF_HW), fixed2)])
    assert len(in_specs) == DEC_INPUTS
    args = [*([ck] * npg), *([cv] * npg), w, new_rows(k_new), new_rows(v_new), z, bm, bm_new,
            lam_p, norm_g.reshape(1, DIFF_HW)]
    out_specs = [pl.BlockSpec((SAMPLE_ROWS, D_MODEL), row)]
    out_shape = [jax.ShapeDtypeStruct((nb * SAMPLE_ROWS, D_MODEL), f32)]
    scratch = [pltpu.VMEM((DEC_GROUPS, 1, DEC_COLS), f32),
               pltpu.VMEM((DEC_GROUPS, 1, DEC_COLS), f32),
               pltpu.VMEM((DEC_GROUPS, DEC_COLS, DIFF_HW), f32)]
    return args, in_specs, out_specs, out_shape, scratch


def _decode_attention(page_table, *operands, lam_init):
    nb, n_pages = page_table.shape
    n_steps = n_pages // DEC_PAGES_PER_STEP
    args, in_specs, out_specs, out_shape, scratch = _decode_specs(page_table, *operands)
    return pl.pallas_call(
        functools.partial(_decode_kernel, lam_init=lam_init, n_steps=n_steps),
        grid_spec=pltpu.PrefetchScalarGridSpec(
            num_scalar_prefetch=1, grid=(nb, n_steps), in_specs=in_specs, out_specs=out_specs,
            scratch_shapes=scratch),
        out_shape=out_shape,
        compiler_params=pltpu.CompilerParams(dimension_semantics=("parallel", "arbitrary"),
                                             vmem_limit_bytes=DEC_VMEM_LIMIT),
        name="decode_diff_attention",
    )(page_table, *args)[0]


def _fused_attn_kernel(pt_ref, *refs, lam_init, n_steps, n_blocks):
    del pt_ref
    n_att_in = 9
    dec_in = refs[:DEC_INPUTS]
    att_in = refs[DEC_INPUTS:DEC_INPUTS + n_att_in]
    o_dec, o_lo, o_hi = refs[DEC_INPUTS + n_att_in:DEC_INPUTS + n_att_in + 3]
    m_ref, l_ref, acc_ref, kb_ref, vb_ref, s_ref = refs[DEC_INPUTS + n_att_in + 3:]
    s = pl.program_id(1)
    step = pl.program_id(0) * n_steps + s
    _decode_step(s, (*dec_in, o_dec, m_ref, l_ref, acc_ref), lam_init=lam_init, n_steps=n_steps)
    _attn_step(step % (n_blocks // 2), *att_in, o_lo, o_hi, kb_ref, vb_ref, s_ref,
               lam_init=lam_init, n_blocks=n_blocks)


def _diff_attention(page_table, decode_operands, prompt_operands, batch, seq, lam_init):
    nb, n_pages = page_table.shape
    n_steps = n_pages // DEC_PAGES_PER_STEP
    nq = seq // ATT_BLOCK
    half = nq // 2
    if nb * n_steps != batch * DIFF_HEADS * half:
        o_s = _decode_attention(page_table, *decode_operands, lam_init=lam_init)
        return o_s, _prompt_attention(*prompt_operands, batch, seq, lam_init)

    def coords(b, s, pt):
        step = b * n_steps + s
        return step // (DIFF_HEADS * half), (step // half) % DIFF_HEADS, step % half

    d_args, d_in, d_out, d_shape, d_scratch = _decode_specs(page_table, *decode_operands)
    p_args, p_in, p_out, p_shape, p_scratch = _prompt_specs(*prompt_operands, batch, seq, coords)
    o_s, o_lo, o_hi = pl.pallas_call(
        functools.partial(_fused_attn_kernel, lam_init=lam_init, n_steps=n_steps, n_blocks=nq),
        grid_spec=pltpu.PrefetchScalarGridSpec(
            num_scalar_prefetch=1, grid=(nb, n_steps), in_specs=d_in + p_in,
            out_specs=d_out + p_out, scratch_shapes=d_scratch + p_scratch),
        out_shape=d_shape + p_shape,
        compiler_params=pltpu.CompilerParams(dimension_semantics=("arbitrary", "arbitrary"),
                                             vmem_limit_bytes=FUSED_VMEM_LIMIT),
        name="diff_attention_both_groups",
    )(page_table, *d_args, *p_args)
    return o_s, (o_lo, o_hi)


def _gate_kernel(x_ref, w1_ref, w2_ref, b_ref, g_ref):
    gl = lax.dot_general(x_ref[...], w1_ref[...], _NT, preferred_element_type=f32)
    u = jnp.dot(gl.astype(bf16), w2_ref[...], preferred_element_type=f32) + b_ref[...]
    ls = jnp.minimum(u, 0.0) - jnp.log(1.0 + jnp.exp(-jnp.abs(u)))
    g_ref[...] = ls * (1.0 / GLA_GATE_NORMALIZER)


def _gla_gate(x, w1, w2, b):
    m = x.shape[0]
    tm = min(m, 512)
    return pl.pallas_call(
        _gate_kernel,
        grid=(m // tm,),
        in_specs=[pl.BlockSpec((tm, D_MODEL), lambda i: (i, 0)),
                  pl.BlockSpec((LANES, D_MODEL), lambda i: (0, 0)),
                  pl.BlockSpec((LANES, GLA_DK_W), lambda i: (0, 0)),
                  pl.BlockSpec((1, GLA_DK_W), lambda i: (0, 0))],
        out_specs=pl.BlockSpec((tm, GLA_DK_W), lambda i: (i, 0)),
        out_shape=jax.ShapeDtypeStruct((m, GLA_DK_W), f32),
        compiler_params=_params("parallel"),
        name="gla_gate",
    )(x, w1, w2, b.reshape(1, GLA_DK_W))


def _gla_kernel(*refs, chunk, chunks_per_step, t_valid, n_steps, has_state):
    q_ref, k_ref, v_ref, g_ref, z_ref, ng_ref = refs[:6]
    s0_ref = refs[6] if has_state else None
    o_ref, so_ref, st_ref = refs[-3:]
    n = pl.program_id(1)

    @pl.when(n == 0)
    def _():
        for h in range(GLA_HEADS):
            if has_state:
                st_ref[h] = s0_ref[h]
            else:
                st_ref[h] = jnp.zeros(st_ref.shape[1:], f32)

    row = lax.broadcasted_iota(jnp.int32, (chunk, chunk), 0)
    col = lax.broadcasted_iota(jnp.int32, (chunk, chunk), 1)
    causal = row >= col
    tri = causal.astype(f32).astype(bf16)
    heads = range(GLA_HEADS)
    kc = [slice(h * GLA_DK, (h + 1) * GLA_DK) for h in heads]
    vc = [slice(h * GLA_DV, (h + 1) * GLA_DV) for h in heads]
    intra = []
    for c in range(chunks_per_step):
        rows = slice(c * chunk, (c + 1) * chunk)
        g = g_ref[rows, :]
        k = k_ref[rows, :]
        if t_valid < chunk:
            valid = lax.broadcasted_iota(jnp.int32, (chunk, GLA_DK_W), 0) < t_valid
            g = jnp.where(valid, g, 0.0)
            k = jnp.where(valid, k, 0.0)
        g_hi = g.astype(bf16)
        g_lo = (g - g_hi.astype(f32)).astype(bf16)
        b = (jnp.dot(tri, g_hi, preferred_element_type=f32)
             + jnp.dot(tri, g_lo, preferred_element_type=f32))
        b_last = b[chunk - 1:chunk, :]
        qe = (q_ref[rows, :] * jnp.exp(b)).astype(bf16)
        ke = (k * jnp.exp(-b)).astype(bf16)
        kd = (k * jnp.exp(b_last - b)).astype(bf16)
        vb = v_ref[rows, :].astype(bf16)
        decay = jnp.broadcast_to(jnp.exp(b_last), (SUBLANES, GLA_DK_W)).T[:, :1]
        a = [lax.dot_general(qe[:, kc[h]], ke[:, kc[h]], _NT, preferred_element_type=f32)
             for h in heads]
        a = [jnp.where(causal, a[h], 0.0).astype(bf16) for h in heads]
        o_intra = [jnp.dot(a[h], vb[:, vc[h]], preferred_element_type=f32) for h in heads]
        intra.append((rows, qe, kd, vb, decay, o_intra))
    st = [st_ref[h] for h in heads]
    for rows, qe, kd, vb, decay, o_intra in intra:
        o = [o_intra[h] + jnp.dot(qe[:, kc[h]], st[h].astype(bf16), preferred_element_type=f32)
             for h in heads]
        st = [decay[kc[h]] * st[h] + lax.dot_general(kd[:, kc[h]], vb[:, vc[h]], _TN,
                                                     preferred_element_type=f32) for h in heads]
        for h in heads:
            o_ref[rows, vc[h]] = _head_epilogue(o[h], ng_ref[...], z_ref[rows, vc[h]],
                                                1.0).astype(o_ref.dtype)
    for h in heads:
        st_ref[h] = st[h]

    @pl.when(n == n_steps - 1)
    def _():
        for h in range(GLA_HEADS):
            so_ref[h] = st_ref[h]


def _gla(q, k, v, g, z, norm_g, s0, layer, n_layers, state_out, batch, seq, t_valid):
    chunk = min(GLA_CHUNK, seq)
    per_step = GLA_CHUNKS_PER_STEP if seq % (GLA_CHUNKS_PER_STEP * chunk) == 0 else 1
    rows = per_step * chunk
    nc = seq // rows
    has_state = s0 is not None
    out_dtype = bf16 if chunk % (2 * SUBLANES) == 0 else f32
    tmap = lambda b, n: (b * nc + n, 0)
    state_spec = pl.BlockSpec((None, None, GLA_HEADS, GLA_DK, GLA_DV),
                              lambda b, n: (layer, b, 0, 0, 0))
    in_specs = [pl.BlockSpec((rows, GLA_DK_W), tmap),
                pl.BlockSpec((rows, GLA_DK_W), tmap),
                pl.BlockSpec((rows, D_MODEL), tmap),
                pl.BlockSpec((rows, GLA_DK_W), tmap),
                pl.BlockSpec((rows, D_MODEL), tmap),
                pl.BlockSpec((1, GLA_DV), lambda b, n: (0, 0))]
    args = [q, k, v, g, z, norm_g.reshape(1, GLA_DV)]
    if has_state:
        in_specs.append(state_spec)
        args.append(s0)
    aliases = {}
    if state_out is not None:
        aliases = {len(args): 1}
        in_specs.append(pl.BlockSpec(memory_space=pl.ANY))
        args.append(state_out)
    return pl.pallas_call(
        functools.partial(_gla_kernel, chunk=chunk, chunks_per_step=per_step, t_valid=t_valid,
                          n_steps=nc, has_state=has_state),
        grid=(batch, nc),
        in_specs=in_specs,
        out_specs=[pl.BlockSpec((rows, D_MODEL), tmap), state_spec],
        out_shape=[jax.ShapeDtypeStruct((batch * seq, D_MODEL), out_dtype),
                   jax.ShapeDtypeStruct((n_layers, batch, GLA_HEADS, GLA_DK, GLA_DV), f32)],
        input_output_aliases=aliases,
        scratch_shapes=[pltpu.VMEM((GLA_HEADS, GLA_DK, GLA_DV), f32)],
        compiler_params=_params("parallel", "arbitrary"),
        name="gla_chunked",
    )(*args)


def _trunk(xp, xs, batch, seq, dec_batch, dec_seq, cache_k, cache_v, state_gla, page_table, weights):
    (rel_bias, diff_w_in, diff_lambda, diff_norm_g, diff_w_out, gla_w_in, gla_w_g1, gla_w_g2,
     gla_b_g, gla_norm_g, gla_w_out, ln_g, ln_b) = weights
    n_diff = (DEPTH + 1) // 2
    xpb = xp.astype(bf16)
    xsb = xs.astype(bf16)
    kbuf = vbuf = None
    n_gla = DEPTH // 2
    states_p = states_s = None
    far = _bias_by_distance(rel_bias, np.array([2 * ATT_BLOCK]))[:, 0]
    tiles = (_prompt_bias_tiles(rel_bias, ATT_BLOCK) - far[:, None, None]) * LOG2E
    bm, bm_new = [t * LOG2E for t in _decode_bias_tables(rel_bias, dec_seq)]
    q_scale = DIFF_DH ** -0.5 * LOG2E
    for i in range(DEPTH):
        j = i // 2
        if i % 2 == 0:
            w = diff_w_in
            qp, qs = _matmul(xpb, xsb, w, j, 0, D_MODEL, bf16, scale=q_scale)
            zp, zs = _matmul(xpb, xsb, w, j, 3 * D_MODEL, D_MODEL, f32, after=(qp,))
            kbuf = _matmul(xpb, xsb, w, j, D_MODEL, D_MODEL, f32, slots=n_diff, slot=j, into=kbuf,
                           after=(zp,))
            vbuf = _matmul(xpb, xsb, w, j, 2 * D_MODEL, D_MODEL, f32, slots=n_diff, slot=j,
                           into=vbuf, after=(kbuf[0],))
            lam_init = 0.8 - 0.6 * math.exp(-0.3 * i)
            os_, op = _diff_attention(
                page_table,
                (cache_k, cache_v, j, qs, kbuf[1][j], vbuf[1][j], zs, bm, bm_new, diff_lambda[j],
                 diff_norm_g[j]),
                (qp, kbuf[0], vbuf[0], j, zp, tiles, diff_lambda[j], diff_norm_g[j]),
                batch, seq, lam_init)
            if j == n_diff - 1:
                k_held, v_held = _hold_until([kbuf[0], vbuf[0]], os_)
                kbuf, vbuf = (k_held, kbuf[1]), (v_held, vbuf[1])
            w_out = diff_w_out[j]
        else:
            w = gla_w_in
            proj = functools.partial(_matmul, xpb, xsb, w, j, w_transposed=True)
            qp, qs = proj(0, GLA_DK_W, f32, scale=GLA_DK ** -0.5)
            kp, ks = proj(GLA_DK_W, GLA_DK_W, f32)
            vp, vs = proj(2 * GLA_DK_W, D_MODEL, f32)
            zp, zs = proj(2 * GLA_DK_W + D_MODEL, D_MODEL, f32)
            gp = _gla_gate(xpb, gla_w_g1[j], gla_w_g2[j], gla_b_g[j])
            gs = _gla_gate(xsb, gla_w_g1[j], gla_w_g2[j], gla_b_g[j])
            op, states_p = _gla(qp, kp, vp, gp, zp, gla_norm_g[j], None, j, n_gla, states_p,
                                batch, seq, seq)
            os_, states_s = _gla(qs, ks, vs, gs, zs, gla_norm_g[j], state_gla, j, n_gla, states_s,
                                 dec_batch, SAMPLE_ROWS, dec_seq)
            w_out = gla_w_out[j]
        xp, xpb = _out_proj_norm(op, w_out, xp, ln_g[i], ln_b[i], seq=seq)
        xs, xsb = _out_proj_norm(os_, w_out, xs, ln_g[i], ln_b[i])
    return xp, xs, kbuf, vbuf, states_p, states_s


def kernel(x_prompt, x_sample, cache_k, cache_v, state_gla, page_table, rel_bias, diff_w_in,
           diff_lambda, diff_norm_g, diff_w_out, gla_w_in, gla_w_g2, gla_b_g, gla_norm_g, gla_w_out,
           ln_g, ln_b):
    batch, seq, _ = x_prompt.shape
    dec_batch, dec_seq, _ = x_sample.shape
    n_pages = page_table.shape[1]
    assert n_pages >= 2 and n_pages % DEC_PAGES_PER_STEP == 0 and dec_seq <= SAMPLE_ROWS
    assert cache_k.shape[2:] == (PAGE_SIZE, DIFF_HEADS, DIFF_HW)
    assert seq % (4 * ATT_BLOCK) == 0 and seq % GLA_CHUNK == 0

    gate_off = 2 * GLA_DK_W + 2 * D_MODEL
    gla_w_t = jnp.swapaxes(gla_w_in, 1, 2)
    w_g1 = jnp.pad(gla_w_t[:, gate_off:, :], ((0, 0), (0, LANES - GLA_GATE_RANK), (0, 0)))
    w_g2 = jnp.pad(gla_w_g2, ((0, 0), (0, LANES - GLA_GATE_RANK), (0, 0)))
    weights = (rel_bias, diff_w_in, diff_lambda, diff_norm_g, diff_w_out.astype(bf16),
               gla_w_t, w_g1.astype(bf16), w_g2.astype(bf16), gla_b_g,
               gla_norm_g,
               gla_w_out.astype(bf16), ln_g, ln_b)

    rows = SAMPLE_ROWS
    xs = jnp.pad(x_sample, ((0, 0), (0, rows - dec_seq), (0, 0))).reshape(dec_batch * rows, D_MODEL)
    y_p, y_s, kbuf, vbuf, s_p, s_s = _trunk(
        x_prompt.reshape(batch * seq, D_MODEL), xs, batch, seq, dec_batch, dec_seq,
        cache_k, cache_v, state_gla, page_table, weights)

    def rows_out(a, b, t, t_keep):
        return a.reshape(a.shape[0], b, t, DIFF_HEADS, DIFF_HW)[:, :, :t_keep]

    return (y_p.reshape(batch, seq, D_MODEL),
            y_s.reshape(dec_batch, rows, D_MODEL)[:, :dec_seq],
            rows_out(kbuf[0], batch, seq, seq), rows_out(vbuf[0], batch, seq, seq), s_p,
            rows_out(kbuf[1], dec_batch, rows, dec_seq), rows_out(vbuf[1], dec_batch, rows, dec_seq),
            s_s)
```

```python
import functools
import math

import jax
import jax.numpy as jnp
import numpy as np
from jax import lax
from jax.experimental import pallas as pl
from jax.experimental.pallas import tpu as pltpu

D_MODEL = 2048
DEPTH = 4
PAGE_SIZE = 128

DIFF_HEADS = 8
DIFF_DH = D_MODEL // (2 * DIFF_HEADS)
DIFF_HW = 2 * DIFF_DH

GLA_HEADS = 4
GLA_DK_W = D_MODEL // 2
GLA_DK = GLA_DK_W // GLA_HEADS
GLA_DV = D_MODEL // GLA_HEADS
GLA_GATE_RANK = 16
GLA_GATE_NORMALIZER = 16.0
GLA_CHUNK = 64
GLA_CHUNKS_PER_STEP = 2

REL_BUCKETS = 32
REL_MAX_DIST = 128

NORM_EPS = 1e-5
DEEPNORM_ALPHA = (2 * DEPTH) ** 0.25

LANES = 128
SUBLANES = 8
VMEM_LIMIT = 48 * 1024 * 1024
MASK_VALUE = -1e30
LOG2E = math.log2(math.e)

ATT_BLOCK = 256
OUT_ROW_GROUPS = 4
DEC_PAGES_PER_STEP = 8
DEC_GROUPS = 4
DEC_VMEM_LIMIT = 56 * 1024 * 1024
FUSED_VMEM_LIMIT = 60 * 1024 * 1024
SAMPLE_ROWS = SUBLANES
PAGE_ROWS = PAGE_SIZE * DIFF_HEADS
NEW_ROWS = 2 * SAMPLE_ROWS * DIFF_HEADS
DEC_COLS = 2 * DIFF_HEADS * SAMPLE_ROWS
assert DEC_COLS == LANES and NEW_ROWS == LANES

_NT = (((1,), (1,)), ((), ()))
_TN = (((0,), (0,)), ((), ()))

bf16 = jnp.bfloat16
f32 = jnp.float32


def _params(*sem):
    return pltpu.CompilerParams(dimension_semantics=sem, vmem_limit_bytes=VMEM_LIMIT)


def _silu(z):
    return z * (1.0 / (1.0 + jnp.exp(-z)))


def _mm_kernel(xp_ref, xs_ref, w_ref, *rest, scale, w_transposed):
    op_ref, os_ref, wb_ref = rest[-3:]
    dims = _NT if w_transposed else (((1,), (0,)), ((), ()))

    def project(x_ref, o_ref):
        acc = lax.dot_general(x_ref[...], wb_ref[...], dims, preferred_element_type=f32)
        if scale != 1.0:
            acc = acc * scale
        o_ref[...] = acc.astype(o_ref.dtype)

    @pl.when(pl.program_id(1) == 0)
    def _():
        wb_ref[...] = w_ref[...].astype(bf16)
        project(xs_ref, os_ref)

    project(xp_ref, op_ref)


def _matmul(xp, xs, w, layer, col_off, n, out_dtype, scale=1.0, slots=None, slot=0, into=None,
            w_transposed=False, after=()):
    m, k = xp.shape
    ms = xs.shape[0]
    tm = min(m, 1024)
    tn = min(n, 1024)
    assert m % tm == 0 and n % tn == 0 and col_off % tn == 0
    off = col_off // tn
    if w_transposed:
        w_spec = pl.BlockSpec((None, tn, k), lambda j, i: (layer, j + off, 0))
        w_tile = (tn, k)
    else:
        w_spec = pl.BlockSpec((None, k, tn), lambda j, i: (layer, 0, j + off))
        w_tile = (k, tn)
    in_specs = [pl.BlockSpec((tm, k), lambda j, i: (i, 0)),
                pl.BlockSpec((ms, k), lambda j, i: (0, 0)),
                w_spec]
    args = [xp, xs, w]
    aliases = {}
    if slots is None:
        out_specs = [pl.BlockSpec((tm, tn), lambda j, i: (i, j)),
                     pl.BlockSpec((ms, tn), lambda j, i: (0, j))]
        out_shape = [jax.ShapeDtypeStruct((m, n), out_dtype),
                     jax.ShapeDtypeStruct((ms, n), out_dtype)]
    else:
        out_specs = [pl.BlockSpec((None, tm, tn), lambda j, i: (slot, i, j)),
                     pl.BlockSpec((None, ms, tn), lambda j, i: (slot, 0, j))]
        out_shape = [jax.ShapeDtypeStruct((slots, m, n), out_dtype),
                     jax.ShapeDtypeStruct((slots, ms, n), out_dtype)]
        if into is not None:
            in_specs += [pl.BlockSpec(memory_space=pl.ANY)] * 2
            args += list(into)
            aliases = {3: 0, 4: 1}
    in_specs += [pl.BlockSpec(memory_space=pl.ANY)] * len(after)
    args += list(after)
    return pl.pallas_call(
        functools.partial(_mm_kernel, scale=scale, w_transposed=w_transposed),
        grid=(n // tn, m // tm),
        in_specs=in_specs,
        out_specs=out_specs,
        out_shape=out_shape,
        input_output_aliases=aliases,
        scratch_shapes=[pltpu.VMEM(w_tile, bf16)],
        compiler_params=_params("parallel", "arbitrary"),
        name="proj_matmul",
    )(*args)


def _hold_kernel(*refs):
    del refs


def _hold_until(bufs, anchor):
    any_spec = pl.BlockSpec(memory_space=pl.ANY)
    return pl.pallas_call(
        _hold_kernel,
        in_specs=[any_spec] * (len(bufs) + 1),
        out_specs=[any_spec] * len(bufs),
        out_shape=[jax.ShapeDtypeStruct(b.shape, b.dtype) for b in bufs],
        input_output_aliases={i: i for i in range(len(bufs))},
        name="hold_row_buffers",
    )(*bufs, anchor)


def _out_kernel(*refs, tiles_per_batch):
    if tiles_per_batch is None:
        o_ref, w_ref, x_ref, g_ref, b_ref, xo_ref, xb_ref = refs
        o = o_ref[...].astype(bf16)
    else:
        olo_ref, ohi_ref, w_ref, x_ref, g_ref, b_ref, xo_ref, xb_ref = refs
        first_half = (pl.program_id(0) % tiles_per_batch) < tiles_per_batch // 2
        o = jnp.where(first_half, olo_ref[...], ohi_ref[...])
    tm = o.shape[0]
    group = tm // OUT_ROW_GROUPS if tm >= OUT_ROW_GROUPS * LANES else tm
    for r0 in range(0, tm, group):
        rows = slice(r0, r0 + group)
        y = jnp.dot(o[rows], w_ref[...], preferred_element_type=f32)
        r = DEEPNORM_ALPHA * x_ref[rows, :] + y
        mu = jnp.mean(r, axis=-1, keepdims=True)
        d = r - mu
        var = jnp.mean(d * d, axis=-1, keepdims=True)
        xn = d * lax.rsqrt(var + NORM_EPS) * g_ref[...] + b_ref[...]
        xo_ref[rows, :] = xn
        xb_ref[rows, :] = xn.astype(bf16)


def _out_proj_norm(o, w, x, g, b, seq=None):
    m = x.shape[0]
    tm = min(m, 512)
    row = lambda i: (i, 0)
    fixed = lambda i: (0, 0)
    if isinstance(o, (tuple, list)):
        tpb = seq // tm
        half = tpb // 2
        assert seq % tm == 0 and tpb % 2 == 0
        o_specs = [pl.BlockSpec((tm, D_MODEL),
                                lambda i: ((i // tpb) * half + jnp.minimum(i % tpb, half - 1), 0)),
                   pl.BlockSpec((tm, D_MODEL),
                                lambda i: ((i // tpb) * half + jnp.maximum(i % tpb - half, 0), 0))]
        o_args = list(o)
    else:
        tpb = None
        o_specs = [pl.BlockSpec((tm, D_MODEL), row)]
        o_args = [o]
    return pl.pallas_call(
        functools.partial(_out_kernel, tiles_per_batch=tpb),
        grid=(m // tm,),
        in_specs=o_specs + [
                  pl.BlockSpec((D_MODEL, D_MODEL), fixed),
                  pl.BlockSpec((tm, D_MODEL), row),
                  pl.BlockSpec((1, D_MODEL), fixed),
                  pl.BlockSpec((1, D_MODEL), fixed)],
        out_specs=[pl.BlockSpec((tm, D_MODEL), row), pl.BlockSpec((tm, D_MODEL), row)],
        out_shape=[jax.ShapeDtypeStruct((m, D_MODEL), f32),
                   jax.ShapeDtypeStruct((m, D_MODEL), bf16)],
        compiler_params=_params("parallel"),
        name="out_proj_deepnorm",
    )(*o_args, w, x, g.reshape(1, D_MODEL), b.reshape(1, D_MODEL))


def _bias_by_distance(rel_bias, dist):
    n = jnp.asarray(dist, jnp.int32)
    max_exact = REL_BUCKETS // 2
    nf = jnp.maximum(n, 1).astype(f32)
    large = max_exact + (jnp.log(nf / max_exact) / math.log(REL_MAX_DIST / max_exact)
                         * (REL_BUCKETS - max_exact)).astype(jnp.int32)
    large = jnp.minimum(large, REL_BUCKETS - 1)
    bucket = jnp.where(n < max_exact, n, large)
    return jnp.moveaxis(rel_bias.astype(f32)[bucket], -1, 0)


def _prompt_bias_tiles(rel_bias, blk):
    k = np.arange(3 * blk - 1)
    d = 2 * blk - 1 - k
    u = jnp.where(jnp.asarray(d >= 0), _bias_by_distance(rel_bias, np.maximum(d, 0)), MASK_VALUE)
    period = 3 * blk
    flat = jnp.tile(jnp.pad(u, ((0, 0), (0, 1))), (1, blk))[:, :blk * (period - 1)]
    return flat.reshape(DIFF_HEADS, blk, period - 1)[:, :, blk - 1:3 * blk - 1]


def _decode_bias_tables(rel_bias, t_valid):
    col = np.arange(DEC_COLS)
    col_h = (col // SAMPLE_ROWS) % DIFF_HEADS
    reps = DEC_COLS // SAMPLE_ROWS
    tq = np.arange(SAMPLE_ROWS)[None, :]
    new_tokens = NEW_ROWS // DIFF_HEADS
    tab = _bias_by_distance(rel_bias, np.arange(2 * PAGE_SIZE + SAMPLE_ROWS + 1))

    def toeplitz_rows(ext, n_tokens):
        wins = [ext[:, t + 1:t + 1 + n_tokens][:, ::-1] for t in range(SAMPLE_ROWS)]
        return jnp.transpose(jnp.stack(wins), (2, 1, 0)).reshape(n_tokens * DIFF_HEADS, SAMPLE_ROWS)

    def expand(vals, visible):
        n_rows = vals.shape[0]
        r = np.arange(n_rows)
        ok = ((r % DIFF_HEADS)[:, None] == col_h[None, :]) & np.tile(visible, (1, reps))
        return jnp.where(jnp.asarray(ok), jnp.tile(vals, (1, reps)), MASK_VALUE)

    everything = np.ones((PAGE_ROWS, SAMPLE_ROWS), bool)
    far_vals = jnp.broadcast_to(jnp.tile(tab[:, 2 * PAGE_SIZE], PAGE_SIZE)[:, None],
                                (PAGE_ROWS, SAMPLE_ROWS))
    far = expand(far_vals, everything)
    last = expand(toeplitz_rows(tab, PAGE_SIZE), everything)
    s_new = (np.arange(NEW_ROWS) // DIFF_HEADS)[:, None]
    ext = jnp.pad(tab[:, :SAMPLE_ROWS], ((0, 0), (new_tokens, 0)))
    new = expand(toeplitz_rows(ext, new_tokens), (s_new <= tq) & (s_new < t_valid))
    return jnp.stack([far, last]), new


def _lambda_value(lp_ref, lam_init):
    lp = lp_ref[...]
    a = jnp.sum(lp[0:1] * lp[1:2], axis=-1, keepdims=True)
    b = jnp.sum(lp[2:3] * lp[3:4], axis=-1, keepdims=True)
    return jnp.exp(a) - jnp.exp(b) + lam_init


def _head_epilogue(o, g, z, out_scale):
    ms = jnp.mean(o * o, axis=-1, keepdims=True)
    return (o * lax.rsqrt(ms + NORM_EPS) * g * out_scale) * _silu(z)


def _attn_step(i, qlo_ref, qhi_ref, k_ref, v_ref, zlo_ref, zhi_ref, bias_ref, lp_ref, g_ref,
               olo_ref, ohi_ref, kb_ref, vb_ref, s_ref, *, lam_init, n_blocks, between=()):
    blk = ATT_BLOCK

    @pl.when(i == 0)
    def _():
        kb_ref[...] = k_ref[...].astype(bf16)
        vb_ref[...] = v_ref[...].astype(bf16)

    lam = _lambda_value(lp_ref, lam_init)

    def lane_fold(x):
        return [x[:, c * LANES:(c + 1) * LANES] for c in range(blk // LANES)]

    def logits_and_max(q_ref, nvis, base):
        q = q_ref[...]
        qs = (q[:, :DIFF_DH], q[:, DIFF_DH:])
        mx = [None, None]
        for j in range(nvis):
            kj = kb_ref[j * blk:(j + 1) * blk, :]
            if j == nvis - 1:
                bias = bias_ref[:, blk:]
            elif j == nvis - 2:
                bias = bias_ref[:, :blk]
            else:
                bias = None
            for mp in range(2):
                s = lax.dot_general(qs[mp], kj[:, mp * DIFF_DH:(mp + 1) * DIFF_DH], _NT,
                                    preferred_element_type=f32)
                if bias is not None:
                    s = s + bias
                s_ref[mp, base + j] = s
                for part in lane_fold(s):
                    mx[mp] = part if mx[mp] is None else jnp.maximum(mx[mp], part)
        return [jnp.max(mx[mp], axis=-1, keepdims=True) for mp in range(2)]

    def exponentials(nvis, base, m):
        ls = [None, None]
        for j in range(nvis):
            for mp in range(2):
                p = jnp.exp2(s_ref[mp, base + j] - m[mp])
                s_ref[mp, base + j] = p
                for part in lane_fold(p):
                    ls[mp] = part if ls[mp] is None else ls[mp] + part
        return (1.0 / jnp.sum(ls[0], axis=-1, keepdims=True),
                lam / jnp.sum(ls[1], axis=-1, keepdims=True))

    def weighted_values(nvis, base, c, z_ref, o_ref):
        acc = None
        for j in range(nvis):
            pd = (s_ref[0, base + j] * c[0] - s_ref[1, base + j] * c[1]).astype(bf16)
            pv = jnp.dot(pd, vb_ref[j * blk:(j + 1) * blk, :], preferred_element_type=f32)
            acc = pv if acc is None else acc + pv
        o_ref[...] = _head_epilogue(acc, g_ref[...], z_ref[...], 1.0 - lam_init).astype(bf16)

    def block_pair(lo):
        pending = list(between)

        def other_work():
            if pending:
                pending.pop(0)()

        n_lo, n_hi = lo + 1, n_blocks - lo
        other_work()
        m_lo = logits_and_max(qlo_ref, n_lo, 0)
        other_work()
        m_hi = logits_and_max(qhi_ref, n_hi, n_lo)
        other_work()
        c_lo = exponentials(n_lo, 0, m_lo)
        other_work()
        c_hi = exponentials(n_hi, n_lo, m_hi)
        while pending:
            other_work()
        weighted_values(n_lo, 0, c_lo, zlo_ref, olo_ref)
        weighted_values(n_hi, n_lo, c_hi, zhi_ref, ohi_ref)

    for lo in range(n_blocks // 2):
        pl.when(i == lo)(functools.partial(block_pair, lo))


def _attn_kernel(*refs, **static):
    _attn_step(pl.program_id(2), *refs, **static)


def _prompt_specs(q, k, v, slot, z, bias_tiles, lam_p, norm_g, batch, seq, coords):
    nq = seq // ATT_BLOCK
    half = nq // 2
    blk = ATT_BLOCK

    def at(f):
        return lambda *g: f(*coords(*g))

    lo_in = at(lambda b, h, i: (b * nq + i, h))
    hi_in = at(lambda b, h, i: (b * nq + nq - 1 - i, h))
    lo_out = at(lambda b, h, i: (b * half + i, h))
    hi_out = at(lambda b, h, i: (b * half + half - 1 - i, h))
    kvmap = at(lambda b, h, i: (slot, b, h))
    out = jax.ShapeDtypeStruct((batch * seq // 2, D_MODEL), bf16)
    args = [q, q, k, v, z, z, bias_tiles, lam_p, norm_g.reshape(1, DIFF_HW)]
    in_specs = [pl.BlockSpec((blk, DIFF_HW), lo_in),
                pl.BlockSpec((blk, DIFF_HW), hi_in),
                pl.BlockSpec((None, seq, DIFF_HW), kvmap),
                pl.BlockSpec((None, seq, DIFF_HW), kvmap),
                pl.BlockSpec((blk, DIFF_HW), lo_in),
                pl.BlockSpec((blk, DIFF_HW), hi_in),
                pl.BlockSpec((None, blk, 2 * blk), at(lambda b, h, i: (h, 0, 0))),
                pl.BlockSpec((4, DIFF_DH), at(lambda b, h, i: (0, 0))),
                pl.BlockSpec((1, DIFF_HW), at(lambda b, h, i: (0, 0)))]
    out_specs = [pl.BlockSpec((blk, DIFF_HW), lo_out), pl.BlockSpec((blk, DIFF_HW), hi_out)]
    scratch = [pltpu.VMEM((seq, DIFF_HW), bf16),
               pltpu.VMEM((seq, DIFF_HW), bf16),
               pltpu.VMEM((2, nq + 1, blk, blk), f32)]
    return args, in_specs, out_specs, [out, out], scratch


def _prompt_attention(q, k, v, slot, z, bias_tiles, lam_p, norm_g, batch, seq, lam_init):
    nq = seq // ATT_BLOCK
    args, in_specs, out_specs, out_shape, scratch = _prompt_specs(
        q, k, v, slot, z, bias_tiles, lam_p, norm_g, batch, seq, lambda b, h, i: (b, h, i))
    return pl.pallas_call(
        functools.partial(_attn_kernel, lam_init=lam_init, n_blocks=nq),
        grid=(batch, DIFF_HEADS, nq // 2),
        in_specs=in_specs,
        out_specs=out_specs,
        out_shape=out_shape,
        scratch_shapes=scratch,
        compiler_params=_params("parallel", "parallel", "arbitrary"),
        name="prompt_diff_attention",
    )(*args)


DEC_INPUTS = 2 * DEC_PAGES_PER_STEP + 8


def _decode_kernel(pt_ref, *refs, **static):
    del pt_ref
    _decode_step(pl.program_id(1), refs, **static)


def _decode_step(s, refs, *, lam_init, n_steps, init=True, groups=range(DEC_GROUPS), finish=True):
    npg = DEC_PAGES_PER_STEP
    k_pages = refs[:npg]
    v_pages = refs[npg:2 * npg]
    (w_ref, kn_ref, vn_ref, z_ref, bm_ref, bmn_ref, lp_ref, g_ref, o_ref,
     m_ref, l_ref, acc_ref) = refs[2 * npg:]

    if init:
        @pl.when(s == 0)
        def _():
            m_ref[...] = jnp.full(m_ref.shape, MASK_VALUE, f32)
            l_ref[...] = jnp.zeros(l_ref.shape, f32)
            acc_ref[...] = jnp.zeros(acc_ref.shape, f32)

    w = w_ref[...]
    eye = (lax.broadcasted_iota(jnp.int32, (DEC_COLS, DEC_COLS), 0)
           == lax.broadcasted_iota(jnp.int32, (DEC_COLS, DEC_COLS), 1))

    def to_column(row):
        return jnp.sum(jnp.where(eye, row, 0.0), axis=1, keepdims=True)

    def update(grp, blocks):
        logits = [jnp.dot(k, w, preferred_element_type=f32) + bm for k, _, bm in blocks]
        cmax = None
        for lg in logits:
            c = jnp.max(lg, axis=0, keepdims=True)
            cmax = c if cmax is None else jnp.maximum(cmax, c)
        m_old = m_ref[grp]
        m_new = jnp.maximum(m_old, cmax)
        a = jnp.exp2(m_old - m_new)
        lsum = None
        pv = None
        for lg, (_, v, _) in zip(logits, blocks):
            p = jnp.exp2(lg - m_new)
            ps = jnp.sum(p, axis=0, keepdims=True)
            lsum = ps if lsum is None else lsum + ps
            d = jnp.dot(p.T.astype(bf16), v.astype(bf16), preferred_element_type=f32)
            pv = d if pv is None else pv + d
        m_ref[grp] = m_new
        l_ref[grp] = a * l_ref[grp] + lsum
        acc_ref[grp] = acc_ref[grp] * to_column(a) + pv

    per_group = npg // DEC_GROUPS
    for grp in groups:
        blocks = []
        for pg in range(grp * per_group, (grp + 1) * per_group):
            if pg == npg - 1:
                bm = bm_ref[jnp.where(s == n_steps - 1, 1, 0)]
            else:
                bm = bm_ref[0]
            blocks.append((k_pages[pg][...], v_pages[pg][...], bm))
        update(grp, blocks)

    if not finish:
        return

    @pl.when(s == n_steps - 1)
    def _():
        update(0, [(kn_ref[...], vn_ref[...], bmn_ref[...])])
        m_all = m_ref[0]
        for grp in range(1, DEC_GROUPS):
            m_all = jnp.maximum(m_all, m_ref[grp])
        l_all = None
        acc_all = None
        for grp in range(DEC_GROUPS):
            a = jnp.exp2(m_ref[grp] - m_all)
            lg = a * l_ref[grp]
            ag = acc_ref[grp] * to_column(a)
            l_all = lg if l_all is None else l_all + lg
            acc_all = ag if acc_all is None else acc_all + ag
        on = acc_all * (1.0 / to_column(l_all))
        lam = _lambda_value(lp_ref, lam_init)
        half = DEC_COLS // 2
        for h in range(DIFF_HEADS):
            r0 = h * SAMPLE_ROWS
            o = on[r0:r0 + SAMPLE_ROWS] - lam * on[half + r0:half + r0 + SAMPLE_ROWS]
            lo = h * DIFF_HW
            o_ref[:, lo:lo + DIFF_HW] = _head_epilogue(o, g_ref[...], z_ref[:, lo:lo + DIFF_HW],
                                                       1.0 - lam_init)


def _decode_specs(page_table, cache_k, cache_v, layer, q, k_new, v_new, z, bm, bm_new, lam_p, norm_g):
    nb, n_pages = page_table.shape
    npg = DEC_PAGES_PER_STEP
    ck = cache_k.reshape(cache_k.shape[0], cache_k.shape[1], PAGE_ROWS, DIFF_HW)
    cv = cache_v.reshape(cache_v.shape[0], cache_v.shape[1], PAGE_ROWS, DIFF_HW)

    q5 = q.reshape(nb, SAMPLE_ROWS, DIFF_HEADS, 2, DIFF_DH)
    qt = jnp.transpose(q5, (0, 3, 4, 2, 1)).reshape(nb, 2, DIFF_DH, DEC_COLS // 2)
    zeros = jnp.zeros_like(qt[:, 0])
    w = jnp.concatenate([jnp.concatenate([qt[:, 0], zeros], axis=-1),
                         jnp.concatenate([zeros, qt[:, 1]], axis=-1)], axis=1)

    def new_rows(a):
        a = a.reshape(nb, SAMPLE_ROWS, DIFF_HEADS, DIFF_HW)
        a = jnp.pad(a, ((0, 0), (0, NEW_ROWS // DIFF_HEADS - SAMPLE_ROWS), (0, 0), (0, 0)))
        return a.reshape(nb, NEW_ROWS, DIFF_HW)

    def page_spec(pg):
        return pl.BlockSpec((None, None, PAGE_ROWS, DIFF_HW),
                            lambda b, s, pt: (layer, pt[b, s * npg + pg], 0, 0))

    per_batch3 = lambda b, s, pt: (b, 0, 0)
    row = lambda b, s, pt: (b, 0)
    fixed2 = lambda b, s, pt: (0, 0)
    in_specs = ([page_spec(pg) for pg in range(npg)] + [page_spec(pg) for pg in range(npg)]
                + [pl.BlockSpec((None, DIFF_HW, DEC_COLS), per_batch3),
                   pl.BlockSpec((None, NEW_ROWS, DIFF_HW), per_batch3),
                   pl.BlockSpec((None, NEW_ROWS, DIFF_HW), per_batch3),
                   pl.BlockSpec((SAMPLE_ROWS, D_MODEL), row),
                   pl.BlockSpec((2, PAGE_ROWS, DEC_COLS), lambda b, s, pt: (0, 0, 0)),
                   pl.BlockSpec((NEW_ROWS, DEC_COLS), fixed2),
                   pl.BlockSpec((4, DIFF_DH), fixed2),
                   pl.BlockSpec((1, DIFF_HW), fixed2)])
    assert len(in_specs) == DEC_INPUTS
    args = [*([ck] * npg), *([cv] * npg), w, new_rows(k_new), new_rows(v_new), z, bm, bm_new,
            lam_p, norm_g.reshape(1, DIFF_HW)]
    out_specs = [pl.BlockSpec((SAMPLE_ROWS, D_MODEL), row)]
    out_shape = [jax.ShapeDtypeStruct((nb * SAMPLE_ROWS, D_MODEL), f32)]
    scratch = [pltpu.VMEM((DEC_GROUPS, 1, DEC_COLS), f32),
               pltpu.VMEM((DEC_GROUPS, 1, DEC_COLS), f32),
               pltpu.VMEM((DEC_GROUPS, DEC_COLS, DIFF_HW), f32)]
    return args, in_specs, out_specs, out_shape, scratch


def _decode_attention(page_table, *operands, lam_init):
    nb, n_pages = page_table.shape
    n_steps = n_pages // DEC_PAGES_PER_STEP
    args, in_specs, out_specs, out_shape, scratch = _decode_specs(page_table, *operands)
    return pl.pallas_call(
        functools.partial(_decode_kernel, lam_init=lam_init, n_steps=n_steps),
        grid_spec=pltpu.PrefetchScalarGridSpec(
            num_scalar_prefetch=1, grid=(nb, n_steps), in_specs=in_specs, out_specs=out_specs,
            scratch_shapes=scratch),
        out_shape=out_shape,
        compiler_params=pltpu.CompilerParams(dimension_semantics=("parallel", "arbitrary"),
                                             vmem_limit_bytes=DEC_VMEM_LIMIT),
        name="decode_diff_attention",
    )(page_table, *args)[0]


def _fused_attn_kernel(pt_ref, *refs, lam_init, n_steps, n_blocks):
    del pt_ref
    n_att_in = 9
    dec_in = refs[:DEC_INPUTS]
    att_in = refs[DEC_INPUTS:DEC_INPUTS + n_att_in]
    o_dec, o_lo, o_hi = refs[DEC_INPUTS + n_att_in:DEC_INPUTS + n_att_in + 3]
    m_ref, l_ref, acc_ref, kb_ref, vb_ref, s_ref = refs[DEC_INPUTS + n_att_in + 3:]
    s = pl.program_id(1)
    step = pl.program_id(0) * n_steps + s
    decode = functools.partial(_decode_step, s, (*dec_in, o_dec, m_ref, l_ref, acc_ref),
                               lam_init=lam_init, n_steps=n_steps)
    decode(init=True, groups=(), finish=False)
    _attn_step(step % (n_blocks // 2), *att_in, o_lo, o_hi, kb_ref, vb_ref, s_ref,
               lam_init=lam_init, n_blocks=n_blocks,
               between=[functools.partial(decode, init=False, groups=(grp,), finish=False)
                        for grp in range(DEC_GROUPS)])
    decode(init=False, groups=(), finish=True)


def _diff_attention(page_table, decode_operands, prompt_operands, batch, seq, lam_init):
    nb, n_pages = page_table.shape
    n_steps = n_pages // DEC_PAGES_PER_STEP
    nq = seq // ATT_BLOCK
    half = nq // 2
    if nb * n_steps != batch * DIFF_HEADS * half:
        o_s = _decode_attention(page_table, *decode_operands, lam_init=lam_init)
        return o_s, _prompt_attention(*prompt_operands, batch, seq, lam_init)

    def coords(b, s, pt):
        step = b * n_steps + s
        return step // (DIFF_HEADS * half), (step // half) % DIFF_HEADS, step % half

    d_args, d_in, d_out, d_shape, d_scratch = _decode_specs(page_table, *decode_operands)
    p_args, p_in, p_out, p_shape, p_scratch = _prompt_specs(*prompt_operands, batch, seq, coords)
    o_s, o_lo, o_hi = pl.pallas_call(
        functools.partial(_fused_attn_kernel, lam_init=lam_init, n_steps=n_steps, n_blocks=nq),
        grid_spec=pltpu.PrefetchScalarGridSpec(
            num_scalar_prefetch=1, grid=(nb, n_steps), in_specs=d_in + p_in,
            out_specs=d_out + p_out, scratch_shapes=d_scratch + p_scratch),
        out_shape=d_shape + p_shape,
        compiler_params=pltpu.CompilerParams(dimension_semantics=("arbitrary", "arbitrary"),
                                             vmem_limit_bytes=FUSED_VMEM_LIMIT),
        name="diff_attention_both_groups",
    )(page_table, *d_args, *p_args)
    return o_s, (o_lo, o_hi)


def _gate_kernel(x_ref, w1_ref, w2_ref, b_ref, g_ref):
    gl = lax.dot_general(x_ref[...], w1_ref[...], _NT, preferred_element_type=f32)
    u = jnp.dot(gl.astype(bf16), w2_ref[...], preferred_element_type=f32) + b_ref[...]
    ls = jnp.minimum(u, 0.0) - jnp.log(1.0 + jnp.exp(-jnp.abs(u)))
    g_ref[...] = ls * (1.0 / GLA_GATE_NORMALIZER)


def _gla_gate(x, w1, w2, b):
    m = x.shape[0]
    tm = min(m, 512)
    return pl.pallas_call(
        _gate_kernel,
        grid=(m // tm,),
        in_specs=[pl.BlockSpec((tm, D_MODEL), lambda i: (i, 0)),
                  pl.BlockSpec((LANES, D_MODEL), lambda i: (0, 0)),
                  pl.BlockSpec((LANES, GLA_DK_W), lambda i: (0, 0)),
                  pl.BlockSpec((1, GLA_DK_W), lambda i: (0, 0))],
        out_specs=pl.BlockSpec((tm, GLA_DK_W), lambda i: (i, 0)),
        out_shape=jax.ShapeDtypeStruct((m, GLA_DK_W), f32),
        compiler_params=_params("parallel"),
        name="gla_gate",
    )(x, w1, w2, b.reshape(1, GLA_DK_W))


def _gla_kernel(*refs, chunk, chunks_per_step, t_valid, n_steps, has_state):
    q_ref, k_ref, v_ref, g_ref, z_ref, ng_ref = refs[:6]
    s0_ref = refs[6] if has_state else None
    o_ref, so_ref, st_ref = refs[-3:]
    n = pl.program_id(1)

    @pl.when(n == 0)
    def _():
        for h in range(GLA_HEADS):
            if has_state:
                st_ref[h] = s0_ref[h]
            else:
                st_ref[h] = jnp.zeros(st_ref.shape[1:], f32)

    row = lax.broadcasted_iota(jnp.int32, (chunk, chunk), 0)
    col = lax.broadcasted_iota(jnp.int32, (chunk, chunk), 1)
    causal = row >= col
    tri = causal.astype(f32).astype(bf16)
    heads = range(GLA_HEADS)
    kc = [slice(h * GLA_DK, (h + 1) * GLA_DK) for h in heads]
    vc = [slice(h * GLA_DV, (h + 1) * GLA_DV) for h in heads]
    intra = []
    for c in range(chunks_per_step):
        rows = slice(c * chunk, (c + 1) * chunk)
        g = g_ref[rows, :]
        k = k_ref[rows, :]
        if t_valid < chunk:
            valid = lax.broadcasted_iota(jnp.int32, (chunk, GLA_DK_W), 0) < t_valid
            g = jnp.where(valid, g, 0.0)
            k = jnp.where(valid, k, 0.0)
        g_hi = g.astype(bf16)
        g_lo = (g - g_hi.astype(f32)).astype(bf16)
        b = (jnp.dot(tri, g_hi, preferred_element_type=f32)
             + jnp.dot(tri, g_lo, preferred_element_type=f32))
        b_last = b[chunk - 1:chunk, :]
        qe = (q_ref[rows, :] * jnp.exp(b)).astype(bf16)
        ke = (k * jnp.exp(-b)).astype(bf16)
        kd = (k * jnp.exp(b_last - b)).astype(bf16)
        vb = v_ref[rows, :].astype(bf16)
        decay = jnp.broadcast_to(jnp.exp(b_last), (SUBLANES, GLA_DK_W)).T[:, :1]
        a = [lax.dot_general(qe[:, kc[h]], ke[:, kc[h]], _NT, preferred_element_type=f32)
             for h in heads]
        a = [jnp.where(causal, a[h], 0.0).astype(bf16) for h in heads]
        o_intra = [jnp.dot(a[h], vb[:, vc[h]], preferred_element_type=f32) for h in heads]
        intra.append((rows, qe, kd, vb, decay, o_intra))
    st = [st_ref[h] for h in heads]
    for rows, qe, kd, vb, decay, o_intra in intra:
        o = [o_intra[h] + jnp.dot(qe[:, kc[h]], st[h].astype(bf16), preferred_element_type=f32)
             for h in heads]
        st = [decay[kc[h]] * st[h] + lax.dot_general(kd[:, kc[h]], vb[:, vc[h]], _TN,
                                                     preferred_element_type=f32) for h in heads]
        for h in heads:
            o_ref[rows, vc[h]] = _head_epilogue(o[h], ng_ref[...], z_ref[rows, vc[h]],
                                                1.0).astype(o_ref.dtype)
    for h in heads:
        st_ref[h] = st[h]

    @pl.when(n == n_steps - 1)
    def _():
        for h in range(GLA_HEADS):
            so_ref[h] = st_ref[h]


def _gla(q, k, v, g, z, norm_g, s0, layer, n_layers, state_out, batch, seq, t_valid):
    chunk = min(GLA_CHUNK, seq)
    per_step = GLA_CHUNKS_PER_STEP if seq % (GLA_CHUNKS_PER_STEP * chunk) == 0 else 1
    rows = per_step * chunk
    nc = seq // rows
    has_state = s0 is not None
    out_dtype = bf16 if chunk % (2 * SUBLANES) == 0 else f32
    tmap = lambda b, n: (b * nc + n, 0)
    state_spec = pl.BlockSpec((None, None, GLA_HEADS, GLA_DK, GLA_DV),
                              lambda b, n: (layer, b, 0, 0, 0))
    in_specs = [pl.BlockSpec((rows, GLA_DK_W), tmap),
                pl.BlockSpec((rows, GLA_DK_W), tmap),
                pl.BlockSpec((rows, D_MODEL), tmap),
                pl.BlockSpec((rows, GLA_DK_W), tmap),
                pl.BlockSpec((rows, D_MODEL), tmap),
                pl.BlockSpec((1, GLA_DV), lambda b, n: (0, 0))]
    args = [q, k, v, g, z, norm_g.reshape(1, GLA_DV)]
    if has_state:
        in_specs.append(state_spec)
        args.append(s0)
    aliases = {}
    if state_out is not None:
        aliases = {len(args): 1}
        in_specs.append(pl.BlockSpec(memory_space=pl.ANY))
        args.append(state_out)
    return pl.pallas_call(
        functools.partial(_gla_kernel, chunk=chunk, chunks_per_step=per_step, t_valid=t_valid,
                          n_steps=nc, has_state=has_state),
        grid=(batch, nc),
        in_specs=in_specs,
        out_specs=[pl.BlockSpec((rows, D_MODEL), tmap), state_spec],
        out_shape=[jax.ShapeDtypeStruct((batch * seq, D_MODEL), out_dtype),
                   jax.ShapeDtypeStruct((n_layers, batch, GLA_HEADS, GLA_DK, GLA_DV), f32)],
        input_output_aliases=aliases,
        scratch_shapes=[pltpu.VMEM((GLA_HEADS, GLA_DK, GLA_DV), f32)],
        compiler_params=_params("parallel", "arbitrary"),
        name="gla_chunked",
    )(*args)


def _trunk(xp, xs, batch, seq, dec_batch, dec_seq, cache_k, cache_v, state_gla, page_table, weights):
    (rel_bias, diff_w_in, diff_lambda, diff_norm_g, diff_w_out, gla_w_in, gla_w_g1, gla_w_g2,
     gla_b_g, gla_norm_g, gla_w_out, ln_g, ln_b) = weights
    n_diff = (DEPTH + 1) // 2
    xpb = xp.astype(bf16)
    xsb = xs.astype(bf16)
    kbuf = vbuf = None
    n_gla = DEPTH // 2
    states_p = states_s = None
    far = _bias_by_distance(rel_bias, np.array([2 * ATT_BLOCK]))[:, 0]
    tiles = (_prompt_bias_tiles(rel_bias, ATT_BLOCK) - far[:, None, None]) * LOG2E
    bm, bm_new = [t * LOG2E for t in _decode_bias_tables(rel_bias, dec_seq)]
    q_scale = DIFF_DH ** -0.5 * LOG2E
    for i in range(DEPTH):
        j = i // 2
        if i % 2 == 0:
            w = diff_w_in
            qp, qs = _matmul(xpb, xsb, w, j, 0, D_MODEL, bf16, scale=q_scale)
            zp, zs = _matmul(xpb, xsb, w, j, 3 * D_MODEL, D_MODEL, f32, after=(qp,))
            kbuf = _matmul(xpb, xsb, w, j, D_MODEL, D_MODEL, f32, slots=n_diff, slot=j, into=kbuf,
                           after=(zp,))
            vbuf = _matmul(xpb, xsb, w, j, 2 * D_MODEL, D_MODEL, f32, slots=n_diff, slot=j,
                           into=vbuf, after=(kbuf[0],))
            lam_init = 0.8 - 0.6 * math.exp(-0.3 * i)
            os_, op = _diff_attention(
                page_table,
                (cache_k, cache_v, j, qs, kbuf[1][j], vbuf[1][j], zs, bm, bm_new, diff_lambda[j],
                 diff_norm_g[j]),
                (qp, kbuf[0], vbuf[0], j, zp, tiles, diff_lambda[j], diff_norm_g[j]),
                batch, seq, lam_init)
            if j == n_diff - 1:
                k_held, v_held = _hold_until([kbuf[0], vbuf[0]], os_)
                kbuf, vbuf = (k_held, kbuf[1]), (v_held, vbuf[1])
            w_out = diff_w_out[j]
        else:
            w = gla_w_in
            proj = functools.partial(_matmul, xpb, xsb, w, j, w_transposed=True)
            qp, qs = proj(0, GLA_DK_W, f32, scale=GLA_DK ** -0.5)
            kp, ks = proj(GLA_DK_W, GLA_DK_W, f32)
            vp, vs = proj(2 * GLA_DK_W, D_MODEL, f32)
            zp, zs = proj(2 * GLA_DK_W + D_MODEL, D_MODEL, f32)
            gp = _gla_gate(xpb, gla_w_g1[j], gla_w_g2[j], gla_b_g[j])
            gs = _gla_gate(xsb, gla_w_g1[j], gla_w_g2[j], gla_b_g[j])
            op, states_p = _gla(qp, kp, vp, gp, zp, gla_norm_g[j], None, j, n_gla, states_p,
                                batch, seq, seq)
            os_, states_s = _gla(qs, ks, vs, gs, zs, gla_norm_g[j], state_gla, j, n_gla, states_s,
                                 dec_batch, SAMPLE_ROWS, dec_seq)
            w_out = gla_w_out[j]
        xp, xpb = _out_proj_norm(op, w_out, xp, ln_g[i], ln_b[i], seq=seq)
        xs, xsb = _out_proj_norm(os_, w_out, xs, ln_g[i], ln_b[i])
    return xp, xs, kbuf, vbuf, states_p, states_s


def kernel(x_prompt, x_sample, cache_k, cache_v, state_gla, page_table, rel_bias, diff_w_in,
           diff_lambda, diff_norm_g, diff_w_out, gla_w_in, gla_w_g2, gla_b_g, gla_norm_g, gla_w_out,
           ln_g, ln_b):
    batch, seq, _ = x_prompt.shape
    dec_batch, dec_seq, _ = x_sample.shape
    n_pages = page_table.shape[1]
    assert n_pages >= 2 and n_pages % DEC_PAGES_PER_STEP == 0 and dec_seq <= SAMPLE_ROWS
    assert cache_k.shape[2:] == (PAGE_SIZE, DIFF_HEADS, DIFF_HW)
    assert seq % (4 * ATT_BLOCK) == 0 and seq % GLA_CHUNK == 0

    gate_off = 2 * GLA_DK_W + 2 * D_MODEL
    gla_w_t = jnp.swapaxes(gla_w_in, 1, 2)
    w_g1 = jnp.pad(gla_w_t[:, gate_off:, :], ((0, 0), (0, LANES - GLA_GATE_RANK), (0, 0)))
    w_g2 = jnp.pad(gla_w_g2, ((0, 0), (0, LANES - GLA_GATE_RANK), (0, 0)))
    weights = (rel_bias, diff_w_in, diff_lambda, diff_norm_g, diff_w_out.astype(bf16),
               gla_w_t, w_g1.astype(bf16), w_g2.astype(bf16), gla_b_g,
               gla_norm_g,
               gla_w_out.astype(bf16), ln_g, ln_b)

    rows = SAMPLE_ROWS
    xs = jnp.pad(x_sample, ((0, 0), (0, rows - dec_seq), (0, 0))).reshape(dec_batch * rows, D_MODEL)
    y_p, y_s, kbuf, vbuf, s_p, s_s = _trunk(
        x_prompt.reshape(batch * seq, D_MODEL), xs, batch, seq, dec_batch, dec_seq,
        cache_k, cache_v, state_gla, page_table, weights)

    def rows_out(a, b, t, t_keep):
        return a.reshape(a.shape[0], b, t, DIFF_HEADS, DIFF_HW)[:, :, :t_keep]

    return (y_p.reshape(batch, seq, D_MODEL),
            y_s.reshape(dec_batch, rows, D_MODEL)[:, :dec_seq],
            rows_out(kbuf[0], batch, seq, seq), rows_out(vbuf[0], batch, seq, seq), s_p,
            rows_out(kbuf[1], dec_batch, rows, dec_seq), rows_out(vbuf[1], dec_batch, rows, dec_seq),
            s_s)
```

```python
import functools
import math

import jax
import jax.numpy as jnp
import numpy as np
from jax import lax
from jax.experimental import pallas as pl
from jax.experimental.pallas import tpu as pltpu

D_MODEL = 2048
DEPTH = 4
PAGE_SIZE = 128

DIFF_HEADS = 8
DIFF_DH = D_MODEL // (2 * DIFF_HEADS)
DIFF_HW = 2 * DIFF_DH

GLA_HEADS = 4
GLA_DK_W = D_MODEL // 2
GLA_DK = GLA_DK_W // GLA_HEADS
GLA_DV = D_MODEL // GLA_HEADS
GLA_GATE_RANK = 16
GLA_GATE_NORMALIZER = 16.0
GLA_CHUNK = 64
GLA_CHUNKS_PER_STEP = 2

REL_BUCKETS = 32
REL_MAX_DIST = 128

NORM_EPS = 1e-5
DEEPNORM_ALPHA = (2 * DEPTH) ** 0.25

LANES = 128
SUBLANES = 8
VMEM_LIMIT = 48 * 1024 * 1024
MASK_VALUE = -1e30
LOG2E = math.log2(math.e)

ATT_BLOCK = 256
OUT_ROW_GROUPS = 4
DEC_PAGES_PER_STEP = 8
DEC_GROUPS = 4
DEC_VMEM_LIMIT = 56 * 1024 * 1024
FUSED_VMEM_LIMIT = 60 * 1024 * 1024
SAMPLE_ROWS = SUBLANES
PAGE_ROWS = PAGE_SIZE * DIFF_HEADS
NEW_ROWS = 2 * SAMPLE_ROWS * DIFF_HEADS
DEC_COLS = 2 * DIFF_HEADS * SAMPLE_ROWS
assert DEC_COLS == LANES and NEW_ROWS == LANES

_NT = (((1,), (1,)), ((), ()))
_TN = (((0,), (0,)), ((), ()))

bf16 = jnp.bfloat16
f32 = jnp.float32


def _params(*sem):
    return pltpu.CompilerParams(dimension_semantics=sem, vmem_limit_bytes=VMEM_LIMIT)


def _silu(z):
    return z * (1.0 / (1.0 + jnp.exp(-z)))


def _mm_kernel(xp_ref, xs_ref, w_ref, *rest, scale, w_transposed):
    op_ref, os_ref, wb_ref = rest[-3:]
    dims = _NT if w_transposed else (((1,), (0,)), ((), ()))

    def project(x_ref, o_ref):
        acc = lax.dot_general(x_ref[...], wb_ref[...], dims, preferred_element_type=f32)
        if scale != 1.0:
            acc = acc * scale
        o_ref[...] = acc.astype(o_ref.dtype)

    @pl.when(pl.program_id(1) == 0)
    def _():
        wb_ref[...] = w_ref[...].astype(bf16)
        project(xs_ref, os_ref)

    project(xp_ref, op_ref)


def _matmul(xp, xs, w, layer, col_off, n, out_dtype, scale=1.0, slots=None, slot=0, into=None,
            w_transposed=False, after=()):
    m, k = xp.shape
    ms = xs.shape[0]
    tm = min(m, 1024)
    tn = min(n, 1024)
    assert m % tm == 0 and n % tn == 0 and col_off % tn == 0
    off = col_off // tn
    if w_transposed:
        w_spec = pl.BlockSpec((None, tn, k), lambda j, i: (layer, j + off, 0))
        w_tile = (tn, k)
    else:
        w_spec = pl.BlockSpec((None, k, tn), lambda j, i: (layer, 0, j + off))
        w_tile = (k, tn)
    in_specs = [pl.BlockSpec((tm, k), lambda j, i: (i, 0)),
                pl.BlockSpec((ms, k), lambda j, i: (0, 0)),
                w_spec]
    args = [xp, xs, w]
    aliases = {}
    if slots is None:
        out_specs = [pl.BlockSpec((tm, tn), lambda j, i: (i, j)),
                     pl.BlockSpec((ms, tn), lambda j, i: (0, j))]
        out_shape = [jax.ShapeDtypeStruct((m, n), out_dtype),
                     jax.ShapeDtypeStruct((ms, n), out_dtype)]
    else:
        out_specs = [pl.BlockSpec((None, tm, tn), lambda j, i: (slot, i, j)),
                     pl.BlockSpec((None, ms, tn), lambda j, i: (slot, 0, j))]
        out_shape = [jax.ShapeDtypeStruct((slots, m, n), out_dtype),
                     jax.ShapeDtypeStruct((slots, ms, n), out_dtype)]
        if into is not None:
            in_specs += [pl.BlockSpec(memory_space=pl.ANY)] * 2
            args += list(into)
            aliases = {3: 0, 4: 1}
    in_specs += [pl.BlockSpec(memory_space=pl.ANY)] * len(after)
    args += list(after)
    return pl.pallas_call(
        functools.partial(_mm_kernel, scale=scale, w_transposed=w_transposed),
        grid=(n // tn, m // tm),
        in_specs=in_specs,
        out_specs=out_specs,
        out_shape=out_shape,
        input_output_aliases=aliases,
        scratch_shapes=[pltpu.VMEM(w_tile, bf16)],
        compiler_params=_params("parallel", "arbitrary"),
        name="proj_matmul",
    )(*args)


def _hold_kernel(*refs):
    del refs


def _hold_until(bufs, anchor):
    any_spec = pl.BlockSpec(memory_space=pl.ANY)
    return pl.pallas_call(
        _hold_kernel,
        in_specs=[any_spec] * (len(bufs) + 1),
        out_specs=[any_spec] * len(bufs),
        out_shape=[jax.ShapeDtypeStruct(b.shape, b.dtype) for b in bufs],
        input_output_aliases={i: i for i in range(len(bufs))},
        name="hold_row_buffers",
    )(*bufs, anchor)


def _out_kernel(*refs, tiles_per_batch):
    if tiles_per_batch is None:
        o_ref, w_ref, x_ref, g_ref, b_ref, xo_ref, xb_ref = refs
        o = o_ref[...].astype(bf16)
    else:
        olo_ref, ohi_ref, w_ref, x_ref, g_ref, b_ref, xo_ref, xb_ref = refs
        first_half = (pl.program_id(0) % tiles_per_batch) < tiles_per_batch // 2
        o = jnp.where(first_half, olo_ref[...], ohi_ref[...])
    tm = o.shape[0]
    group = tm // OUT_ROW_GROUPS if tm >= OUT_ROW_GROUPS * LANES else tm
    for r0 in range(0, tm, group):
        rows = slice(r0, r0 + group)
        y = jnp.dot(o[rows], w_ref[...], preferred_element_type=f32)
        r = DEEPNORM_ALPHA * x_ref[rows, :] + y
        mu = jnp.mean(r, axis=-1, keepdims=True)
        d = r - mu
        var = jnp.mean(d * d, axis=-1, keepdims=True)
        xn = d * lax.rsqrt(var + NORM_EPS) * g_ref[...] + b_ref[...]
        xo_ref[rows, :] = xn
        xb_ref[rows, :] = xn.astype(bf16)


def _out_proj_norm(o, w, x, g, b, seq=None):
    m = x.shape[0]
    tm = min(m, 512)
    row = lambda i: (i, 0)
    fixed = lambda i: (0, 0)
    if isinstance(o, (tuple, list)):
        tpb = seq // tm
        half = tpb // 2
        assert seq % tm == 0 and tpb % 2 == 0
        o_specs = [pl.BlockSpec((tm, D_MODEL),
                                lambda i: ((i // tpb) * half + jnp.minimum(i % tpb, half - 1), 0)),
                   pl.BlockSpec((tm, D_MODEL),
                                lambda i: ((i // tpb) * half + jnp.maximum(i % tpb - half, 0), 0))]
        o_args = list(o)
    else:
        tpb = None
        o_specs = [pl.BlockSpec((tm, D_MODEL), row)]
        o_args = [o]
    return pl.pallas_call(
        functools.partial(_out_kernel, tiles_per_batch=tpb),
        grid=(m // tm,),
        in_specs=o_specs + [
                  pl.BlockSpec((D_MODEL, D_MODEL), fixed),
                  pl.BlockSpec((tm, D_MODEL), row),
                  pl.BlockSpec((1, D_MODEL), fixed),
                  pl.BlockSpec((1, D_MODEL), fixed)],
        out_specs=[pl.BlockSpec((tm, D_MODEL), row), pl.BlockSpec((tm, D_MODEL), row)],
        out_shape=[jax.ShapeDtypeStruct((m, D_MODEL), f32),
                   jax.ShapeDtypeStruct((m, D_MODEL), bf16)],
        compiler_params=_params("parallel"),
        name="out_proj_deepnorm",
    )(*o_args, w, x, g.reshape(1, D_MODEL), b.reshape(1, D_MODEL))


def _bias_by_distance(rel_bias, dist):
    n = jnp.asarray(dist, jnp.int32)
    max_exact = REL_BUCKETS // 2
    nf = jnp.maximum(n, 1).astype(f32)
    large = max_exact + (jnp.log(nf / max_exact) / math.log(REL_MAX_DIST / max_exact)
                         * (REL_BUCKETS - max_exact)).astype(jnp.int32)
    large = jnp.minimum(large, REL_BUCKETS - 1)
    bucket = jnp.where(n < max_exact, n, large)
    return jnp.moveaxis(rel_bias.astype(f32)[bucket], -1, 0)


def _prompt_bias_tiles(rel_bias, blk):
    k = np.arange(3 * blk - 1)
    d = 2 * blk - 1 - k
    u = jnp.where(jnp.asarray(d >= 0), _bias_by_distance(rel_bias, np.maximum(d, 0)), MASK_VALUE)
    period = 3 * blk
    flat = jnp.tile(jnp.pad(u, ((0, 0), (0, 1))), (1, blk))[:, :blk * (period - 1)]
    return flat.reshape(DIFF_HEADS, blk, period - 1)[:, :, blk - 1:3 * blk - 1]


def _decode_bias_tables(rel_bias, t_valid):
    col = np.arange(DEC_COLS)
    col_h = (col // SAMPLE_ROWS) % DIFF_HEADS
    reps = DEC_COLS // SAMPLE_ROWS
    tq = np.arange(SAMPLE_ROWS)[None, :]
    new_tokens = NEW_ROWS // DIFF_HEADS
    tab = _bias_by_distance(rel_bias, np.arange(2 * PAGE_SIZE + SAMPLE_ROWS + 1))

    def toeplitz_rows(ext, n_tokens):
        wins = [ext[:, t + 1:t + 1 + n_tokens][:, ::-1] for t in range(SAMPLE_ROWS)]
        return jnp.transpose(jnp.stack(wins), (2, 1, 0)).reshape(n_tokens * DIFF_HEADS, SAMPLE_ROWS)

    def expand(vals, visible):
        n_rows = vals.shape[0]
        r = np.arange(n_rows)
        ok = ((r % DIFF_HEADS)[:, None] == col_h[None, :]) & np.tile(visible, (1, reps))
        return jnp.where(jnp.asarray(ok), jnp.tile(vals, (1, reps)), MASK_VALUE)

    everything = np.ones((PAGE_ROWS, SAMPLE_ROWS), bool)
    far_vals = jnp.broadcast_to(jnp.tile(tab[:, 2 * PAGE_SIZE], PAGE_SIZE)[:, None],
                                (PAGE_ROWS, SAMPLE_ROWS))
    far = expand(far_vals, everything)
    last = expand(toeplitz_rows(tab, PAGE_SIZE), everything)
    s_new = (np.arange(NEW_ROWS) // DIFF_HEADS)[:, None]
    ext = jnp.pad(tab[:, :SAMPLE_ROWS], ((0, 0), (new_tokens, 0)))
    new = expand(toeplitz_rows(ext, new_tokens), (s_new <= tq) & (s_new < t_valid))
    return jnp.stack([far, last]), new


def _lambda_value(lp_ref, lam_init):
    lp = lp_ref[...]
    a = jnp.sum(lp[0:1] * lp[1:2], axis=-1, keepdims=True)
    b = jnp.sum(lp[2:3] * lp[3:4], axis=-1, keepdims=True)
    return jnp.exp(a) - jnp.exp(b) + lam_init


def _head_epilogue(o, g, z, out_scale):
    ms = jnp.mean(o * o, axis=-1, keepdims=True)
    return (o * lax.rsqrt(ms + NORM_EPS) * g * out_scale) * _silu(z)


def _attn_step(i, qlo_ref, qhi_ref, k_ref, v_ref, zlo_ref, zhi_ref, bias_ref, lp_ref, g_ref,
               olo_ref, ohi_ref, kb_ref, vb_ref, s_ref, *, lam_init, n_blocks, between=()):
    blk = ATT_BLOCK

    @pl.when(i == 0)
    def _():
        kb_ref[...] = k_ref[...].astype(bf16)
        vb_ref[...] = v_ref[...].astype(bf16)

    lam = _lambda_value(lp_ref, lam_init)

    def lane_fold(x):
        return [x[:, c * LANES:(c + 1) * LANES] for c in range(blk // LANES)]

    def logits_and_max(q_ref, nvis, base):
        q = q_ref[...]
        qs = (q[:, :DIFF_DH], q[:, DIFF_DH:])
        mx = [None, None]
        for j in range(nvis):
            kj = kb_ref[j * blk:(j + 1) * blk, :]
            if j == nvis - 1:
                bias = bias_ref[:, blk:]
            elif j == nvis - 2:
                bias = bias_ref[:, :blk]
            else:
                bias = None
            for mp in range(2):
                s = lax.dot_general(qs[mp], kj[:, mp * DIFF_DH:(mp + 1) * DIFF_DH], _NT,
                                    preferred_element_type=f32)
                if bias is not None:
                    s = s + bias
                s_ref[mp, base + j] = s
                for part in lane_fold(s):
                    mx[mp] = part if mx[mp] is None else jnp.maximum(mx[mp], part)
        return [jnp.max(mx[mp], axis=-1, keepdims=True) for mp in range(2)]

    def exponentials(nvis, base, m):
        ls = [None, None]
        for j in range(nvis):
            for mp in range(2):
                p = jnp.exp2(s_ref[mp, base + j] - m[mp])
                s_ref[mp, base + j] = p
                for part in lane_fold(p):
                    ls[mp] = part if ls[mp] is None else ls[mp] + part
        return (1.0 / jnp.sum(ls[0], axis=-1, keepdims=True),
                lam / jnp.sum(ls[1], axis=-1, keepdims=True))

    def weighted_values(nvis, base, c, z_ref, o_ref):
        acc = None
        for j in range(nvis):
            pd = (s_ref[0, base + j] * c[0] - s_ref[1, base + j] * c[1]).astype(bf16)
            pv = jnp.dot(pd, vb_ref[j * blk:(j + 1) * blk, :], preferred_element_type=f32)
            acc = pv if acc is None else acc + pv
        o_ref[...] = _head_epilogue(acc, g_ref[...], z_ref[...], 1.0 - lam_init).astype(bf16)

    def block_pair(lo):
        pending = list(between)

        def other_work():
            if pending:
                pending.pop(0)()

        n_lo, n_hi = lo + 1, n_blocks - lo
        other_work()
        m_lo = logits_and_max(qlo_ref, n_lo, 0)
        other_work()
        m_hi = logits_and_max(qhi_ref, n_hi, n_lo)
        other_work()
        c_lo = exponentials(n_lo, 0, m_lo)
        other_work()
        c_hi = exponentials(n_hi, n_lo, m_hi)
        while pending:
            other_work()
        weighted_values(n_lo, 0, c_lo, zlo_ref, olo_ref)
        weighted_values(n_hi, n_lo, c_hi, zhi_ref, ohi_ref)

    for lo in range(n_blocks // 2):
        pl.when(i == lo)(functools.partial(block_pair, lo))


def _attn_kernel(*refs, **static):
    _attn_step(pl.program_id(2), *refs, **static)


def _prompt_specs(q, k, v, slot, z, bias_tiles, lam_p, norm_g, batch, seq, coords):
    nq = seq // ATT_BLOCK
    half = nq // 2
    blk = ATT_BLOCK

    def at(f):
        return lambda *g: f(*coords(*g))

    lo_in = at(lambda b, h, i: (b * nq + i, h))
    hi_in = at(lambda b, h, i: (b * nq + nq - 1 - i, h))
    lo_out = at(lambda b, h, i: (b * half + i, h))
    hi_out = at(lambda b, h, i: (b * half + half - 1 - i, h))
    kvmap = at(lambda b, h, i: (slot, b, h))
    out = jax.ShapeDtypeStruct((batch * seq // 2, D_MODEL), bf16)
    args = [q, q, k, v, z, z, bias_tiles, lam_p, norm_g.reshape(1, DIFF_HW)]
    in_specs = [pl.BlockSpec((blk, DIFF_HW), lo_in),
                pl.BlockSpec((blk, DIFF_HW), hi_in),
                pl.BlockSpec((None, seq, DIFF_HW), kvmap),
                pl.BlockSpec((None, seq, DIFF_HW), kvmap),
                pl.BlockSpec((blk, DIFF_HW), lo_in),
                pl.BlockSpec((blk, DIFF_HW), hi_in),
                pl.BlockSpec((None, blk, 2 * blk), at(lambda b, h, i: (h, 0, 0))),
                pl.BlockSpec((4, DIFF_DH), at(lambda b, h, i: (0, 0))),
                pl.BlockSpec((1, DIFF_HW), at(lambda b, h, i: (0, 0)))]
    out_specs = [pl.BlockSpec((blk, DIFF_HW), lo_out), pl.BlockSpec((blk, DIFF_HW), hi_out)]
    scratch = [pltpu.VMEM((seq, DIFF_HW), bf16),
               pltpu.VMEM((seq, DIFF_HW), bf16),
               pltpu.VMEM((2, nq + 1, blk, blk), f32)]
    return args, in_specs, out_specs, [out, out], scratch


def _prompt_attention(q, k, v, slot, z, bias_tiles, lam_p, norm_g, batch, seq, lam_init):
    nq = seq // ATT_BLOCK
    args, in_specs, out_specs, out_shape, scratch = _prompt_specs(
        q, k, v, slot, z, bias_tiles, lam_p, norm_g, batch, seq, lambda b, h, i: (b, h, i))
    return pl.pallas_call(
        functools.partial(_attn_kernel, lam_init=lam_init, n_blocks=nq),
        grid=(batch, DIFF_HEADS, nq // 2),
        in_specs=in_specs,
        out_specs=out_specs,
        out_shape=out_shape,
        scratch_shapes=scratch,
        compiler_params=_params("parallel", "parallel", "arbitrary"),
        name="prompt_diff_attention",
    )(*args)


DEC_INPUTS = 2 * DEC_PAGES_PER_STEP + 8


def _decode_kernel(pt_ref, *refs, **static):
    del pt_ref
    _decode_step(pl.program_id(1), refs, **static)


def _decode_step(s, refs, *, lam_init, n_steps, init=True, groups=range(DEC_GROUPS), finish=True):
    npg = DEC_PAGES_PER_STEP
    k_pages = refs[:npg]
    v_pages = refs[npg:2 * npg]
    (w_ref, kn_ref, vn_ref, z_ref, bm_ref, bmn_ref, lp_ref, g_ref, o_ref,
     m_ref, l_ref, acc_ref) = refs[2 * npg:]

    if init:
        @pl.when(s == 0)
        def _():
            m_ref[...] = jnp.full(m_ref.shape, MASK_VALUE, f32)
            l_ref[...] = jnp.zeros(l_ref.shape, f32)
            acc_ref[...] = jnp.zeros(acc_ref.shape, f32)

    w = w_ref[...]
    eye = (lax.broadcasted_iota(jnp.int32, (DEC_COLS, DEC_COLS), 0)
           == lax.broadcasted_iota(jnp.int32, (DEC_COLS, DEC_COLS), 1))

    def to_column(row):
        return jnp.sum(jnp.where(eye, row, 0.0), axis=1, keepdims=True)

    def update(grp, blocks):
        logits = [jnp.dot(k, w, preferred_element_type=f32) + bm for k, _, bm in blocks]
        cmax = None
        for lg in logits:
            c = jnp.max(lg, axis=0, keepdims=True)
            cmax = c if cmax is None else jnp.maximum(cmax, c)
        m_old = m_ref[grp]
        m_new = jnp.maximum(m_old, cmax)
        a = jnp.exp2(m_old - m_new)
        lsum = None
        pv = None
        for lg, (_, v, _) in zip(logits, blocks):
            p = jnp.exp2(lg - m_new)
            ps = jnp.sum(p, axis=0, keepdims=True)
            lsum = ps if lsum is None else lsum + ps
            d = jnp.dot(p.T.astype(bf16), v.astype(bf16), preferred_element_type=f32)
            pv = d if pv is None else pv + d
        m_ref[grp] = m_new
        l_ref[grp] = a * l_ref[grp] + lsum
        acc_ref[grp] = acc_ref[grp] * to_column(a) + pv

    per_group = npg // DEC_GROUPS
    for grp in groups:
        blocks = []
        for pg in range(grp * per_group, (grp + 1) * per_group):
            if pg == npg - 1:
                bm = bm_ref[jnp.where(s == n_steps - 1, 1, 0)]
            else:
                bm = bm_ref[0]
            blocks.append((k_pages[pg][...], v_pages[pg][...], bm))
        update(grp, blocks)

    if not finish:
        return

    @pl.when(s == n_steps - 1)
    def _():
        update(0, [(kn_ref[...], vn_ref[...], bmn_ref[...])])
        m_all = m_ref[0]
        for grp in range(1, DEC_GROUPS):
            m_all = jnp.maximum(m_all, m_ref[grp])
        l_all = None
        acc_all = None
        for grp in range(DEC_GROUPS):
            a = jnp.exp2(m_ref[grp] - m_all)
            lg = a * l_ref[grp]
            ag = acc_ref[grp] * to_column(a)
            l_all = lg if l_all is None else l_all + lg
            acc_all = ag if acc_all is None else acc_all + ag
        on = acc_all * (1.0 / to_column(l_all))
        lam = _lambda_value(lp_ref, lam_init)
        half = DEC_COLS // 2
        for h in range(DIFF_HEADS):
            r0 = h * SAMPLE_ROWS
            o = on[r0:r0 + SAMPLE_ROWS] - lam * on[half + r0:half + r0 + SAMPLE_ROWS]
            lo = h * DIFF_HW
            o_ref[:, lo:lo + DIFF_HW] = _head_epilogue(o, g_ref[...], z_ref[:, lo:lo + DIFF_HW],
                                                       1.0 - lam_init)


def _decode_specs(page_table, cache_k, cache_v, layer, q, k_new, v_new, z, bm, bm_new, lam_p, norm_g):
    nb, n_pages = page_table.shape
    npg = DEC_PAGES_PER_STEP
    ck = cache_k.reshape(cache_k.shape[0], cache_k.shape[1], PAGE_ROWS, DIFF_HW)
    cv = cache_v.reshape(cache_v.shape[0], cache_v.shape[1], PAGE_ROWS, DIFF_HW)

    q5 = q.reshape(nb, SAMPLE_ROWS, DIFF_HEADS, 2, DIFF_DH)
    qt = jnp.transpose(q5, (0, 3, 4, 2, 1)).reshape(nb, 2, DIFF_DH, DEC_COLS // 2)
    zeros = jnp.zeros_like(qt[:, 0])
    w = jnp.concatenate([jnp.concatenate([qt[:, 0], zeros], axis=-1),
                         jnp.concatenate([zeros, qt[:, 1]], axis=-1)], axis=1)

    def new_rows(a):
        a = a.reshape(nb, SAMPLE_ROWS, DIFF_HEADS, DIFF_HW)
        a = jnp.pad(a, ((0, 0), (0, NEW_ROWS // DIFF_HEADS - SAMPLE_ROWS), (0, 0), (0, 0)))
        return a.reshape(nb, NEW_ROWS, DIFF_HW)

    def page_spec(pg):
        return pl.BlockSpec((None, None, PAGE_ROWS, DIFF_HW),
                            lambda b, s, pt: (layer, pt[b, s * npg + pg], 0, 0))

    per_batch3 = lambda b, s, pt: (b, 0, 0)
    row = lambda b, s, pt: (b, 0)
    fixed2 = lambda b, s, pt: (0, 0)
    in_specs = ([page_spec(pg) for pg in range(npg)] + [page_spec(pg) for pg in range(npg)]
                + [pl.BlockSpec((None, DIFF_HW, DEC_COLS), per_batch3),
                   pl.BlockSpec((None, NEW_ROWS, DIFF_HW), per_batch3),
                   pl.BlockSpec((None, NEW_ROWS, DIFF_HW), per_batch3),
                   pl.BlockSpec((SAMPLE_ROWS, D_MODEL), row),
                   pl.BlockSpec((2, PAGE_ROWS, DEC_COLS), lambda b, s, pt: (0, 0, 0)),
                   pl.BlockSpec((NEW_ROWS, DEC_COLS), fixed2),
                   pl.BlockSpec((4, DIFF_DH), fixed2),
                   pl.BlockSpec((1, DIFF_HW), fixed2)])
    assert len(in_specs) == DEC_INPUTS
    args = [*([ck] * npg), *([cv] * npg), w, new_rows(k_new), new_rows(v_new), z, bm, bm_new,
            lam_p, norm_g.reshape(1, DIFF_HW)]
    out_specs = [pl.BlockSpec((SAMPLE_ROWS, D_MODEL), row)]
    out_shape = [jax.ShapeDtypeStruct((nb * SAMPLE_ROWS, D_MODEL), f32)]
    scratch = [pltpu.VMEM((DEC_GROUPS, 1, DEC_COLS), f32),
               pltpu.VMEM((DEC_GROUPS, 1, DEC_COLS), f32),
               pltpu.VMEM((DEC_GROUPS, DEC_COLS, DIFF_HW), f32)]
    return args, in_specs, out_specs, out_shape, scratch


def _decode_attention(page_table, *operands, lam_init):
    nb, n_pages = page_table.shape
    n_steps = n_pages // DEC_PAGES_PER_STEP
    args, in_specs, out_specs, out_shape, scratch = _decode_specs(page_table, *operands)
    return pl.pallas_call(
        functools.partial(_decode_kernel, lam_init=lam_init, n_steps=n_steps),
        grid_spec=pltpu.PrefetchScalarGridSpec(
            num_scalar_prefetch=1, grid=(nb, n_steps), in_specs=in_specs, out_specs=out_specs,
            scratch_shapes=scratch),
        out_shape=out_shape,
        compiler_params=pltpu.CompilerParams(dimension_semantics=("parallel", "arbitrary"),
                                             vmem_limit_bytes=DEC_VMEM_LIMIT),
        name="decode_diff_attention",
    )(page_table, *args)[0]


def _fused_attn_kernel(pt_ref, *refs, lam_init, n_steps, n_blocks):
    del pt_ref
    n_att_in = 9
    dec_in = refs[:DEC_INPUTS]
    att_in = refs[DEC_INPUTS:DEC_INPUTS + n_att_in]
    o_dec, o_lo, o_hi = refs[DEC_INPUTS + n_att_in:DEC_INPUTS + n_att_in + 3]
    m_ref, l_ref, acc_ref, kb_ref, vb_ref, s_ref = refs[DEC_INPUTS + n_att_in + 3:]
    s = pl.program_id(1)
    step = pl.program_id(0) * n_steps + s
    decode = functools.partial(_decode_step, s, (*dec_in, o_dec, m_ref, l_ref, acc_ref),
                               lam_init=lam_init, n_steps=n_steps)
    decode(init=True, groups=(), finish=False)
    _attn_step(step % (n_blocks // 2), *att_in, o_lo, o_hi, kb_ref, vb_ref, s_ref,
               lam_init=lam_init, n_blocks=n_blocks,
               between=[functools.partial(decode, init=False, groups=(grp,), finish=False)
                        for grp in range(DEC_GROUPS)])
    decode(init=False, groups=(), finish=True)


def _diff_attention(page_table, decode_operands, prompt_operands, batch, seq, lam_init):
    nb, n_pages = page_table.shape
    n_steps = n_pages // DEC_PAGES_PER_STEP
    nq = seq // ATT_BLOCK
    half = nq // 2
    if nb * n_steps != batch * DIFF_HEADS * half:
        o_s = _decode_attention(page_table, *decode_operands, lam_init=lam_init)
        return o_s, _prompt_attention(*prompt_operands, batch, seq, lam_init)

    def coords(b, s, pt):
        step = b * n_steps + s
        return step // (DIFF_HEADS * half), (step // half) % DIFF_HEADS, step % half

    d_args, d_in, d_out, d_shape, d_scratch = _decode_specs(page_table, *decode_operands)
    p_args, p_in, p_out, p_shape, p_scratch = _prompt_specs(*prompt_operands, batch, seq, coords)
    o_s, o_lo, o_hi = pl.pallas_call(
        functools.partial(_fused_attn_kernel, lam_init=lam_init, n_steps=n_steps, n_blocks=nq),
        grid_spec=pltpu.PrefetchScalarGridSpec(
            num_scalar_prefetch=1, grid=(nb, n_steps), in_specs=d_in + p_in,
            out_specs=d_out + p_out, scratch_shapes=d_scratch + p_scratch),
        out_shape=d_shape + p_shape,
        compiler_params=pltpu.CompilerParams(dimension_semantics=("arbitrary", "arbitrary"),
                                             vmem_limit_bytes=FUSED_VMEM_LIMIT),
        name="diff_attention_both_groups",
    )(page_table, *d_args, *p_args)
    return o_s, (o_lo, o_hi)


def _gate_kernel(x_ref, w1_ref, w2_ref, b_ref, g_ref):
    gl = lax.dot_general(x_ref[...], w1_ref[...], _NT, preferred_element_type=f32)
    u = jnp.dot(gl.astype(bf16), w2_ref[...], preferred_element_type=f32) + b_ref[...]
    ls = jnp.minimum(u, 0.0) - jnp.log(1.0 + jnp.exp(-jnp.abs(u)))
    g_ref[...] = ls * (1.0 / GLA_GATE_NORMALIZER)


def _gla_gate(x, w1, w2, b):
    m = x.shape[0]
    tm = min(m, 512)
    return pl.pallas_call(
        _gate_kernel,
        grid=(m // tm,),
        in_specs=[pl.BlockSpec((tm, D_MODEL), lambda i: (i, 0)),
                  pl.BlockSpec((LANES, D_MODEL), lambda i: (0, 0)),
                  pl.BlockSpec((LANES, GLA_DK_W), lambda i: (0, 0)),
                  pl.BlockSpec((1, GLA_DK_W), lambda i: (0, 0))],
        out_specs=pl.BlockSpec((tm, GLA_DK_W), lambda i: (i, 0)),
        out_shape=jax.ShapeDtypeStruct((m, GLA_DK_W), f32),
        compiler_params=_params("parallel"),
        name="gla_gate",
    )(x, w1, w2, b.reshape(1, GLA_DK_W))


def _gla_kernel(*refs, chunk, chunks_per_step, t_valid, n_steps, has_state):
    q_ref, k_ref, v_ref, g_ref, z_ref, ng_ref = refs[:6]
    s0_ref = refs[6] if has_state else None
    o_ref, so_ref, st_ref = refs[-3:]
    n = pl.program_id(1)

    @pl.when(n == 0)
    def _():
        for h in range(GLA_HEADS):
            if has_state:
                st_ref[h] = s0_ref[h]
            else:
                st_ref[h] = jnp.zeros(st_ref.shape[1:], f32)

    row = lax.broadcasted_iota(jnp.int32, (chunk, chunk), 0)
    col = lax.broadcasted_iota(jnp.int32, (chunk, chunk), 1)
    causal = row >= col
    tri = causal.astype(f32).astype(bf16)
    heads = range(GLA_HEADS)
    kc = [slice(h * GLA_DK, (h + 1) * GLA_DK) for h in heads]
    vc = [slice(h * GLA_DV, (h + 1) * GLA_DV) for h in heads]
    intra = []
    for c in range(chunks_per_step):
        rows = slice(c * chunk, (c + 1) * chunk)
        g = g_ref[rows, :]
        k = k_ref[rows, :]
        if t_valid < chunk:
            valid = lax.broadcasted_iota(jnp.int32, (chunk, GLA_DK_W), 0) < t_valid
            g = jnp.where(valid, g, 0.0)
            k = jnp.where(valid, k, 0.0)
        g_hi = g.astype(bf16)
        g_lo = (g - g_hi.astype(f32)).astype(bf16)
        b = (jnp.dot(tri, g_hi, preferred_element_type=f32)
             + jnp.dot(tri, g_lo, preferred_element_type=f32))
        b_last = b[chunk - 1:chunk, :]
        qe = (q_ref[rows, :] * jnp.exp(b)).astype(bf16)
        ke = (k * jnp.exp(-b)).astype(bf16)
        kd = (k * jnp.exp(b_last - b)).astype(bf16)
        vb = v_ref[rows, :].astype(bf16)
        decay = jnp.broadcast_to(jnp.exp(b_last), (SUBLANES, GLA_DK_W)).T[:, :1]
        a = [lax.dot_general(qe[:, kc[h]], ke[:, kc[h]], _NT, preferred_element_type=f32)
             for h in heads]
        a = [jnp.where(causal, a[h], 0.0).astype(bf16) for h in heads]
        o_intra = [jnp.dot(a[h], vb[:, vc[h]], preferred_element_type=f32) for h in heads]
        intra.append((rows, qe, kd, vb, decay, o_intra))
    st = [st_ref[h] for h in heads]
    for rows, qe, kd, vb, decay, o_intra in intra:
        o = [o_intra[h] + jnp.dot(qe[:, kc[h]], st[h].astype(bf16), preferred_element_type=f32)
             for h in heads]
        st = [decay[kc[h]] * st[h] + lax.dot_general(kd[:, kc[h]], vb[:, vc[h]], _TN,
                                                     preferred_element_type=f32) for h in heads]
        for h in heads:
            o_ref[rows, vc[h]] = _head_epilogue(o[h], ng_ref[...], z_ref[rows, vc[h]],
                                                1.0).astype(o_ref.dtype)
    for h in heads:
        st_ref[h] = st[h]

    @pl.when(n == n_steps - 1)
    def _():
        for h in range(GLA_HEADS):
            so_ref[h] = st_ref[h]


def _gla(q, k, v, g, z, norm_g, s0, layer, n_layers, state_out, batch, seq, t_valid):
    chunk = min(GLA_CHUNK, seq)
    per_step = GLA_CHUNKS_PER_STEP if seq % (GLA_CHUNKS_PER_STEP * chunk) == 0 else 1
    rows = per_step * chunk
    nc = seq // rows
    has_state = s0 is not None
    out_dtype = bf16 if chunk % (2 * SUBLANES) == 0 else f32
    tmap = lambda b, n: (b * nc + n, 0)
    state_spec = pl.BlockSpec((None, None, GLA_HEADS, GLA_DK, GLA_DV),
                              lambda b, n: (layer, b, 0, 0, 0))
    in_specs = [pl.BlockSpec((rows, GLA_DK_W), tmap),
                pl.BlockSpec((rows, GLA_DK_W), tmap),
                pl.BlockSpec((rows, D_MODEL), tmap),
                pl.BlockSpec((rows, GLA_DK_W), tmap),
                pl.BlockSpec((rows, D_MODEL), tmap),
                pl.BlockSpec((1, GLA_DV), lambda b, n: (0, 0))]
    args = [q, k, v, g, z, norm_g.reshape(1, GLA_DV)]
    if has_state:
        in_specs.append(state_spec)
        args.append(s0)
    aliases = {}
    if state_out is not None:
        aliases = {len(args): 1}
        in_specs.append(pl.BlockSpec(memory_space=pl.ANY))
        args.append(state_out)
    return pl.pallas_call(
        functools.partial(_gla_kernel, chunk=chunk, chunks_per_step=per_step, t_valid=t_valid,
                          n_steps=nc, has_state=has_state),
        grid=(batch, nc),
        in_specs=in_specs,
        out_specs=[pl.BlockSpec((rows, D_MODEL), tmap), state_spec],
        out_shape=[jax.ShapeDtypeStruct((batch * seq, D_MODEL), out_dtype),
                   jax.ShapeDtypeStruct((n_layers, batch, GLA_HEADS, GLA_DK, GLA_DV), f32)],
        input_output_aliases=aliases,
        scratch_shapes=[pltpu.VMEM((GLA_HEADS, GLA_DK, GLA_DV), f32)],
        compiler_params=_params("parallel", "arbitrary"),
        name="gla_chunked",
    )(*args)


def _trunk(xp, xs, batch, seq, dec_batch, dec_seq, cache_k, cache_v, state_gla, page_table, weights):
    (rel_bias, diff_w_in, diff_lambda, diff_norm_g, diff_w_out, gla_w_in, gla_w_g1, gla_w_g2,
     gla_b_g, gla_norm_g, gla_w_out, ln_g, ln_b) = weights
    n_diff = (DEPTH + 1) // 2
    xpb = xp.astype(bf16)
    xsb = xs.astype(bf16)
    kbuf = vbuf = None
    n_gla = DEPTH // 2
    last_diff = 2 * (n_diff - 1)
    states_p = states_s = None
    far = _bias_by_distance(rel_bias, np.array([2 * ATT_BLOCK]))[:, 0]
    tiles = (_prompt_bias_tiles(rel_bias, ATT_BLOCK) - far[:, None, None]) * LOG2E
    bm, bm_new = [t * LOG2E for t in _decode_bias_tables(rel_bias, dec_seq)]
    q_scale = DIFF_DH ** -0.5 * LOG2E
    for i in range(DEPTH):
        j = i // 2
        if i % 2 == 0:
            w = diff_w_in
            qp, qs = _matmul(xpb, xsb, w, j, 0, D_MODEL, bf16, scale=q_scale)
            zp, zs = _matmul(xpb, xsb, w, j, 3 * D_MODEL, D_MODEL, f32, after=(qp,))
            kbuf = _matmul(xpb, xsb, w, j, D_MODEL, D_MODEL, f32, slots=n_diff, slot=j, into=kbuf,
                           after=(zp,))
            vbuf = _matmul(xpb, xsb, w, j, 2 * D_MODEL, D_MODEL, f32, slots=n_diff, slot=j,
                           into=vbuf, after=(kbuf[0],))
            lam_init = 0.8 - 0.6 * math.exp(-0.3 * i)
            os_, op = _diff_attention(
                page_table,
                (cache_k, cache_v, j, qs, kbuf[1][j], vbuf[1][j], zs, bm, bm_new, diff_lambda[j],
                 diff_norm_g[j]),
                (qp, kbuf[0], vbuf[0], j, zp, tiles, diff_lambda[j], diff_norm_g[j]),
                batch, seq, lam_init)
            w_out = diff_w_out[j]
        else:
            w = gla_w_in
            proj = functools.partial(_matmul, xpb, xsb, w, j, w_transposed=True)
            qp, qs = proj(0, GLA_DK_W, f32, scale=GLA_DK ** -0.5)
            kp, ks = proj(GLA_DK_W, GLA_DK_W, f32)
            vp, vs = proj(2 * GLA_DK_W, D_MODEL, f32)
            zp, zs = proj(2 * GLA_DK_W + D_MODEL, D_MODEL, f32)
            if i == last_diff + 1:
                vbuf = (_hold_until([vbuf[0]], zp)[0], vbuf[1])
            gp = _gla_gate(xpb, gla_w_g1[j], gla_w_g2[j], gla_b_g[j])
            gs = _gla_gate(xsb, gla_w_g1[j], gla_w_g2[j], gla_b_g[j])
            op, states_p = _gla(qp, kp, vp, gp, zp, gla_norm_g[j], None, j, n_gla, states_p,
                                batch, seq, seq)
            os_, states_s = _gla(qs, ks, vs, gs, zs, gla_norm_g[j], state_gla, j, n_gla, states_s,
                                 dec_batch, SAMPLE_ROWS, dec_seq)
            w_out = gla_w_out[j]
        xp, xpb = _out_proj_norm(op, w_out, xp, ln_g[i], ln_b[i], seq=seq)
        xs, xsb = _out_proj_norm(os_, w_out, xs, ln_g[i], ln_b[i])
        if i == last_diff:
            kbuf = (_hold_until([kbuf[0]], xp)[0], kbuf[1])
            if last_diff + 1 >= DEPTH:
                vbuf = (_hold_until([vbuf[0]], xp)[0], vbuf[1])
    return xp, xs, kbuf, vbuf, states_p, states_s


def kernel(x_prompt, x_sample, cache_k, cache_v, state_gla, page_table, rel_bias, diff_w_in,
           diff_lambda, diff_norm_g, diff_w_out, gla_w_in, gla_w_g2, gla_b_g, gla_norm_g, gla_w_out,
           ln_g, ln_b):
    batch, seq, _ = x_prompt.shape
    dec_batch, dec_seq, _ = x_sample.shape
    n_pages = page_table.shape[1]
    assert n_pages >= 2 and n_pages % DEC_PAGES_PER_STEP == 0 and dec_seq <= SAMPLE_ROWS
    assert cache_k.shape[2:] == (PAGE_SIZE, DIFF_HEADS, DIFF_HW)
    assert seq % (4 * ATT_BLOCK) == 0 and seq % GLA_CHUNK == 0

    gate_off = 2 * GLA_DK_W + 2 * D_MODEL
    gla_w_t = jnp.swapaxes(gla_w_in, 1, 2)
    w_g1 = jnp.pad(gla_w_t[:, gate_off:, :], ((0, 0), (0, LANES - GLA_GATE_RANK), (0, 0)))
    w_g2 = jnp.pad(gla_w_g2, ((0, 0), (0, LANES - GLA_GATE_RANK), (0, 0)))
    weights = (rel_bias, diff_w_in, diff_lambda, diff_norm_g, diff_w_out.astype(bf16),
               gla_w_t, w_g1.astype(bf16), w_g2.astype(bf16), gla_b_g,
               gla_norm_g,
               gla_w_out.astype(bf16), ln_g, ln_b)

    rows = SAMPLE_ROWS
    xs = jnp.pad(x_sample, ((0, 0), (0, rows - dec_seq), (0, 0))).reshape(dec_batch * rows, D_MODEL)
    y_p, y_s, kbuf, vbuf, s_p, s_s = _trunk(
        x_prompt.reshape(batch * seq, D_MODEL), xs, batch, seq, dec_batch, dec_seq,
        cache_k, cache_v, state_gla, page_table, weights)

    def rows_out(a, b, t, t_keep):
        return a.reshape(a.shape[0], b, t, DIFF_HEADS, DIFF_HW)[:, :, :t_keep]

    return (y_p.reshape(batch, seq, D_MODEL),
            y_s.reshape(dec_batch, rows, D_MODEL)[:, :dec_seq],
            rows_out(kbuf[0], batch, seq, seq), rows_out(vbuf[0], batch, seq, seq), s_p,
            rows_out(kbuf[1], dec_batch, rows, dec_seq), rows_out(vbuf[1], dec_batch, rows, dec_seq),
            s_s)
```

```python
import functools
import math

import jax
import jax.numpy as jnp
import numpy as np
from jax import lax
from jax.experimental import pallas as pl
from jax.experimental.pallas import tpu as pltpu

D_MODEL = 2048
DEPTH = 4
PAGE_SIZE = 128

DIFF_HEADS = 8
DIFF_DH = D_MODEL // (2 * DIFF_HEADS)
DIFF_HW = 2 * DIFF_DH

GLA_HEADS = 4
GLA_DK_W = D_MODEL // 2
GLA_DK = GLA_DK_W // GLA_HEADS
GLA_DV = D_MODEL // GLA_HEADS
GLA_GATE_RANK = 16
GLA_GATE_NORMALIZER = 16.0
GLA_CHUNK = 64
GLA_CHUNKS_PER_STEP = 4

REL_BUCKETS = 32
REL_MAX_DIST = 128

NORM_EPS = 1e-5
DEEPNORM_ALPHA = (2 * DEPTH) ** 0.25

LANES = 128
SUBLANES = 8
VMEM_LIMIT = 48 * 1024 * 1024
MASK_VALUE = -1e30
LOG2E = math.log2(math.e)

ATT_BLOCK = 256
OUT_ROW_GROUPS = 4
DEC_PAGES_PER_STEP = 8
DEC_GROUPS = 4
DEC_VMEM_LIMIT = 56 * 1024 * 1024
FUSED_VMEM_LIMIT = 60 * 1024 * 1024
SAMPLE_ROWS = SUBLANES
PAGE_ROWS = PAGE_SIZE * DIFF_HEADS
NEW_ROWS = 2 * SAMPLE_ROWS * DIFF_HEADS
DEC_COLS = 2 * DIFF_HEADS * SAMPLE_ROWS
assert DEC_COLS == LANES and NEW_ROWS == LANES

_NT = (((1,), (1,)), ((), ()))
_TN = (((0,), (0,)), ((), ()))

bf16 = jnp.bfloat16
f32 = jnp.float32


def _params(*sem):
    return pltpu.CompilerParams(dimension_semantics=sem, vmem_limit_bytes=VMEM_LIMIT)


def _silu(z):
    return z * (1.0 / (1.0 + jnp.exp(-z)))


def _mm_kernel(xp_ref, xs_ref, w_ref, *rest, scale, w_transposed):
    op_ref, os_ref, wb_ref = rest[-3:]
    dims = _NT if w_transposed else (((1,), (0,)), ((), ()))

    def project(x_ref, o_ref):
        acc = lax.dot_general(x_ref[...], wb_ref[...], dims, preferred_element_type=f32)
        if scale != 1.0:
            acc = acc * scale
        o_ref[...] = acc.astype(o_ref.dtype)

    @pl.when(pl.program_id(1) == 0)
    def _():
        wb_ref[...] = w_ref[...].astype(bf16)
        project(xs_ref, os_ref)

    project(xp_ref, op_ref)


def _matmul(xp, xs, w, layer, col_off, n, out_dtype, scale=1.0, slots=None, slot=0, into=None,
            w_transposed=False, after=()):
    m, k = xp.shape
    ms = xs.shape[0]
    tm = min(m, 1024)
    tn = min(n, 1024)
    assert m % tm == 0 and n % tn == 0 and col_off % tn == 0
    off = col_off // tn
    if w_transposed:
        w_spec = pl.BlockSpec((None, tn, k), lambda j, i: (layer, j + off, 0))
        w_tile = (tn, k)
    else:
        w_spec = pl.BlockSpec((None, k, tn), lambda j, i: (layer, 0, j + off))
        w_tile = (k, tn)
    in_specs = [pl.BlockSpec((tm, k), lambda j, i: (i, 0)),
                pl.BlockSpec((ms, k), lambda j, i: (0, 0)),
                w_spec]
    args = [xp, xs, w]
    aliases = {}
    if slots is None:
        out_specs = [pl.BlockSpec((tm, tn), lambda j, i: (i, j)),
                     pl.BlockSpec((ms, tn), lambda j, i: (0, j))]
        out_shape = [jax.ShapeDtypeStruct((m, n), out_dtype),
                     jax.ShapeDtypeStruct((ms, n), out_dtype)]
    else:
        out_specs = [pl.BlockSpec((None, tm, tn), lambda j, i: (slot, i, j)),
                     pl.BlockSpec((None, ms, tn), lambda j, i: (slot, 0, j))]
        out_shape = [jax.ShapeDtypeStruct((slots, m, n), out_dtype),
                     jax.ShapeDtypeStruct((slots, ms, n), out_dtype)]
        if into is not None:
            in_specs += [pl.BlockSpec(memory_space=pl.ANY)] * 2
            args += list(into)
            aliases = {3: 0, 4: 1}
    in_specs += [pl.BlockSpec(memory_space=pl.ANY)] * len(after)
    args += list(after)
    return pl.pallas_call(
        functools.partial(_mm_kernel, scale=scale, w_transposed=w_transposed),
        grid=(n // tn, m // tm),
        in_specs=in_specs,
        out_specs=out_specs,
        out_shape=out_shape,
        input_output_aliases=aliases,
        scratch_shapes=[pltpu.VMEM(w_tile, bf16)],
        compiler_params=_params("parallel", "arbitrary"),
        name="proj_matmul",
    )(*args)


def _hold_kernel(*refs):
    del refs


def _hold_until(bufs, anchor):
    any_spec = pl.BlockSpec(memory_space=pl.ANY)
    return pl.pallas_call(
        _hold_kernel,
        in_specs=[any_spec] * (len(bufs) + 1),
        out_specs=[any_spec] * len(bufs),
        out_shape=[jax.ShapeDtypeStruct(b.shape, b.dtype) for b in bufs],
        input_output_aliases={i: i for i in range(len(bufs))},
        name="hold_row_buffers",
    )(*bufs, anchor)


def _out_kernel(*refs, tiles_per_batch):
    if tiles_per_batch is None:
        o_ref, w_ref, x_ref, g_ref, b_ref, xo_ref, xb_ref = refs
        o = o_ref[...].astype(bf16)
    else:
        olo_ref, ohi_ref, w_ref, x_ref, g_ref, b_ref, xo_ref, xb_ref = refs
        first_half = (pl.program_id(0) % tiles_per_batch) < tiles_per_batch // 2
        o = jnp.where(first_half, olo_ref[...], ohi_ref[...])
    tm = o.shape[0]
    group = tm // OUT_ROW_GROUPS if tm >= OUT_ROW_GROUPS * LANES else tm
    for r0 in range(0, tm, group):
        rows = slice(r0, r0 + group)
        y = jnp.dot(o[rows], w_ref[...], preferred_element_type=f32)
        r = DEEPNORM_ALPHA * x_ref[rows, :] + y
        mu = jnp.mean(r, axis=-1, keepdims=True)
        d = r - mu
        var = jnp.mean(d * d, axis=-1, keepdims=True)
        xn = d * lax.rsqrt(var + NORM_EPS) * g_ref[...] + b_ref[...]
        xo_ref[rows, :] = xn
        xb_ref[rows, :] = xn.astype(bf16)


def _out_proj_norm(o, w, x, g, b, seq=None):
    m = x.shape[0]
    tm = min(m, 512)
    row = lambda i: (i, 0)
    fixed = lambda i: (0, 0)
    if isinstance(o, (tuple, list)):
        tpb = seq // tm
        half = tpb // 2
        assert seq % tm == 0 and tpb % 2 == 0
        o_specs = [pl.BlockSpec((tm, D_MODEL),
                                lambda i: ((i // tpb) * half + jnp.minimum(i % tpb, half - 1), 0)),
                   pl.BlockSpec((tm, D_MODEL),
                                lambda i: ((i // tpb) * half + jnp.maximum(i % tpb - half, 0), 0))]
        o_args = list(o)
    else:
        tpb = None
        o_specs = [pl.BlockSpec((tm, D_MODEL), row)]
        o_args = [o]
    return pl.pallas_call(
        functools.partial(_out_kernel, tiles_per_batch=tpb),
        grid=(m // tm,),
        in_specs=o_specs + [
                  pl.BlockSpec((D_MODEL, D_MODEL), fixed),
                  pl.BlockSpec((tm, D_MODEL), row),
                  pl.BlockSpec((1, D_MODEL), fixed),
                  pl.BlockSpec((1, D_MODEL), fixed)],
        out_specs=[pl.BlockSpec((tm, D_MODEL), row), pl.BlockSpec((tm, D_MODEL), row)],
        out_shape=[jax.ShapeDtypeStruct((m, D_MODEL), f32),
                   jax.ShapeDtypeStruct((m, D_MODEL), bf16)],
        compiler_params=_params("parallel"),
        name="out_proj_deepnorm",
    )(*o_args, w, x, g.reshape(1, D_MODEL), b.reshape(1, D_MODEL))


def _bias_by_distance(rel_bias, dist):
    n = jnp.asarray(dist, jnp.int32)
    max_exact = REL_BUCKETS // 2
    nf = jnp.maximum(n, 1).astype(f32)
    large = max_exact + (jnp.log(nf / max_exact) / math.log(REL_MAX_DIST / max_exact)
                         * (REL_BUCKETS - max_exact)).astype(jnp.int32)
    large = jnp.minimum(large, REL_BUCKETS - 1)
    bucket = jnp.where(n < max_exact, n, large)
    return jnp.moveaxis(rel_bias.astype(f32)[bucket], -1, 0)


def _prompt_bias_tiles(rel_bias, blk):
    k = np.arange(3 * blk - 1)
    d = 2 * blk - 1 - k
    u = jnp.where(jnp.asarray(d >= 0), _bias_by_distance(rel_bias, np.maximum(d, 0)), MASK_VALUE)
    period = 3 * blk
    flat = jnp.tile(jnp.pad(u, ((0, 0), (0, 1))), (1, blk))[:, :blk * (period - 1)]
    return flat.reshape(DIFF_HEADS, blk, period - 1)[:, :, blk - 1:3 * blk - 1]


def _decode_bias_tables(rel_bias, t_valid):
    col = np.arange(DEC_COLS)
    col_h = (col // SAMPLE_ROWS) % DIFF_HEADS
    reps = DEC_COLS // SAMPLE_ROWS
    tq = np.arange(SAMPLE_ROWS)[None, :]
    new_tokens = NEW_ROWS // DIFF_HEADS
    tab = _bias_by_distance(rel_bias, np.arange(2 * PAGE_SIZE + SAMPLE_ROWS + 1))

    def toeplitz_rows(ext, n_tokens):
        wins = [ext[:, t + 1:t + 1 + n_tokens][:, ::-1] for t in range(SAMPLE_ROWS)]
        return jnp.transpose(jnp.stack(wins), (2, 1, 0)).reshape(n_tokens * DIFF_HEADS, SAMPLE_ROWS)

    def expand(vals, visible):
        n_rows = vals.shape[0]
        r = np.arange(n_rows)
        ok = ((r % DIFF_HEADS)[:, None] == col_h[None, :]) & np.tile(visible, (1, reps))
        return jnp.where(jnp.asarray(ok), jnp.tile(vals, (1, reps)), MASK_VALUE)

    everything = np.ones((PAGE_ROWS, SAMPLE_ROWS), bool)
    far_vals = jnp.broadcast_to(jnp.tile(tab[:, 2 * PAGE_SIZE], PAGE_SIZE)[:, None],
                                (PAGE_ROWS, SAMPLE_ROWS))
    far = expand(far_vals, everything)
    last = expand(toeplitz_rows(tab, PAGE_SIZE), everything)
    s_new = (np.arange(NEW_ROWS) // DIFF_HEADS)[:, None]
    ext = jnp.pad(tab[:, :SAMPLE_ROWS], ((0, 0), (new_tokens, 0)))
    new = expand(toeplitz_rows(ext, new_tokens), (s_new <= tq) & (s_new < t_valid))
    return jnp.stack([far, last]), new


def _lambda_value(lp_ref, lam_init):
    lp = lp_ref[...]
    a = jnp.sum(lp[0:1] * lp[1:2], axis=-1, keepdims=True)
    b = jnp.sum(lp[2:3] * lp[3:4], axis=-1, keepdims=True)
    return jnp.exp(a) - jnp.exp(b) + lam_init


def _head_epilogue(o, g, z, out_scale):
    ms = jnp.mean(o * o, axis=-1, keepdims=True)
    return (o * lax.rsqrt(ms + NORM_EPS) * g * out_scale) * _silu(z)


def _attn_step(i, qlo_ref, qhi_ref, k_ref, v_ref, zlo_ref, zhi_ref, bias_ref, lp_ref, g_ref,
               olo_ref, ohi_ref, kb_ref, vb_ref, s_ref, *, lam_init, n_blocks, between=()):
    blk = ATT_BLOCK

    @pl.when(i == 0)
    def _():
        kb_ref[...] = k_ref[...].astype(bf16)
        vb_ref[...] = v_ref[...].astype(bf16)

    lam = _lambda_value(lp_ref, lam_init)

    def lane_fold(x):
        return [x[:, c * LANES:(c + 1) * LANES] for c in range(blk // LANES)]

    def logits_and_max(q_ref, nvis, base):
        q = q_ref[...]
        qs = (q[:, :DIFF_DH], q[:, DIFF_DH:])
        mx = [None, None]
        for j in range(nvis):
            kj = kb_ref[j * blk:(j + 1) * blk, :]
            if j == nvis - 1:
                bias = bias_ref[:, blk:]
            elif j == nvis - 2:
                bias = bias_ref[:, :blk]
            else:
                bias = None
            for mp in range(2):
                s = lax.dot_general(qs[mp], kj[:, mp * DIFF_DH:(mp + 1) * DIFF_DH], _NT,
                                    preferred_element_type=f32)
                if bias is not None:
                    s = s + bias
                s_ref[mp, base + j] = s
                for part in lane_fold(s):
                    mx[mp] = part if mx[mp] is None else jnp.maximum(mx[mp], part)
        return [jnp.max(mx[mp], axis=-1, keepdims=True) for mp in range(2)]

    def exponentials(nvis, base, m):
        ls = [None, None]
        for j in range(nvis):
            for mp in range(2):
                p = jnp.exp2(s_ref[mp, base + j] - m[mp])
                s_ref[mp, base + j] = p
                for part in lane_fold(p):
                    ls[mp] = part if ls[mp] is None else ls[mp] + part
        return (1.0 / jnp.sum(ls[0], axis=-1, keepdims=True),
                lam / jnp.sum(ls[1], axis=-1, keepdims=True))

    def weighted_values(nvis, base, c, z_ref, o_ref):
        acc = None
        for j in range(nvis):
            pd = (s_ref[0, base + j] * c[0] - s_ref[1, base + j] * c[1]).astype(bf16)
            pv = jnp.dot(pd, vb_ref[j * blk:(j + 1) * blk, :], preferred_element_type=f32)
            acc = pv if acc is None else acc + pv
        o_ref[...] = _head_epilogue(acc, g_ref[...], z_ref[...], 1.0 - lam_init).astype(bf16)

    def block_pair(lo):
        pending = list(between)

        def other_work():
            if pending:
                pending.pop(0)()

        n_lo, n_hi = lo + 1, n_blocks - lo
        other_work()
        m_lo = logits_and_max(qlo_ref, n_lo, 0)
        other_work()
        m_hi = logits_and_max(qhi_ref, n_hi, n_lo)
        other_work()
        c_lo = exponentials(n_lo, 0, m_lo)
        other_work()
        c_hi = exponentials(n_hi, n_lo, m_hi)
        while pending:
            other_work()
        weighted_values(n_lo, 0, c_lo, zlo_ref, olo_ref)
        weighted_values(n_hi, n_lo, c_hi, zhi_ref, ohi_ref)

    for lo in range(n_blocks // 2):
        pl.when(i == lo)(functools.partial(block_pair, lo))


def _attn_kernel(*refs, **static):
    _attn_step(pl.program_id(2), *refs, **static)


def _prompt_specs(q, k, v, slot, z, bias_tiles, lam_p, norm_g, batch, seq, coords):
    nq = seq // ATT_BLOCK
    half = nq // 2
    blk = ATT_BLOCK

    def at(f):
        return lambda *g: f(*coords(*g))

    lo_in = at(lambda b, h, i: (b * nq + i, h))
    hi_in = at(lambda b, h, i: (b * nq + nq - 1 - i, h))
    lo_out = at(lambda b, h, i: (b * half + i, h))
    hi_out = at(lambda b, h, i: (b * half + half - 1 - i, h))
    kvmap = at(lambda b, h, i: (slot, b, h))
    out = jax.ShapeDtypeStruct((batch * seq // 2, D_MODEL), bf16)
    args = [q, q, k, v, z, z, bias_tiles, lam_p, norm_g.reshape(1, DIFF_HW)]
    in_specs = [pl.BlockSpec((blk, DIFF_HW), lo_in),
                pl.BlockSpec((blk, DIFF_HW), hi_in),
                pl.BlockSpec((None, seq, DIFF_HW), kvmap),
                pl.BlockSpec((None, seq, DIFF_HW), kvmap),
                pl.BlockSpec((blk, DIFF_HW), lo_in),
                pl.BlockSpec((blk, DIFF_HW), hi_in),
                pl.BlockSpec((None, blk, 2 * blk), at(lambda b, h, i: (h, 0, 0))),
                pl.BlockSpec((4, DIFF_DH), at(lambda b, h, i: (0, 0))),
                pl.BlockSpec((1, DIFF_HW), at(lambda b, h, i: (0, 0)))]
    out_specs = [pl.BlockSpec((blk, DIFF_HW), lo_out), pl.BlockSpec((blk, DIFF_HW), hi_out)]
    scratch = [pltpu.VMEM((seq, DIFF_HW), bf16),
               pltpu.VMEM((seq, DIFF_HW), bf16),
               pltpu.VMEM((2, nq + 1, blk, blk), f32)]
    return args, in_specs, out_specs, [out, out], scratch


def _prompt_attention(q, k, v, slot, z, bias_tiles, lam_p, norm_g, batch, seq, lam_init):
    nq = seq // ATT_BLOCK
    args, in_specs, out_specs, out_shape, scratch = _prompt_specs(
        q, k, v, slot, z, bias_tiles, lam_p, norm_g, batch, seq, lambda b, h, i: (b, h, i))
    return pl.pallas_call(
        functools.partial(_attn_kernel, lam_init=lam_init, n_blocks=nq),
        grid=(batch, DIFF_HEADS, nq // 2),
        in_specs=in_specs,
        out_specs=out_specs,
        out_shape=out_shape,
        scratch_shapes=scratch,
        compiler_params=_params("parallel", "parallel", "arbitrary"),
        name="prompt_diff_attention",
    )(*args)


DEC_INPUTS = 2 * DEC_PAGES_PER_STEP + 8


def _decode_kernel(pt_ref, *refs, **static):
    del pt_ref
    _decode_step(pl.program_id(1), refs, **static)


def _decode_step(s, refs, *, lam_init, n_steps, init=True, groups=range(DEC_GROUPS), finish=True):
    npg = DEC_PAGES_PER_STEP
    k_pages = refs[:npg]
    v_pages = refs[npg:2 * npg]
    (w_ref, kn_ref, vn_ref, z_ref, bm_ref, bmn_ref, lp_ref, g_ref, o_ref,
     m_ref, l_ref, acc_ref) = refs[2 * npg:]

    if init:
        @pl.when(s == 0)
        def _():
            m_ref[...] = jnp.full(m_ref.shape, MASK_VALUE, f32)
            l_ref[...] = jnp.zeros(l_ref.shape, f32)
            acc_ref[...] = jnp.zeros(acc_ref.shape, f32)

    w = w_ref[...]
    eye = (lax.broadcasted_iota(jnp.int32, (DEC_COLS, DEC_COLS), 0)
           == lax.broadcasted_iota(jnp.int32, (DEC_COLS, DEC_COLS), 1))

    def to_column(row):
        return jnp.sum(jnp.where(eye, row, 0.0), axis=1, keepdims=True)

    def update(grp, blocks):
        logits = [jnp.dot(k, w, preferred_element_type=f32) + bm for k, _, bm in blocks]
        cmax = None
        for lg in logits:
            c = jnp.max(lg, axis=0, keepdims=True)
            cmax = c if cmax is None else jnp.maximum(cmax, c)
        m_old = m_ref[grp]
        m_new = jnp.maximum(m_old, cmax)
        a = jnp.exp2(m_old - m_new)
        lsum = None
        pv = None
        for lg, (_, v, _) in zip(logits, blocks):
            p = jnp.exp2(lg - m_new)
            ps = jnp.sum(p, axis=0, keepdims=True)
            lsum = ps if lsum is None else lsum + ps
            d = jnp.dot(p.T.astype(bf16), v.astype(bf16), preferred_element_type=f32)
            pv = d if pv is None else pv + d
        m_ref[grp] = m_new
        l_ref[grp] = a * l_ref[grp] + lsum
        acc_ref[grp] = acc_ref[grp] * to_column(a) + pv

    per_group = npg // DEC_GROUPS
    for grp in groups:
        blocks = []
        for pg in range(grp * per_group, (grp + 1) * per_group):
            if pg == npg - 1:
                bm = bm_ref[jnp.where(s == n_steps - 1, 1, 0)]
            else:
                bm = bm_ref[0]
            blocks.append((k_pages[pg][...], v_pages[pg][...], bm))
        update(grp, blocks)

    if not finish:
        return

    @pl.when(s == n_steps - 1)
    def _():
        update(0, [(kn_ref[...], vn_ref[...], bmn_ref[...])])
        m_all = m_ref[0]
        for grp in range(1, DEC_GROUPS):
            m_all = jnp.maximum(m_all, m_ref[grp])
        l_all = None
        acc_all = None
        for grp in range(DEC_GROUPS):
            a = jnp.exp2(m_ref[grp] - m_all)
            lg = a * l_ref[grp]
            ag = acc_ref[grp] * to_column(a)
            l_all = lg if l_all is None else l_all + lg
            acc_all = ag if acc_all is None else acc_all + ag
        on = acc_all * (1.0 / to_column(l_all))
        lam = _lambda_value(lp_ref, lam_init)
        half = DEC_COLS // 2
        for h in range(DIFF_HEADS):
            r0 = h * SAMPLE_ROWS
            o = on[r0:r0 + SAMPLE_ROWS] - lam * on[half + r0:half + r0 + SAMPLE_ROWS]
            lo = h * DIFF_HW
            o_ref[:, lo:lo + DIFF_HW] = _head_epilogue(o, g_ref[...], z_ref[:, lo:lo + DIFF_HW],
                                                       1.0 - lam_init)


def _decode_specs(page_table, cache_k, cache_v, layer, q, k_new, v_new, z, bm, bm_new, lam_p, norm_g):
    nb, n_pages = page_table.shape
    npg = DEC_PAGES_PER_STEP
    ck = cache_k.reshape(cache_k.shape[0], cache_k.shape[1], PAGE_ROWS, DIFF_HW)
    cv = cache_v.reshape(cache_v.shape[0], cache_v.shape[1], PAGE_ROWS, DIFF_HW)

    q5 = q.reshape(nb, SAMPLE_ROWS, DIFF_HEADS, 2, DIFF_DH)
    qt = jnp.transpose(q5, (0, 3, 4, 2, 1)).reshape(nb, 2, DIFF_DH, DEC_COLS // 2)
    zeros = jnp.zeros_like(qt[:, 0])
    w = jnp.concatenate([jnp.concatenate([qt[:, 0], zeros], axis=-1),
                         jnp.concatenate([zeros, qt[:, 1]], axis=-1)], axis=1)

    def new_rows(a):
        a = a.reshape(nb, SAMPLE_ROWS, DIFF_HEADS, DIFF_HW)
        a = jnp.pad(a, ((0, 0), (0, NEW_ROWS // DIFF_HEADS - SAMPLE_ROWS), (0, 0), (0, 0)))
        return a.reshape(nb, NEW_ROWS, DIFF_HW)

    def page_spec(pg):
        return pl.BlockSpec((None, None, PAGE_ROWS, DIFF_HW),
                            lambda b, s, pt: (layer, pt[b, s * npg + pg], 0, 0))

    per_batch3 = lambda b, s, pt: (b, 0, 0)
    row = lambda b, s, pt: (b, 0)
    fixed2 = lambda b, s, pt: (0, 0)
    in_specs = ([page_spec(pg) for pg in range(npg)] + [page_spec(pg) for pg in range(npg)]
                + [pl.BlockSpec((None, DIFF_HW, DEC_COLS), per_batch3),
                   pl.BlockSpec((None, NEW_ROWS, DIFF_HW), per_batch3),
                   pl.BlockSpec((None, NEW_ROWS, DIFF_HW), per_batch3),
                   pl.BlockSpec((SAMPLE_ROWS, D_MODEL), row),
                   pl.BlockSpec((2, PAGE_ROWS, DEC_COLS), lambda b, s, pt: (0, 0, 0)),
                   pl.BlockSpec((NEW_ROWS, DEC_COLS), fixed2),
                   pl.BlockSpec((4, DIFF_DH), fixed2),
                   pl.BlockSpec((1, DIFF_HW), fixed2)])
    assert len(in_specs) == DEC_INPUTS
    args = [*([ck] * npg), *([cv] * npg), w, new_rows(k_new), new_rows(v_new), z, bm, bm_new,
            lam_p, norm_g.reshape(1, DIFF_HW)]
    out_specs = [pl.BlockSpec((SAMPLE_ROWS, D_MODEL), row)]
    out_shape = [jax.ShapeDtypeStruct((nb * SAMPLE_ROWS, D_MODEL), f32)]
    scratch = [pltpu.VMEM((DEC_GROUPS, 1, DEC_COLS), f32),
               pltpu.VMEM((DEC_GROUPS, 1, DEC_COLS), f32),
               pltpu.VMEM((DEC_GROUPS, DEC_COLS, DIFF_HW), f32)]
    return args, in_specs, out_specs, out_shape, scratch


def _decode_attention(page_table, *operands, lam_init):
    nb, n_pages = page_table.shape
    n_steps = n_pages // DEC_PAGES_PER_STEP
    args, in_specs, out_specs, out_shape, scratch = _decode_specs(page_table, *operands)
    return pl.pallas_call(
        functools.partial(_decode_kernel, lam_init=lam_init, n_steps=n_steps),
        grid_spec=pltpu.PrefetchScalarGridSpec(
            num_scalar_prefetch=1, grid=(nb, n_steps), in_specs=in_specs, out_specs=out_specs,
            scratch_shapes=scratch),
        out_shape=out_shape,
        compiler_params=pltpu.CompilerParams(dimension_semantics=("parallel", "arbitrary"),
                                             vmem_limit_bytes=DEC_VMEM_LIMIT),
        name="decode_diff_attention",
    )(page_table, *args)[0]


def _fused_attn_kernel(pt_ref, *refs, lam_init, n_steps, n_blocks):
    del pt_ref
    n_att_in = 9
    dec_in = refs[:DEC_INPUTS]
    att_in = refs[DEC_INPUTS:DEC_INPUTS + n_att_in]
    o_dec, o_lo, o_hi = refs[DEC_INPUTS + n_att_in:DEC_INPUTS + n_att_in + 3]
    m_ref, l_ref, acc_ref, kb_ref, vb_ref, s_ref = refs[DEC_INPUTS + n_att_in + 3:]
    s = pl.program_id(1)
    step = pl.program_id(0) * n_steps + s
    decode = functools.partial(_decode_step, s, (*dec_in, o_dec, m_ref, l_ref, acc_ref),
                               lam_init=lam_init, n_steps=n_steps)
    decode(init=True, groups=(), finish=False)
    _attn_step(step % (n_blocks // 2), *att_in, o_lo, o_hi, kb_ref, vb_ref, s_ref,
               lam_init=lam_init, n_blocks=n_blocks,
               between=[functools.partial(decode, init=False, groups=(grp,), finish=False)
                        for grp in range(DEC_GROUPS)])
    decode(init=False, groups=(), finish=True)


def _diff_attention(page_table, decode_operands, prompt_operands, batch, seq, lam_init):
    nb, n_pages = page_table.shape
    n_steps = n_pages // DEC_PAGES_PER_STEP
    nq = seq // ATT_BLOCK
    half = nq // 2
    if nb * n_steps != batch * DIFF_HEADS * half:
        o_s = _decode_attention(page_table, *decode_operands, lam_init=lam_init)
        return o_s, _prompt_attention(*prompt_operands, batch, seq, lam_init)

    def coords(b, s, pt):
        step = b * n_steps + s
        return step // (DIFF_HEADS * half), (step // half) % DIFF_HEADS, step % half

    d_args, d_in, d_out, d_shape, d_scratch = _decode_specs(page_table, *decode_operands)
    p_args, p_in, p_out, p_shape, p_scratch = _prompt_specs(*prompt_operands, batch, seq, coords)
    o_s, o_lo, o_hi = pl.pallas_call(
        functools.partial(_fused_attn_kernel, lam_init=lam_init, n_steps=n_steps, n_blocks=nq),
        grid_spec=pltpu.PrefetchScalarGridSpec(
            num_scalar_prefetch=1, grid=(nb, n_steps), in_specs=d_in + p_in,
            out_specs=d_out + p_out, scratch_shapes=d_scratch + p_scratch),
        out_shape=d_shape + p_shape,
        compiler_params=pltpu.CompilerParams(dimension_semantics=("arbitrary", "arbitrary"),
                                             vmem_limit_bytes=FUSED_VMEM_LIMIT),
        name="diff_attention_both_groups",
    )(page_table, *d_args, *p_args)
    return o_s, (o_lo, o_hi)


def _gate_kernel(x_ref, w1_ref, w2_ref, b_ref, g_ref):
    gl = lax.dot_general(x_ref[...], w1_ref[...], _NT, preferred_element_type=f32)
    u = jnp.dot(gl.astype(bf16), w2_ref[...], preferred_element_type=f32) + b_ref[...]
    ls = jnp.minimum(u, 0.0) - jnp.log(1.0 + jnp.exp(-jnp.abs(u)))
    g_ref[...] = ls * (1.0 / GLA_GATE_NORMALIZER)


def _gla_gate(x, w1, w2, b):
    m = x.shape[0]
    tm = min(m, 512)
    return pl.pallas_call(
        _gate_kernel,
        grid=(m // tm,),
        in_specs=[pl.BlockSpec((tm, D_MODEL), lambda i: (i, 0)),
                  pl.BlockSpec((LANES, D_MODEL), lambda i: (0, 0)),
                  pl.BlockSpec((LANES, GLA_DK_W), lambda i: (0, 0)),
                  pl.BlockSpec((1, GLA_DK_W), lambda i: (0, 0))],
        out_specs=pl.BlockSpec((tm, GLA_DK_W), lambda i: (i, 0)),
        out_shape=jax.ShapeDtypeStruct((m, GLA_DK_W), f32),
        compiler_params=_params("parallel"),
        name="gla_gate",
    )(x, w1, w2, b.reshape(1, GLA_DK_W))


def _gla_kernel(*refs, chunk, chunks_per_step, t_valid, n_steps, has_state):
    q_ref, k_ref, v_ref, g_ref, z_ref, ng_ref = refs[:6]
    s0_ref = refs[6] if has_state else None
    o_ref, so_ref, st_ref = refs[-3:]
    n = pl.program_id(1)

    @pl.when(n == 0)
    def _():
        for h in range(GLA_HEADS):
            if has_state:
                st_ref[h] = s0_ref[h]
            else:
                st_ref[h] = jnp.zeros(st_ref.shape[1:], f32)

    row = lax.broadcasted_iota(jnp.int32, (chunk, chunk), 0)
    col = lax.broadcasted_iota(jnp.int32, (chunk, chunk), 1)
    causal = row >= col
    tri = causal.astype(f32).astype(bf16)
    heads = range(GLA_HEADS)
    kc = [slice(h * GLA_DK, (h + 1) * GLA_DK) for h in heads]
    vc = [slice(h * GLA_DV, (h + 1) * GLA_DV) for h in heads]
    intra = []
    for c in range(chunks_per_step):
        rows = slice(c * chunk, (c + 1) * chunk)
        g = g_ref[rows, :]
        k = k_ref[rows, :]
        if t_valid < chunk:
            valid = lax.broadcasted_iota(jnp.int32, (chunk, GLA_DK_W), 0) < t_valid
            g = jnp.where(valid, g, 0.0)
            k = jnp.where(valid, k, 0.0)
        g_hi = g.astype(bf16)
        g_lo = (g - g_hi.astype(f32)).astype(bf16)
        b = (jnp.dot(tri, g_hi, preferred_element_type=f32)
             + jnp.dot(tri, g_lo, preferred_element_type=f32))
        b_last = b[chunk - 1:chunk, :]
        qe = (q_ref[rows, :] * jnp.exp(b)).astype(bf16)
        ke = (k * jnp.exp(-b)).astype(bf16)
        kd = (k * jnp.exp(b_last - b)).astype(bf16)
        vb = v_ref[rows, :].astype(bf16)
        decay = jnp.broadcast_to(jnp.exp(b_last), (SUBLANES, GLA_DK_W)).T[:, :1]
        a = [lax.dot_general(qe[:, kc[h]], ke[:, kc[h]], _NT, preferred_element_type=f32)
             for h in heads]
        a = [jnp.where(causal, a[h], 0.0).astype(bf16) for h in heads]
        o_intra = [jnp.dot(a[h], vb[:, vc[h]], preferred_element_type=f32) for h in heads]
        intra.append((rows, qe, kd, vb, decay, o_intra))
    st = [st_ref[h] for h in heads]
    for rows, qe, kd, vb, decay, o_intra in intra:
        o = [o_intra[h] + jnp.dot(qe[:, kc[h]], st[h].astype(bf16), preferred_element_type=f32)
             for h in heads]
        st = [decay[kc[h]] * st[h] + lax.dot_general(kd[:, kc[h]], vb[:, vc[h]], _TN,
                                                     preferred_element_type=f32) for h in heads]
        for h in heads:
            o_ref[rows, vc[h]] = _head_epilogue(o[h], ng_ref[...], z_ref[rows, vc[h]],
                                                1.0).astype(o_ref.dtype)
    for h in heads:
        st_ref[h] = st[h]

    @pl.when(n == n_steps - 1)
    def _():
        for h in range(GLA_HEADS):
            so_ref[h] = st_ref[h]


def _gla(q, k, v, g, z, norm_g, s0, layer, n_layers, state_out, batch, seq, t_valid):
    chunk = min(GLA_CHUNK, seq)
    per_step = GLA_CHUNKS_PER_STEP if seq % (GLA_CHUNKS_PER_STEP * chunk) == 0 else 1
    rows = per_step * chunk
    nc = seq // rows
    has_state = s0 is not None
    out_dtype = bf16 if chunk % (2 * SUBLANES) == 0 else f32
    tmap = lambda b, n: (b * nc + n, 0)
    state_spec = pl.BlockSpec((None, None, GLA_HEADS, GLA_DK, GLA_DV),
                              lambda b, n: (layer, b, 0, 0, 0))
    in_specs = [pl.BlockSpec((rows, GLA_DK_W), tmap),
                pl.BlockSpec((rows, GLA_DK_W), tmap),
                pl.BlockSpec((rows, D_MODEL), tmap),
                pl.BlockSpec((rows, GLA_DK_W), tmap),
                pl.BlockSpec((rows, D_MODEL), tmap),
                pl.BlockSpec((1, GLA_DV), lambda b, n: (0, 0))]
    args = [q, k, v, g, z, norm_g.reshape(1, GLA_DV)]
    if has_state:
        in_specs.append(state_spec)
        args.append(s0)
    aliases = {}
    if state_out is not None:
        aliases = {len(args): 1}
        in_specs.append(pl.BlockSpec(memory_space=pl.ANY))
        args.append(state_out)
    return pl.pallas_call(
        functools.partial(_gla_kernel, chunk=chunk, chunks_per_step=per_step, t_valid=t_valid,
                          n_steps=nc, has_state=has_state),
        grid=(batch, nc),
        in_specs=in_specs,
        out_specs=[pl.BlockSpec((rows, D_MODEL), tmap), state_spec],
        out_shape=[jax.ShapeDtypeStruct((batch * seq, D_MODEL), out_dtype),
                   jax.ShapeDtypeStruct((n_layers, batch, GLA_HEADS, GLA_DK, GLA_DV), f32)],
        input_output_aliases=aliases,
        scratch_shapes=[pltpu.VMEM((GLA_HEADS, GLA_DK, GLA_DV), f32)],
        compiler_params=_params("parallel", "arbitrary"),
        name="gla_chunked",
    )(*args)


def _trunk(xp, xs, batch, seq, dec_batch, dec_seq, cache_k, cache_v, state_gla, page_table, weights):
    (rel_bias, diff_w_in, diff_lambda, diff_norm_g, diff_w_out, gla_w_in, gla_w_g1, gla_w_g2,
     gla_b_g, gla_norm_g, gla_w_out, ln_g, ln_b) = weights
    n_diff = (DEPTH + 1) // 2
    xpb = xp.astype(bf16)
    xsb = xs.astype(bf16)
    kbuf = vbuf = None
    n_gla = DEPTH // 2
    last_diff = 2 * (n_diff - 1)
    states_p = states_s = None
    far = _bias_by_distance(rel_bias, np.array([2 * ATT_BLOCK]))[:, 0]
    tiles = (_prompt_bias_tiles(rel_bias, ATT_BLOCK) - far[:, None, None]) * LOG2E
    bm, bm_new = [t * LOG2E for t in _decode_bias_tables(rel_bias, dec_seq)]
    q_scale = DIFF_DH ** -0.5 * LOG2E
    for i in range(DEPTH):
        j = i // 2
        if i % 2 == 0:
            w = diff_w_in
            qp, qs = _matmul(xpb, xsb, w, j, 0, D_MODEL, bf16, scale=q_scale)
            zp, zs = _matmul(xpb, xsb, w, j, 3 * D_MODEL, D_MODEL, f32, after=(qp,))
            kbuf = _matmul(xpb, xsb, w, j, D_MODEL, D_MODEL, f32, slots=n_diff, slot=j, into=kbuf,
                           after=(zp,))
            vbuf = _matmul(xpb, xsb, w, j, 2 * D_MODEL, D_MODEL, f32, slots=n_diff, slot=j,
                           into=vbuf, after=(kbuf[0],))
            lam_init = 0.8 - 0.6 * math.exp(-0.3 * i)
            os_, op = _diff_attention(
                page_table,
                (cache_k, cache_v, j, qs, kbuf[1][j], vbuf[1][j], zs, bm, bm_new, diff_lambda[j],
                 diff_norm_g[j]),
                (qp, kbuf[0], vbuf[0], j, zp, tiles, diff_lambda[j], diff_norm_g[j]),
                batch, seq, lam_init)
            w_out = diff_w_out[j]
        else:
            w = gla_w_in
            proj = functools.partial(_matmul, xpb, xsb, w, j, w_transposed=True)
            qp, qs = proj(0, GLA_DK_W, f32, scale=GLA_DK ** -0.5)
            kp, ks = proj(GLA_DK_W, GLA_DK_W, f32)
            vp, vs = proj(2 * GLA_DK_W, D_MODEL, f32)
            zp, zs = proj(2 * GLA_DK_W + D_MODEL, D_MODEL, f32)
            if i == last_diff + 1:
                vbuf = (_hold_until([vbuf[0]], zp)[0], vbuf[1])
            gp = _gla_gate(xpb, gla_w_g1[j], gla_w_g2[j], gla_b_g[j])
            gs = _gla_gate(xsb, gla_w_g1[j], gla_w_g2[j], gla_b_g[j])
            op, states_p = _gla(qp, kp, vp, gp, zp, gla_norm_g[j], None, j, n_gla, states_p,
                                batch, seq, seq)
            os_, states_s = _gla(qs, ks, vs, gs, zs, gla_norm_g[j], state_gla, j, n_gla, states_s,
                                 dec_batch, SAMPLE_ROWS, dec_seq)
            w_out = gla_w_out[j]
        xp, xpb = _out_proj_norm(op, w_out, xp, ln_g[i], ln_b[i], seq=seq)
        xs, xsb = _out_proj_norm(os_, w_out, xs, ln_g[i], ln_b[i])
        if i == last_diff:
            kbuf = (_hold_until([kbuf[0]], xp)[0], kbuf[1])
            if last_diff + 1 >= DEPTH:
                vbuf = (_hold_until([vbuf[0]], xp)[0], vbuf[1])
    return xp, xs, kbuf, vbuf, states_p, states_s


def kernel(x_prompt, x_sample, cache_k, cache_v, state_gla, page_table, rel_bias, diff_w_in,
           diff_lambda, diff_norm_g, diff_w_out, gla_w_in, gla_w_g2, gla_b_g, gla_norm_g, gla_w_out,
           ln_g, ln_b):
    batch, seq, _ = x_prompt.shape
    dec_batch, dec_seq, _ = x_sample.shape
    n_pages = page_table.shape[1]
    assert n_pages >= 2 and n_pages % DEC_PAGES_PER_STEP == 0 and dec_seq <= SAMPLE_ROWS
    assert cache_k.shape[2:] == (PAGE_SIZE, DIFF_HEADS, DIFF_HW)
    assert seq % (4 * ATT_BLOCK) == 0 and seq % GLA_CHUNK == 0

    gate_off = 2 * GLA_DK_W + 2 * D_MODEL
    gla_w_t = jnp.swapaxes(gla_w_in, 1, 2)
    w_g1 = jnp.pad(gla_w_t[:, gate_off:, :], ((0, 0), (0, LANES - GLA_GATE_RANK), (0, 0)))
    w_g2 = jnp.pad(gla_w_g2, ((0, 0), (0, LANES - GLA_GATE_RANK), (0, 0)))
    weights = (rel_bias, diff_w_in, diff_lambda, diff_norm_g, diff_w_out.astype(bf16),
               gla_w_t, w_g1.astype(bf16), w_g2.astype(bf16), gla_b_g,
               gla_norm_g,
               gla_w_out.astype(bf16), ln_g, ln_b)

    rows = SAMPLE_ROWS
    xs = jnp.pad(x_sample, ((0, 0), (0, rows - dec_seq), (0, 0))).reshape(dec_batch * rows, D_MODEL)
    y_p, y_s, kbuf, vbuf, s_p, s_s = _trunk(
        x_prompt.reshape(batch * seq, D_MODEL), xs, batch, seq, dec_batch, dec_seq,
        cache_k, cache_v, state_gla, page_table, weights)

    def rows_out(a, b, t, t_keep):
        return a.reshape(a.shape[0], b, t, DIFF_HEADS, DIFF_HW)[:, :, :t_keep]

    return (y_p.reshape(batch, seq, D_MODEL),
            y_s.reshape(dec_batch, rows, D_MODEL)[:, :dec_seq],
            rows_out(kbuf[0], batch, seq, seq), rows_out(vbuf[0], batch, seq, seq), s_p,
            rows_out(kbuf[1], dec_batch, rows, dec_seq), rows_out(vbuf[1], dec_batch, rows, dec_seq),
            s_s)
```
